```python
import jax, jax.numpy as jnp
from jax import lax
import numpy as np

D_MODEL = 2048
BATCH = 8
SEQ = 4096
DEPTH = 4

N_MIXERS = 4
N_SB = len(range(0, DEPTH, N_MIXERS))
N_GDN = len(range(1, DEPTH, N_MIXERS))
N_DSW = len(range(2, DEPTH, N_MIXERS))
N_LRU = len(range(3, DEPTH, N_MIXERS))

HEAD_DIM = 128
NORM_EPS = 1e-6
Q_BLOCK = 128

SB_HEADS = D_MODEL // HEAD_DIM

GDN_K_HEADS = D_MODEL // HEAD_DIM
GDN_V_HEADS = 2 * GDN_K_HEADS
GDN_KEY_DIM = GDN_K_HEADS * HEAD_DIM
GDN_VAL_DIM = GDN_V_HEADS * HEAD_DIM
GDN_CONV = 4
GDN_CHUNK = 64
GDN_IN = 2 * GDN_KEY_DIM + 2 * GDN_VAL_DIM + 2 * GDN_V_HEADS

DSW_GROUPS = ((128, 1), (512, 4), (2048, 16))
DSW_HEADS_PER_GROUP = 6
DSW_HEADS = len(DSW_GROUPS) * DSW_HEADS_PER_GROUP
DSW_BLOCK = 128
ROPE_DIM = HEAD_DIM // 4
ROPE_THETA = 500000.0

LRU_WIDTH = D_MODEL
LRU_BLOCK_DIM = 256
LRU_BLOCKS = LRU_WIDTH // LRU_BLOCK_DIM
LRU_CONV = 4
LRU_C = 8.0

FFN_HIDDEN = ((8 * D_MODEL + 3 * 256 - 1) // (3 * 256)) * 256

kernel_name = 'hybrid_sb_gdn_dilated_rglru_trunk'


def rms_norm(x, g):
    xf = x.astype(jnp.float32)
    y = xf * lax.rsqrt(jnp.mean(xf * xf, axis=-1, keepdims=True) + NORM_EPS)
    return (y * g.astype(jnp.float32)).astype(x.dtype)


def l2_norm(x):
    xf = x.astype(jnp.float32)
    return xf * lax.rsqrt(jnp.sum(xf * xf, axis=-1, keepdims=True) + NORM_EPS)


def causal_dwconv(x, w):
    K, C = w.shape
    return lax.conv_general_dilated(
        x, w[:, None, :], window_strides=(1,), padding=((K - 1, 0),),
        dimension_numbers=('NWC', 'WIO', 'NWC'), feature_group_count=C)


def partial_rope(x, positions):
    half = ROPE_DIM // 2
    inv_freq = ROPE_THETA ** (-jnp.arange(half, dtype=jnp.float32) / half)
    ang = positions.astype(jnp.float32)[..., None] * inv_freq
    cos = jnp.cos(ang)[:, :, None, :]
    sin = jnp.sin(ang)[:, :, None, :]
    xr = x[..., :ROPE_DIM].astype(jnp.float32)
    x1, x2 = xr[..., :half], xr[..., half:]
    rot = jnp.concatenate([x1 * cos - x2 * sin, x2 * cos + x1 * sin], axis=-1)
    return jnp.concatenate([rot.astype(x.dtype), x[..., ROPE_DIM:]], axis=-1)


def swiglu(h, w_gu, w_down):
    gate, up = jnp.split(h @ w_gu, 2, axis=-1)
    return (jax.nn.silu(gate) * up) @ w_down


def stick_breaking_mixer(h, w_in, q_norm, k_norm, w_out):
    B, T, _ = h.shape
    H, dh = SB_HEADS, HEAD_DIM
    q, k, v = jnp.split(h @ w_in, 3, axis=-1)
    q = rms_norm(q.reshape(B, T, H, dh), q_norm)
    k = rms_norm(k.reshape(B, T, H, dh), k_norm)
    v = v.reshape(B, T, H, dh)
    nb = T // Q_BLOCK
    qb = q.reshape(B, nb, Q_BLOCK, H, dh).transpose(1, 0, 3, 2, 4)
    key_pos = jnp.arange(T)
    scale = HEAD_DIM ** -0.5

    def block(args):
        q_blk, n = args
        z = jnp.einsum('bhqd,bshd->bhqs', q_blk, k, preferred_element_type=jnp.float32) * scale
        q_pos = n * Q_BLOCK + jnp.arange(Q_BLOCK)
        past = key_pos[None, :] < q_pos[:, None]
        neg_log_1m_beta = jnp.where(past, jax.nn.softplus(z), 0.0)
        between = lax.cumsum(neg_log_1m_beta, axis=3, reverse=True) - neg_log_1m_beta
        log_w = jax.nn.log_sigmoid(z) - between
        w = jnp.where(past, jnp.exp(log_w), 0.0)
        return jnp.einsum('bhqs,bshd->bqhd', w.astype(v.dtype), v)

    o = lax.map(block, (qb, jnp.arange(nb)))
    o = o.transpose(1, 0, 2, 3, 4).reshape(B, T, H * dh)
    return o @ w_out


def chunk_gated_delta_rule(q, k, v, g, beta):
    B, T, H, dk = k.shape
    dv = v.shape[-1]
    C = GDN_CHUNK
    N = T // C
    f32 = jnp.float32

    def chunks(x):
        return x.astype(f32).reshape(B, N, C, H, -1).transpose(1, 0, 3, 2, 4)

    q, k, v = chunks(q), chunks(k), chunks(v)
    beta = chunks(beta[..., None])[..., 0]
    g = jnp.cumsum(chunks(g[..., None])[..., 0], axis=-1)
    causal = jnp.tril(jnp.ones((C, C), bool))
    strict = jnp.tril(jnp.ones((C, C), bool), -1)
    diff = g[..., :, None] - g[..., None, :]
    decay = jnp.where(causal, jnp.exp(jnp.where(causal, diff, 0.0)), 0.0)
    kb = k * beta[..., None]
    vb = v * beta[..., None]
    a_mat = jnp.where(strict, jnp.einsum('nbhcd,nbhsd->nbhcs', kb, k) * decay, 0.0)
    eye = jnp.eye(C, dtype=f32)
    t_mat = lax.linalg.triangular_solve(a_mat + eye, jnp.broadcast_to(eye, a_mat.shape),
                                        left_side=True, lower=True, unit_diagonal=True)
    u = t_mat @ vb
    w = t_mat @ (kb * jnp.exp(g)[..., None])

    def step(state, inp):
        q_c, k_c, u_c, w_c, g_c, decay_c = inp
        v_new = u_c - w_c @ state
        attn = jnp.einsum('bhcd,bhsd->bhcs', q_c, k_c) * decay_c
        out = (q_c * jnp.exp(g_c)[..., None]) @ state + attn @ v_new
        g_last = g_c[..., -1:]
        k_dec = k_c * jnp.exp(g_last - g_c)[..., None]
        state = state * jnp.exp(g_last)[..., None] + jnp.einsum('bhcd,bhce->bhde', k_dec, v_new)
        return state, out

    state0 = jnp.zeros((B, H, dk, dv), f32)
    _, out = lax.scan(step, state0, (q, k, u, w, g, decay))
    return out.transpose(1, 0, 3, 2, 4).reshape(B, T, H, dv)


def gated_deltanet_mixer(h, w_in, conv_w, a_log, dt_bias, o_norm, w_out):
    B, T, _ = h.shape
    Kd, Vd, Hk, Hv, dh = GDN_KEY_DIM, GDN_VAL_DIM, GDN_K_HEADS, GDN_V_HEADS, HEAD_DIM
    proj = h @ w_in
    qkv, z, b, a = jnp.split(proj, [2 * Kd + Vd, 2 * Kd + 2 * Vd, 2 * Kd + 2 * Vd + Hv], axis=-1)
    qkv = jax.nn.silu(causal_dwconv(qkv, conv_w))
    q, k, v = jnp.split(qkv, [Kd, 2 * Kd], axis=-1)
    rep = Hv // Hk
    q = jnp.repeat(q.reshape(B, T, Hk, dh), rep, axis=2)
    k = jnp.repeat(k.reshape(B, T, Hk, dh), rep, axis=2)
    v = v.reshape(B, T, Hv, dh)
    q = l2_norm(q) * dh ** -0.5
    k = l2_norm(k)
    beta = jax.nn.sigmoid(b.astype(jnp.float32))
    g = -jnp.exp(a_log.astype(jnp.float32)) * jax.nn.softplus(
        a.astype(jnp.float32) + dt_bias.astype(jnp.float32))
    o = chunk_gated_delta_rule(q, k, v, g, beta)
    o = rms_norm(o, o_norm) * jax.nn.silu(z.reshape(B, T, Hv, dh).astype(jnp.float32))
    return o.astype(h.dtype).reshape(B, T, Vd) @ w_out


def dilated_band_attention(q, k, v, window, dilation):
    B, T, H, dh = q.shape
    blk = DSW_BLOCK
    span = window // dilation
    unit = dilation * blk
    t_pad = -(-T // unit) * unit
    sub_len = t_pad // dilation
    nb = sub_len // blk

    def gather(x):
        x = jnp.pad(x, ((0, 0), (0, t_pad - T), (0, 0), (0, 0)))
        x = x.reshape(B, sub_len, dilation, H, dh).transpose(0, 2, 1, 3, 4)
        return x.reshape(B, dilation, nb, blk, H, dh)

    def with_previous_block(x):
        prev = jnp.pad(x[:, :, :-1], ((0, 0), (0, 0), (1, 0), (0, 0), (0, 0), (0, 0)))
        return jnp.concatenate([prev, x], axis=3)

    qg = gather(q)
    kw = with_previous_block(gather(k))
    vw = with_previous_block(gather(v))
    s = jnp.einsum('brnqhd,brnkhd->brnhqk', qg, kw, preferred_element_type=jnp.float32) * dh ** -0.5
    qi = jnp.arange(blk)[:, None]
    kj = jnp.arange(2 * blk)[None, :]
    steps = blk + qi - kj
    band = (steps >= 0) & (steps <= span)
    key_sub = jnp.arange(nb)[:, None, None] * blk + (kj - blk)[None]
    mask = band[None] & (key_sub >= 0)
    s = jnp.where(mask[:, None], s, -jnp.inf)
    m = jnp.max(s, axis=-1, keepdims=True)
    p = jnp.exp(s - m)
    den = jnp.sum(p, axis=-1, keepdims=True)
    o = jnp.einsum('brnhqk,brnkhd->brnqhd', (p / den).astype(v.dtype), vw)
    lse = (m + jnp.log(den))[..., 0]
    o = o.reshape(B, dilation, sub_len, H, dh).transpose(0, 2, 1, 3, 4).reshape(B, t_pad, H, dh)[:, :T]
    lse = lse.transpose(0, 1, 2, 4, 3).reshape(B, dilation, sub_len, H)
    lse = lse.transpose(0, 2, 1, 3).reshape(B, t_pad, H)[:, :T]
    return o, lse


def dilated_window_mixer(h, positions, w_in, q_norm, k_norm, w_out):
    B, T, _ = h.shape
    G, Hg, dh = len(DSW_GROUPS), DSW_HEADS_PER_GROUP, HEAD_DIM
    q, k, v = jnp.split(h @ w_in, 3, axis=-1)
    q = partial_rope(rms_norm(q.reshape(B, T, G * Hg, dh), q_norm), positions)
    k = partial_rope(rms_norm(k.reshape(B, T, G * Hg, dh), k_norm), positions)
    v = v.reshape(B, T, G * Hg, dh)
    outs, lses = [], []
    for gi, (window, dilation) in enumerate(DSW_GROUPS):
        sl = slice(gi * Hg, (gi + 1) * Hg)
        o_g, lse_g = dilated_band_attention(q[:, :, sl], k[:, :, sl], v[:, :, sl], window, dilation)
        outs.append(o_g)
        lses.append(lse_g)
    o = jnp.stack(outs, axis=2)
    alpha = jax.nn.softmax(jnp.stack(lses, axis=2), axis=2)
    o = (o * alpha[..., None].astype(o.dtype)).reshape(B, T, G * Hg * dh)
    return o @ w_out


def rglru_mixer(h, w_in, conv_w, conv_b, w_a, b_a, w_x, b_x, lam, w_out):
    B, T, _ = h.shape
    gate, xr = jnp.split(h @ w_in, 2, axis=-1)
    gate = jax.nn.gelu(gate, approximate=True)
    xr = causal_dwconv(xr, conv_w) + conv_b
    xb = xr.reshape(B, T, LRU_BLOCKS, LRU_BLOCK_DIM)
    r = jax.nn.sigmoid(jnp.einsum('btni,nij->btnj', xb, w_a) + b_a).reshape(B, T, LRU_WIDTH)
    i = jax.nn.sigmoid(jnp.einsum('btni,nij->btnj', xb, w_x) + b_x).reshape(B, T, LRU_WIDTH)
    log_a = -LRU_C * r.astype(jnp.float32) * jax.nn.softplus(-lam.astype(jnp.float32))
    a = jnp.exp(log_a)
    u = jnp.sqrt(-jnp.expm1(2.0 * log_a)) * (i * xr).astype(jnp.float32)

    def combine(left, right):
        a_l, b_l = left
        a_r, b_r = right
        return a_l * a_r, a_r * b_l + b_r

    _, hs = lax.associative_scan(combine, (a, u), axis=1)
    return (hs.astype(h.dtype) * gate) @ w_out


def setup_inputs(seed: int = 0) -> dict:
    key = jax.random.key(seed)
    ks = iter(jax.random.split(key, 48))
    f32 = jnp.float32

    def dense(shape, fan_in):
        return jax.random.normal(next(ks), shape, f32) * fan_in ** -0.5

    def gain(shape):
        return 1.0 + 0.02 * jax.random.normal(next(ks), shape, f32)

    def small(shape):
        return 0.01 * jax.random.normal(next(ks), shape, f32)

    x = jax.random.normal(next(ks), (BATCH, SEQ, D_MODEL), f32)
    offsets = jax.random.randint(next(ks), (BATCH, 1), 0, SEQ, jnp.int32)
    positions = offsets + jnp.arange(SEQ, dtype=jnp.int32)[None, :]

    dt = jnp.exp(jax.random.uniform(next(ks), (N_GDN, GDN_V_HEADS), f32, np.log(1e-3), np.log(1e-1)))
    gdn_dt_bias = dt + jnp.log(-jnp.expm1(-dt))
    gdn_a_log = jnp.log(jax.random.uniform(next(ks), (N_GDN, GDN_V_HEADS), f32, 1.0, 16.0))
    a_c = jax.random.uniform(next(ks), (N_LRU, LRU_WIDTH), f32, 0.9, 0.999)
    s_lam = a_c ** (1.0 / LRU_C)
    lru_lambda = jnp.log(s_lam) - jnp.log1p(-s_lam)

    return {
        'x': x,
        'positions': positions,
        'mix_norm': gain((DEPTH, D_MODEL)),
        'ffn_norm': gain((DEPTH, D_MODEL)),
        'ffn_w_gu': dense((DEPTH, D_MODEL, 2 * FFN_HIDDEN), D_MODEL),
        'ffn_w_down': dense((DEPTH, FFN_HIDDEN, D_MODEL), FFN_HIDDEN),
        'sb_w_in': dense((N_SB, D_MODEL, 3 * SB_HEADS * HEAD_DIM), D_MODEL),
        'sb_q_norm': gain((N_SB, HEAD_DIM)),
        'sb_k_norm': gain((N_SB, HEAD_DIM)),
        'sb_w_out': dense((N_SB, SB_HEADS * HEAD_DIM, D_MODEL), SB_HEADS * HEAD_DIM),
        'gdn_w_in': dense((N_GDN, D_MODEL, GDN_IN), D_MODEL),
        'gdn_conv_w': dense((N_GDN, GDN_CONV, 2 * GDN_KEY_DIM + GDN_VAL_DIM), GDN_CONV),
        'gdn_a_log': gdn_a_log,
        'gdn_dt_bias': gdn_dt_bias,
        'gdn_o_norm': gain((N_GDN, HEAD_DIM)),
        'gdn_w_out': dense((N_GDN, GDN_VAL_DIM, D_MODEL), GDN_VAL_DIM),
        'dsw_w_in': dense((N_DSW, D_MODEL, 3 * DSW_HEADS * HEAD_DIM), D_MODEL),
        'dsw_q_norm': gain((N_DSW, HEAD_DIM)),
        'dsw_k_norm': gain((N_DSW, HEAD_DIM)),
        'dsw_w_out': dense((N_DSW, DSW_HEADS * HEAD_DIM, D_MODEL), DSW_HEADS * HEAD_DIM),
        'lru_w_in': dense((N_LRU, D_MODEL, 2 * LRU_WIDTH), D_MODEL),
        'lru_conv_w': dense((N_LRU, LRU_CONV, LRU_WIDTH), LRU_CONV),
        'lru_conv_b': small((N_LRU, LRU_WIDTH)),
        'lru_w_a': dense((N_LRU, LRU_BLOCKS, LRU_BLOCK_DIM, LRU_BLOCK_DIM), LRU_BLOCK_DIM),
        'lru_b_a': small((N_LRU, LRU_BLOCKS, LRU_BLOCK_DIM)),
        'lru_w_x': dense((N_LRU, LRU_BLOCKS, LRU_BLOCK_DIM, LRU_BLOCK_DIM), LRU_BLOCK_DIM),
        'lru_b_x': small((N_LRU, LRU_BLOCKS, LRU_BLOCK_DIM)),
        'lru_lambda': lru_lambda,
        'lru_w_out': dense((N_LRU, LRU_WIDTH, D_MODEL), LRU_WIDTH),
    }


def reference(x, positions, mix_norm, ffn_norm, ffn_w_gu, ffn_w_down,
              sb_w_in, sb_q_norm, sb_k_norm, sb_w_out,
              gdn_w_in, gdn_conv_w, gdn_a_log, gdn_dt_bias, gdn_o_norm, gdn_w_out,
              dsw_w_in, dsw_q_norm, dsw_k_norm, dsw_w_out,
              lru_w_in, lru_conv_w, lru_conv_b, lru_w_a, lru_b_a, lru_w_x, lru_b_x,
              lru_lambda, lru_w_out):
    for i in range(DEPTH):
        kind, j = i % N_MIXERS, i // N_MIXERS
        h = rms_norm(x, mix_norm[i])
        if kind == 0:
            y = stick_breaking_mixer(h, sb_w_in[j], sb_q_norm[j], sb_k_norm[j], sb_w_out[j])
        elif kind == 1:
            y = gated_deltanet_mixer(h, gdn_w_in[j], gdn_conv_w[j], gdn_a_log[j], gdn_dt_bias[j],
                                     gdn_o_norm[j], gdn_w_out[j])
        elif kind == 2:
            y = dilated_window_mixer(h, positions, dsw_w_in[j], dsw_q_norm[j], dsw_k_norm[j], dsw_w_out[j])
        else:
            y = rglru_mixer(h, lru_w_in[j], lru_conv_w[j], lru_conv_b[j], lru_w_a[j], lru_b_a[j],
                            lru_w_x[j], lru_b_x[j], lru_lambda[j], lru_w_out[j])
        x = x + y.astype(x.dtype)
        x = x + swiglu(rms_norm(x, ffn_norm[i]), ffn_w_gu[i], ffn_w_down[i]).astype(x.dtype)
    return x
```

```python
import functools
import math

import jax
import jax.numpy as jnp
from jax import lax
from jax.experimental import pallas as pl
from jax.experimental.pallas import tpu as pltpu

F32 = jnp.float32
BF16 = jnp.bfloat16

NORM_EPS = 1e-6
HEAD_DIM = 128
LANES = 128
SUBLANES = 8
VMEM_LIMIT_BYTES = 56 * 1024 * 1024

GDN_CONV = 4
GDN_CHUNK = 64
DSW_GROUPS = ((128, 1), (512, 4), (2048, 16))
DSW_HEADS_PER_GROUP = 6
DSW_BLOCK = 128
ROPE_DIM = HEAD_DIM // 4
ROPE_THETA = 500000.0
LRU_BLOCK_DIM = 256
LRU_C = 8.0
NEG_BIG = -1e30


def _pick(dim, pref, align):
    if dim <= pref:
        return dim
    t = (pref // align) * align
    while t >= align:
        if dim % t == 0:
            return t
        t -= align
    return dim


def _params(*sem):
    return pltpu.CompilerParams(dimension_semantics=sem, vmem_limit_bytes=VMEM_LIMIT_BYTES)


def _softplus(x):
    return jnp.maximum(x, 0.0) + jnp.log(1.0 + jnp.exp(-jnp.abs(x)))


def _sigmoid(x):
    return 1.0 / (1.0 + jnp.exp(-x))


def _silu(x):
    return x * _sigmoid(x)


def _rms(x, g):
    ms = jnp.mean(x * x, axis=-1, keepdims=True)
    return x * lax.rsqrt(ms + NORM_EPS) * g


def _norm_matmul_kernel(*refs, epilogue, n_extra):
    x_ref, g_ref, w_ref = refs[:3]
    extra = refs[3:3 + n_extra]
    o_ref, xn_ref = refs[3 + n_extra], refs[4 + n_extra]

    @pl.when(pl.program_id(1) == 0)
    def _():
        xn_ref[...] = _rms(x_ref[...], g_ref[...]).astype(BF16)

    y = jnp.dot(xn_ref[...], w_ref[...], preferred_element_type=F32)
    if epilogue is not None:
        y = epilogue(y, *[e[...] for e in extra])
    o_ref[...] = y.astype(o_ref.dtype)


def norm_matmul(x, g, w, *, col0=0, ncols=None, epilogue=None, row_extras=(), const_extras=(),
                out_dtype=F32, tm=1024, tn=1024, name="norm_matmul"):
    M, K = x.shape
    ncols = w.shape[1] - col0 if ncols is None else ncols
    tm = _pick(M, tm, SUBLANES)
    tn = _pick(ncols, tn, LANES)
    assert col0 % tn == 0 and ncols % tn == 0 and M % tm == 0
    off = col0 // tn
    in_specs = [
        pl.BlockSpec((tm, K), lambda i, j: (i, 0)),
        pl.BlockSpec((1, K), lambda i, j: (0, 0)),
        pl.BlockSpec((K, tn), lambda i, j: (0, j + off)),
    ]
    args = [x, g.reshape(1, K), w]
    for e in row_extras:
        in_specs.append(pl.BlockSpec((tm, e.shape[1]), lambda i, j: (i, 0)))
        args.append(e)
    for e in const_extras:
        in_specs.append(pl.BlockSpec(e.shape, lambda i, j: (0, 0)))
        args.append(e)
    n_extra = len(row_extras) + len(const_extras)
    return pl.pallas_call(
        functools.partial(_norm_matmul_kernel, epilogue=epilogue, n_extra=n_extra),
        grid=(M // tm, ncols // tn),
        in_specs=in_specs,
        out_specs=pl.BlockSpec((tm, tn), lambda i, j: (i, j)),
        out_shape=jax.ShapeDtypeStruct((M, ncols), out_dtype),
        scratch_shapes=[pltpu.VMEM((tm, K), BF16)],
        compiler_params=_params("parallel", "arbitrary"),
        name=name,
    )(*args)


def _ep_headnorm(scale, rope):
    half = ROPE_DIM // 2

    def ep(y, *extra):
        if rope:
            cos_t, sin_t, gain = extra
            lane = lax.broadcasted_iota(jnp.int32, cos_t.shape, 1)
        else:
            (gain,) = extra
        outs = []
        for h in range(y.shape[1] // HEAD_DIM):
            yh = y[:, h * HEAD_DIM:(h + 1) * HEAD_DIM]
            ms = jnp.mean(yh * yh, axis=-1, keepdims=True)
            yh = yh * lax.rsqrt(ms + NORM_EPS) * gain
            if rope:
                partner = jnp.where(lane < half, pltpu.roll(yh, HEAD_DIM - half, 1),
                                    pltpu.roll(yh, half, 1))
                yh = yh * cos_t + partner * sin_t
            outs.append(yh * scale if scale != 1.0 else yh)
        return outs[0] if len(outs) == 1 else jnp.concatenate(outs, axis=1)

    return ep


def _ep_gelu(y):
    c = math.sqrt(2.0 / math.pi)
    return 0.5 * y * (1.0 + jnp.tanh(c * (y + 0.044715 * (y * y * y))))


def _ep_gdn_gates(n_heads):
    def ep(y, a_log, dt_bias):
        lane = lax.broadcasted_iota(jnp.int32, y.shape, 1)
        beta = _sigmoid(y)
        g = -jnp.exp(a_log) * _softplus(y + dt_bias)
        return jnp.where(lane < n_heads, beta, g)

    return ep


def _matmul_res_kernel(a_ref, w_ref, r_ref, o_ref):
    y = jnp.dot(a_ref[...].astype(BF16), w_ref[...], preferred_element_type=F32)
    o_ref[...] = r_ref[...] + y


def matmul_residual(a, w, res, *, tm=1024, tn=512, name="out_proj"):
    M, K = a.shape
    N = w.shape[1]
    tm = _pick(M, tm, SUBLANES)
    tn = _pick(N, tn, LANES)
    return pl.pallas_call(
        _matmul_res_kernel,
        grid=(M // tm, N // tn),
        in_specs=[
            pl.BlockSpec((tm, K), lambda i, j: (i, 0)),
            pl.BlockSpec((K, tn), lambda i, j: (0, j)),
            pl.BlockSpec((tm, tn), lambda i, j: (i, j)),
        ],
        out_specs=pl.BlockSpec((tm, tn), lambda i, j: (i, j)),
        out_shape=jax.ShapeDtypeStruct((M, N), F32),
        compiler_params=_params("parallel", "arbitrary"),
        name=name,
    )(a, w, res)


def _ffn_kernel(x_ref, g_ref, wg_ref, wu_ref, wd_ref, o_ref, xn_ref):
    @pl.when(pl.program_id(1) == 0)
    def _():
        x = x_ref[...]
        xn_ref[...] = _rms(x, g_ref[...]).astype(BF16)
        o_ref[...] = x

    xn = xn_ref[...]
    gate = jnp.dot(xn, wg_ref[...], preferred_element_type=F32)
    up = jnp.dot(xn, wu_ref[...], preferred_element_type=F32)
    h = (_silu(gate) * up).astype(BF16)
    o_ref[...] += jnp.dot(h, wd_ref[...], preferred_element_type=F32)


def ffn(x, g, w_gu, w_down, *, tm=512, tf=512):
    M, D = x.shape
    Fh = w_down.shape[0]
    tm = _pick(M, tm, SUBLANES)
    tf = _pick(Fh, tf, LANES)
    nf = Fh // tf
    return pl.pallas_call(
        _ffn_kernel,
        grid=(M // tm, nf),
        in_specs=[
            pl.BlockSpec((tm, D), lambda i, j: (i, 0)),
            pl.BlockSpec((1, D), lambda i, j: (0, 0)),
            pl.BlockSpec((D, tf), lambda i, j: (0, j)),
            pl.BlockSpec((D, tf), lambda i, j: (0, j + nf)),
            pl.BlockSpec((tf, D), lambda i, j: (j, 0)),
        ],
        out_specs=pl.BlockSpec((tm, D), lambda i, j: (i, 0)),
        out_shape=jax.ShapeDtypeStruct((M, D), F32),
        scratch_shapes=[pltpu.VMEM((tm, D), BF16)],
        compiler_params=_params("parallel", "arbitrary"),
        name="ffn",
    )(x, g.reshape(1, D), w_gu, w_gu, w_down)


def _sb_kernel(q_ref, k_ref, v_ref, o_ref, *, blk):
    i = pl.program_id(2)
    q = q_ref[...]
    row = lax.broadcasted_iota(jnp.int32, (blk, blk), 0)
    col = lax.broadcasted_iota(jnp.int32, (blk, blk), 1)
    suffix = jnp.where(row >= col, 1.0, 0.0).astype(BF16)
    past = col < row

    def block(j, acc, run, masked):
        start = pl.multiple_of(j * blk, blk)
        k = k_ref[pl.ds(start, blk), :]
        v = v_ref[pl.ds(start, blk), :]
        z = lax.dot_general(q, k, (((1,), (1,)), ((), ())), preferred_element_type=F32)
        sp = _softplus(z)
        if masked:
            sp = jnp.where(past, sp, 0.0)
        hi = sp.astype(BF16)
        lo = (sp - hi.astype(F32)).astype(BF16)
        cum = (jnp.dot(hi, suffix, preferred_element_type=F32)
               + jnp.dot(lo, suffix, preferred_element_type=F32))
        logw = z - cum - run
        if masked:
            logw = jnp.where(past, logw, NEG_BIG)
        w = jnp.exp(logw).astype(BF16)
        acc = acc + jnp.dot(w, v, preferred_element_type=F32)
        return acc, run + cum[:, 0:1]

    acc0 = jnp.zeros((blk, HEAD_DIM), F32)
    run0 = jnp.zeros((blk, 1), F32)
    acc, run = block(i, acc0, run0, True)

    def body(s, carry):
        return block(i - 1 - s, carry[0], carry[1], False)

    acc, run = lax.fori_loop(0, i, body, (acc, run))
    o_ref[...] = acc.astype(o_ref.dtype)


def sb_attention(q, k, v, *, blk=128):
    B, T, HD = q.shape
    H = HD // HEAD_DIM
    blk = min(blk, T)
    assert T % blk == 0
    return pl.pallas_call(
        functools.partial(_sb_kernel, blk=blk),
        grid=(B, H, T // blk),
        in_specs=[
            pl.BlockSpec((None, blk, HEAD_DIM), lambda b, h, i: (b, i, h)),
            pl.BlockSpec((None, T, HEAD_DIM), lambda b, h, i: (b, 0, h)),
            pl.BlockSpec((None, T, HEAD_DIM), lambda b, h, i: (b, 0, h)),
        ],
        out_specs=pl.BlockSpec((None, blk, HEAD_DIM), lambda b, h, i: (b, i, h)),
        out_shape=jax.ShapeDtypeStruct((B, T, HD), BF16),
        compiler_params=_params("parallel", "parallel", "arbitrary"),
        name="sb_attention",
    )(q, k, v)


def _conv_kernel(x_ref, halo_ref, w_ref, b_ref, o_ref, buf_ref, *, taps, tt, silu, l2_scale):
    i = pl.program_id(1)
    halo = halo_ref[...]
    buf_ref[0:SUBLANES, :] = jnp.where(i > 0, halo, 0.0)
    buf_ref[SUBLANES:SUBLANES + tt, :] = x_ref[...]
    w = w_ref[...]
    y = b_ref[...] + w[taps - 1:taps, :] * x_ref[...]
    for kk in range(taps - 1):
        shift = taps - 1 - kk
        y = y + w[kk:kk + 1, :] * buf_ref[SUBLANES - shift:SUBLANES - shift + tt, :]
    if silu:
        y = _silu(y)
    if l2_scale is not None:
        outs = []
        for h in range(y.shape[1] // HEAD_DIM):
            yh = y[:, h * HEAD_DIM:(h + 1) * HEAD_DIM]
            ss = jnp.sum(yh * yh, axis=-1, keepdims=True)
            outs.append(yh * (lax.rsqrt(ss + NORM_EPS) * l2_scale))
        y = outs[0] if len(outs) == 1 else jnp.concatenate(outs, axis=1)
    o_ref[...] = y.astype(o_ref.dtype)


def causal_conv(x, w, bias, *, col0, ncols, wcol0, silu, l2_scale, out_dtype, tt=512, tc=512):
    B, T, _ = x.shape
    taps = w.shape[0]
    tt = _pick(T, tt, SUBLANES)
    tc = _pick(ncols, tc, LANES)
    assert col0 % tc == 0 and wcol0 % tc == 0 and T % tt == 0 and tt % SUBLANES == 0
    off, woff = col0 // tc, wcol0 // tc
    rb = tt // SUBLANES
    return pl.pallas_call(
        functools.partial(_conv_kernel, taps=taps, tt=tt, silu=silu, l2_scale=l2_scale),
        grid=(B, T // tt, ncols // tc),
        in_specs=[
            pl.BlockSpec((None, tt, tc), lambda b, i, c: (b, i, c + off)),
            pl.BlockSpec((None, SUBLANES, tc), lambda b, i, c: (b, jnp.maximum(i * rb - 1, 0), c + off)),
            pl.BlockSpec((taps, tc), lambda b, i, c: (0, c + woff)),
            pl.BlockSpec((1, tc), lambda b, i, c: (0, c + woff)),
        ],
        out_specs=pl.BlockSpec((None, tt, tc), lambda b, i, c: (b, i, c)),
        out_shape=jax.ShapeDtypeStruct((B, T, ncols), out_dtype),
        scratch_shapes=[pltpu.VMEM((tt + SUBLANES, tc), F32)],
        compiler_params=_params("parallel", "parallel", "parallel"),
        name="causal_conv",
    )(x, x, w, bias)


def _bdot(a, b):
    return jnp.dot(a.astype(BF16), b.astype(BF16), preferred_element_type=F32)


def _bdot_nt(a, b):
    return lax.dot_general(a.astype(BF16), b.astype(BF16), (((1,), (1,)), ((), ())),
                           preferred_element_type=F32)


def _bdot_tn(a, b):
    return lax.dot_general(a.astype(BF16), b.astype(BF16), (((0,), (0,)), ((), ())),
                           preferred_element_type=F32)


def _gdn_kernel(q_ref, k_ref, v_ref, z_ref, g_ref, beta_ref, gain_ref, o_ref, *, chunk, n_chunks):
    C = chunk
    ri = lax.broadcasted_iota(jnp.int32, (C, C), 0)
    ci = lax.broadcasted_iota(jnp.int32, (C, C), 1)
    eye = ri == ci
    lower = ci <= ri
    strict = ci < ri
    gain = gain_ref[...]
    n_double = max(1, int(math.ceil(math.log2(C))) - 1)

    def body(n, state):
        start = pl.multiple_of(n * C, C)
        q = q_ref[pl.ds(start, C), :]
        k = k_ref[pl.ds(start, C), :]
        v = v_ref[pl.ds(start, C), :]
        g_row = g_ref[pl.ds(n, 1), :]
        beta_row = beta_ref[pl.ds(n, 1), :]
        g_col = jnp.sum(jnp.where(eye, g_row, 0.0), axis=1, keepdims=True)
        beta_col = jnp.sum(jnp.where(eye, beta_row, 0.0), axis=1, keepdims=True)
        gc_col = jnp.sum(jnp.where(lower, g_row, 0.0), axis=1, keepdims=True)
        gc_row = jnp.sum(jnp.where(ri <= ci, g_col, 0.0), axis=0, keepdims=True)
        g_last = jnp.sum(g_row, axis=1, keepdims=True)
        decay = jnp.where(lower, jnp.exp(jnp.where(lower, gc_col - gc_row, 0.0)), 0.0)

        kb = k * beta_col
        a_mat = jnp.where(strict, _bdot_nt(kb, k) * decay, 0.0)
        t_mat = jnp.where(eye, 1.0, 0.0) - a_mat
        p = a_mat
        for _ in range(n_double):
            p = _bdot(p, p)
            t_mat = t_mat + _bdot(t_mat, p)
        u = _bdot(t_mat, v * beta_col)
        w = _bdot(t_mat, kb * jnp.exp(gc_col))

        v_new = u - _bdot(w, state)
        attn = _bdot_nt(q, k) * decay
        out = _bdot(q * jnp.exp(gc_col), state) + _bdot(attn, v_new)
        k_dec = k * jnp.exp(g_last - gc_col)
        state = state * jnp.exp(g_last) + _bdot_tn(k_dec, v_new)

        zz = z_ref[pl.ds(start, C), :]
        o_ref[pl.ds(start, C), :] = (_rms(out, gain) * _silu(zz)).astype(o_ref.dtype)
        return state

    lax.fori_loop(0, n_chunks, body, jnp.zeros((HEAD_DIM, HEAD_DIM), F32))


def gdn_delta_rule(q, k, v, proj, z_col0, g, beta, gain, *, chunk=GDN_CHUNK):
    B, T, KD = q.shape
    VD = v.shape[2]
    Hk, Hv = KD // HEAD_DIM, VD // HEAD_DIM
    rep = Hv // Hk
    N = T // chunk
    zoff = z_col0 // HEAD_DIM
    seq = lambda f: pl.BlockSpec((None, T, HEAD_DIM), f)
    return pl.pallas_call(
        functools.partial(_gdn_kernel, chunk=chunk, n_chunks=N),
        grid=(B, Hv),
        in_specs=[
            seq(lambda b, h: (b, 0, h // rep)),
            seq(lambda b, h: (b, 0, h // rep)),
            seq(lambda b, h: (b, 0, h)),
            seq(lambda b, h: (b, 0, h + zoff)),
            pl.BlockSpec((None, None, N, chunk), lambda b, h: (b, h, 0, 0)),
            pl.BlockSpec((None, None, N, chunk), lambda b, h: (b, h, 0, 0)),
            pl.BlockSpec((1, HEAD_DIM), lambda b, h: (0, 0)),
        ],
        out_specs=seq(lambda b, h: (b, 0, h)),
        out_shape=jax.ShapeDtypeStruct((B, T, VD), BF16),
        compiler_params=_params("parallel", "parallel"),
        name="gdn_delta_rule",
    )(q, k, v, proj, g, beta, gain)


def _rope_table_kernel(pos_ref, freq_ref, sign_ref, cos_ref, sin_ref):
    ang = pos_ref[...].astype(F32) * freq_ref[...]
    cos_ref[...] = jnp.cos(ang)
    sin_ref[...] = jnp.sin(ang) * sign_ref[...]


def rope_tables(pos, *, tm=1024):
    M = pos.shape[0]
    tm = _pick(M, tm, SUBLANES)
    half = ROPE_DIM // 2
    lane = jnp.arange(HEAD_DIM)
    inv_freq = ROPE_THETA ** (-(lane % half).astype(F32) / half)
    freq = jnp.where(lane < ROPE_DIM, inv_freq, 0.0).reshape(1, HEAD_DIM).astype(F32)
    sign = jnp.where(lane < half, -1.0, 1.0).reshape(1, HEAD_DIM).astype(F32)
    spec = pl.BlockSpec((tm, HEAD_DIM), lambda i: (i, 0))
    const = pl.BlockSpec((1, HEAD_DIM), lambda i: (0, 0))
    return pl.pallas_call(
        _rope_table_kernel,
        grid=(M // tm,),
        in_specs=[pl.BlockSpec((tm, 1), lambda i: (i, 0)), const, const],
        out_specs=[spec, spec],
        out_shape=[jax.ShapeDtypeStruct((M, HEAD_DIM), F32)] * 2,
        compiler_params=_params("parallel"),
        name="rope_tables",
    )(pos, freq, sign)


def _band_kernel(q_ref, kc_ref, kp_ref, vc_ref, vp_ref, o_ref, lse_ref, *, blk, span, heads):
    n = pl.program_id(2)
    qi = lax.broadcasted_iota(jnp.int32, (blk, 2 * blk), 0)
    kj = lax.broadcasted_iota(jnp.int32, (blk, 2 * blk), 1)
    steps = blk + qi - kj
    first_key = jnp.where(n > 0, 0, blk)
    valid = (steps >= 0) & (steps <= span) & (kj >= first_key)
    for h in range(heads):
        sl = slice(h * HEAD_DIM, (h + 1) * HEAD_DIM)
        q = q_ref[:, sl]
        kw = jnp.concatenate([kp_ref[:, sl], kc_ref[:, sl]], axis=0)
        vw = jnp.concatenate([vp_ref[:, sl], vc_ref[:, sl]], axis=0)
        s = lax.dot_general(q, kw, (((1,), (1,)), ((), ())), preferred_element_type=F32)
        s = jnp.where(valid, s, NEG_BIG)
        m = jnp.max(s, axis=-1, keepdims=True)
        p = jnp.where(valid, jnp.exp(s - m), 0.0)
        den = jnp.sum(p, axis=-1, keepdims=True)
        o = jnp.dot((p / den).astype(BF16), vw, preferred_element_type=F32)
        o_ref[:, sl] = o
        lse_ref[:, sl] = jnp.broadcast_to(m + jnp.log(den), (blk, HEAD_DIM))


def band_attention(q, k, v, group, window, dilation, *, blk=DSW_BLOCK, heads=DSW_HEADS_PER_GROUP):
    B, T, HD = q.shape
    d = dilation
    span = window // d
    assert span <= blk and T % (d * blk) == 0
    sub = T // d
    nb = sub // blk
    gw = heads * HEAD_DIM
    gpr = HD // gw
    view = lambda a: a.reshape(B, sub, d * HD)
    cur = pl.BlockSpec((None, blk, gw), lambda b, r, n: (b, n, r * gpr + group))
    prev = pl.BlockSpec((None, blk, gw), lambda b, r, n: (b, jnp.maximum(n - 1, 0), r * gpr + group))
    out = pl.BlockSpec((None, blk, gw), lambda b, r, n: (b, n, r))
    o, lse = pl.pallas_call(
        functools.partial(_band_kernel, blk=blk, span=span, heads=heads),
        grid=(B, d, nb),
        in_specs=[cur, cur, prev, cur, prev],
        out_specs=[out, out],
        out_shape=[jax.ShapeDtypeStruct((B, sub, d * gw), F32)] * 2,
        compiler_params=_params("parallel", "parallel", "arbitrary"),
        name=f"band_attention_d{d}",
    )(view(q), view(k), view(k), view(v), view(v))
    return o.reshape(B, T, gw), lse.reshape(B, T, gw)


def _combine_kernel(*refs, n):
    o_refs, l_refs, out_ref = refs[:n], refs[n:2 * n], refs[2 * n]
    ls = [r[...] for r in l_refs]
    m = functools.reduce(jnp.maximum, ls)
    es = [jnp.exp(l - m) for l in ls]
    inv = 1.0 / functools.reduce(lambda a, b: a + b, es)
    w = o_refs[0].shape[1]
    for gi in range(n):
        out_ref[:, gi * w:(gi + 1) * w] = (o_refs[gi][...] * (es[gi] * inv)).astype(out_ref.dtype)


def combine_groups(os_, lses, *, tm=512):
    n = len(os_)
    M, w = os_[0].shape
    tm = _pick(M, tm, SUBLANES)
    spec = pl.BlockSpec((tm, w), lambda i: (i, 0))
    return pl.pallas_call(
        functools.partial(_combine_kernel, n=n),
        grid=(M // tm,),
        in_specs=[spec] * (2 * n),
        out_specs=pl.BlockSpec((tm, n * w), lambda i: (i, 0)),
        out_shape=jax.ShapeDtypeStruct((M, n * w), BF16),
        compiler_params=_params("parallel"),
        name="combine_groups",
    )(*os_, *lses)


def _lru_kernel(x_ref, gate_ref, wa_ref, wx_ref, ba_ref, bx_ref, lam_ref, o_ref,
                a_ref, u_ref, h_ref, *, tt, nblk):
    @pl.when(pl.program_id(1) == 0)
    def _():
        h_ref[...] = jnp.zeros_like(h_ref)

    for nb in range(nblk):
        sl = slice(nb * LRU_BLOCK_DIM, (nb + 1) * LRU_BLOCK_DIM)
        xb = x_ref[:, sl]
        xb16 = xb.astype(BF16)
        r = _sigmoid(jnp.dot(xb16, wa_ref[nb], preferred_element_type=F32) + ba_ref[:, sl])
        ig = _sigmoid(jnp.dot(xb16, wx_ref[nb], preferred_element_type=F32) + bx_ref[:, sl])
        log_a = (-LRU_C) * r * _softplus(-lam_ref[:, sl])
        a_ref[:, sl] = jnp.exp(log_a)
        u_ref[:, sl] = jnp.sqrt(1.0 - jnp.exp(2.0 * log_a)) * (ig * xb)

    def group(gi, h):
        base = pl.multiple_of(gi * SUBLANES, SUBLANES)
        for r8 in range(SUBLANES):
            h = a_ref[pl.ds(base + r8, 1), :] * h + u_ref[pl.ds(base + r8, 1), :]
            u_ref[pl.ds(base + r8, 1), :] = h
        return h

    h_ref[...] = lax.fori_loop(0, tt // SUBLANES, group, h_ref[...])
    o_ref[...] = (u_ref[...] * gate_ref[...].astype(F32)).astype(o_ref.dtype)


def lru_scan(xc, gate, w_a, w_x, b_a, b_x, lam, *, tt=256):
    B, T, W = xc.shape
    nblk = W // LRU_BLOCK_DIM
    tt = _pick(T, tt, SUBLANES)
    tile = pl.BlockSpec((None, tt, W), lambda b, i: (b, i, 0))
    wspec = pl.BlockSpec((nblk, LRU_BLOCK_DIM, LRU_BLOCK_DIM), lambda b, i: (0, 0, 0))
    vec = pl.BlockSpec((1, W), lambda b, i: (0, 0))
    return pl.pallas_call(
        functools.partial(_lru_kernel, tt=tt, nblk=nblk),
        grid=(B, T // tt),
        in_specs=[tile, tile, wspec, wspec, vec, vec, vec],
        out_specs=tile,
        out_shape=jax.ShapeDtypeStruct((B, T, W), BF16),
        scratch_shapes=[pltpu.VMEM((tt, W), F32), pltpu.VMEM((tt, W), F32), pltpu.VMEM((1, W), F32)],
        compiler_params=_params("parallel", "arbitrary"),
        name="lru_scan",
    )(xc, gate, w_a, w_x, b_a, b_x, lam)


def _row(v):
    return v.reshape(1, -1).astype(F32)


def sb_mixer(x2, B, T, norm_g, w_in, q_norm, k_norm, w_out):
    hd = w_in.shape[1] // 3
    w_in = w_in.astype(BF16)
    scale = HEAD_DIM ** -0.5
    q = norm_matmul(x2, norm_g, w_in, col0=0, ncols=hd, epilogue=_ep_headnorm(scale, False),
                    const_extras=(_row(q_norm),), out_dtype=BF16, name="sb_proj_q")
    k = norm_matmul(x2, norm_g, w_in, col0=hd, ncols=hd, epilogue=_ep_headnorm(1.0, False),
                    const_extras=(_row(k_norm),), out_dtype=BF16, name="sb_proj_k")
    v = norm_matmul(x2, norm_g, w_in, col0=2 * hd, ncols=hd, out_dtype=BF16, name="sb_proj_v")
    shp = (B, T, hd)
    o = sb_attention(q.reshape(shp), k.reshape(shp), v.reshape(shp))
    return matmul_residual(o.reshape(B * T, hd), w_out.astype(BF16), x2, name="sb_out")


def gdn_mixer(x2, B, T, norm_g, w_in, conv_w, a_log, dt_bias, o_norm, w_out):
    M = B * T
    Hv = a_log.shape[0]
    vd = Hv * HEAD_DIM
    kd = (conv_w.shape[1] - vd) // 2
    main = 2 * kd + 2 * vd
    proj = norm_matmul(x2, norm_g, w_in.astype(BF16), col0=0, ncols=main, name="gdn_proj")
    w_ba = jnp.pad(w_in[:, main:], ((0, 0), (0, LANES - 2 * Hv))).astype(BF16)
    pad = lambda p: jnp.pad(p.astype(F32), (Hv, LANES - 2 * Hv)).reshape(1, LANES)
    gates = norm_matmul(x2, norm_g, w_ba, epilogue=_ep_gdn_gates(Hv),
                        const_extras=(pad(a_log), pad(dt_bias)), name="gdn_gates")
    proj3 = proj.reshape(B, T, main)
    zero_b = jnp.zeros((1, conv_w.shape[1]), F32)
    conv = functools.partial(causal_conv, proj3, conv_w, zero_b, silu=True, out_dtype=F32)
    q = conv(col0=0, ncols=kd, wcol0=0, l2_scale=HEAD_DIM ** -0.5)
    k = conv(col0=kd, ncols=kd, wcol0=kd, l2_scale=1.0)
    v = conv(col0=2 * kd, ncols=vd, wcol0=2 * kd, l2_scale=None)
    N = T // GDN_CHUNK
    per_head = lambda a: a.reshape(B, T, Hv).transpose(0, 2, 1).reshape(B, Hv, N, GDN_CHUNK)
    beta = per_head(gates[:, :Hv])
    g = per_head(gates[:, Hv:2 * Hv])
    o = gdn_delta_rule(q, k, v, proj3, 2 * kd + vd, g, beta, _row(o_norm))
    return matmul_residual(o.reshape(M, vd), w_out.astype(BF16), x2, name="gdn_out")


def dsw_mixer(x2, B, T, norm_g, positions, w_in, q_norm, k_norm, w_out):
    M = B * T
    hd = w_in.shape[1] // 3
    w_in = w_in.astype(BF16)
    cos_t, sin_t = rope_tables(positions.reshape(M, 1).astype(jnp.int32))
    scale = HEAD_DIM ** -0.5
    tn = 3 * HEAD_DIM * 2
    q = norm_matmul(x2, norm_g, w_in, col0=0, ncols=hd, epilogue=_ep_headnorm(scale, True),
                    row_extras=(cos_t, sin_t), const_extras=(_row(q_norm),), out_dtype=BF16, tn=tn,
                    name="dsw_proj_q")
    k = norm_matmul(x2, norm_g, w_in, col0=hd, ncols=hd, epilogue=_ep_headnorm(1.0, True),
                    row_extras=(cos_t, sin_t), const_extras=(_row(k_norm),), out_dtype=BF16, tn=tn,
                    name="dsw_proj_k")
    v = norm_matmul(x2, norm_g, w_in, col0=2 * hd, ncols=hd, out_dtype=BF16, tn=tn, name="dsw_proj_v")
    shp = (B, T, hd)
    q, k, v = q.reshape(shp), k.reshape(shp), v.reshape(shp)
    os_, lses = [], []
    for gi, (window, dilation) in enumerate(DSW_GROUPS):
        o_g, lse_g = band_attention(q, k, v, gi, window, dilation)
        os_.append(o_g.reshape(M, -1))
        lses.append(lse_g.reshape(M, -1))
    o = combine_groups(os_, lses)
    return matmul_residual(o, w_out.astype(BF16), x2, name="dsw_out")


def lru_mixer(x2, B, T, norm_g, w_in, conv_w, conv_b, w_a, b_a, w_x, b_x, lam, w_out):
    W = w_in.shape[1] // 2
    w_in = w_in.astype(BF16)
    gate = norm_matmul(x2, norm_g, w_in, col0=0, ncols=W, epilogue=_ep_gelu, out_dtype=BF16,
                       name="lru_proj_gate")
    xr = norm_matmul(x2, norm_g, w_in, col0=W, ncols=W, name="lru_proj_x")
    xc = causal_conv(xr.reshape(B, T, W), conv_w, _row(conv_b), col0=0, ncols=W, wcol0=0,
                     silu=False, l2_scale=None, out_dtype=F32)
    y = lru_scan(xc, gate.reshape(B, T, W), w_a.astype(BF16), w_x.astype(BF16),
                 _row(b_a), _row(b_x), _row(lam))
    return matmul_residual(y.reshape(B * T, W), w_out.astype(BF16), x2, name="lru_out")


def kernel(x, positions, mix_norm, ffn_norm, ffn_w_gu, ffn_w_down, sb_w_in, sb_q_norm, sb_k_norm, sb_w_out, gdn_w_in, gdn_conv_w, gdn_a_log, gdn_dt_bias, gdn_o_norm, gdn_w_out, dsw_w_in, dsw_q_norm, dsw_k_norm, dsw_w_out, lru_w_in, lru_conv_w, lru_conv_b, lru_w_a, lru_b_a, lru_w_x, lru_b_x, lru_lambda, lru_w_out):
    B, T, D = x.shape
    depth = mix_norm.shape[0]
    x2 = x.reshape(B * T, D)
    for i in range(depth):
        kind, j = i % 4, i // 4
        if kind == 0:
            x2 = sb_mixer(x2, B, T, mix_norm[i], sb_w_in[j], sb_q_norm[j], sb_k_norm[j], sb_w_out[j])
        elif kind == 1:
            x2 = gdn_mixer(x2, B, T, mix_norm[i], gdn_w_in[j], gdn_conv_w[j], gdn_a_log[j],
                           gdn_dt_bias[j], gdn_o_norm[j], gdn_w_out[j])
        elif kind == 2:
            x2 = dsw_mixer(x2, B, T, mix_norm[i], positions, dsw_w_in[j], dsw_q_norm[j],
                           dsw_k_norm[j], dsw_w_out[j])
        else:
            x2 = lru_mixer(x2, B, T, mix_norm[i], lru_w_in[j], lru_conv_w[j], lru_conv_b[j],
                           lru_w_a[j], lru_b_a[j], lru_w_x[j], lru_b_x[j], lru_lambda[j], lru_w_out[j])
        x2 = ffn(x2, ffn_norm[i], ffn_w_gu[i].astype(BF16), ffn_w_down[i].astype(BF16))
    return x2.reshape(B, T, D)
```

```python
import functools
import math

import jax
import jax.numpy as jnp
from jax import lax
from jax.experimental import pallas as pl
from jax.experimental.pallas import tpu as pltpu

F32 = jnp.float32
BF16 = jnp.bfloat16

NORM_EPS = 1e-6
HEAD_DIM = 128
LANES = 128
SUBLANES = 8
VMEM_LIMIT_BYTES = 56 * 1024 * 1024

GDN_CONV = 4
GDN_CHUNK = 64
DSW_GROUPS = ((128, 1), (512, 4), (2048, 16))
DSW_HEADS_PER_GROUP = 6
DSW_BLOCK = 128
ROPE_DIM = HEAD_DIM // 4
ROPE_THETA = 500000.0
LRU_BLOCK_DIM = 256
LRU_C = 8.0
NEG_BIG = -1e30


def _pick(dim, pref, align):
    if dim <= pref:
        return dim
    t = (pref // align) * align
    while t >= align:
        if dim % t == 0:
            return t
        t -= align
    return dim


def _params(*sem):
    return pltpu.CompilerParams(dimension_semantics=sem, vmem_limit_bytes=VMEM_LIMIT_BYTES)


def _softplus(x):
    return jnp.maximum(x, 0.0) + jnp.log(1.0 + jnp.exp(-jnp.abs(x)))


def _sigmoid(x):
    return 1.0 / (1.0 + jnp.exp(-x))


def _silu(x):
    return x * _sigmoid(x)


def _rms(x, g):
    ms = jnp.mean(x * x, axis=-1, keepdims=True)
    return x * lax.rsqrt(ms + NORM_EPS) * g


def _norm_matmul_kernel(*refs, epilogue, n_extra):
    x_ref, g_ref, w_ref = refs[:3]
    extra = refs[3:3 + n_extra]
    o_ref, xn_ref = refs[3 + n_extra], refs[4 + n_extra]

    @pl.when(pl.program_id(1) == 0)
    def _():
        xn_ref[...] = _rms(x_ref[...], g_ref[...]).astype(BF16)

    y = jnp.dot(xn_ref[...], w_ref[...], preferred_element_type=F32)
    if epilogue is not None:
        y = epilogue(y, *[e[...] for e in extra])
    o_ref[...] = y.astype(o_ref.dtype)


def norm_matmul(x, g, w, *, col0=0, ncols=None, epilogue=None, row_extras=(), const_extras=(),
                out_dtype=F32, tm=1024, tn=1024, name="norm_matmul"):
    M, K = x.shape
    ncols = w.shape[1] - col0 if ncols is None else ncols
    tm = _pick(M, tm, SUBLANES)
    tn = _pick(ncols, tn, LANES)
    assert col0 % tn == 0 and ncols % tn == 0 and M % tm == 0
    off = col0 // tn
    in_specs = [
        pl.BlockSpec((tm, K), lambda i, j: (i, 0)),
        pl.BlockSpec((1, K), lambda i, j: (0, 0)),
        pl.BlockSpec((K, tn), lambda i, j: (0, j + off)),
    ]
    args = [x, g.reshape(1, K), w]
    for e in row_extras:
        in_specs.append(pl.BlockSpec((tm, e.shape[1]), lambda i, j: (i, 0)))
        args.append(e)
    for e in const_extras:
        in_specs.append(pl.BlockSpec(e.shape, lambda i, j: (0, 0)))
        args.append(e)
    n_extra = len(row_extras) + len(const_extras)
    return pl.pallas_call(
        functools.partial(_norm_matmul_kernel, epilogue=epilogue, n_extra=n_extra),
        grid=(M // tm, ncols // tn),
        in_specs=in_specs,
        out_specs=pl.BlockSpec((tm, tn), lambda i, j: (i, j)),
        out_shape=jax.ShapeDtypeStruct((M, ncols), out_dtype),
        scratch_shapes=[pltpu.VMEM((tm, K), BF16)],
        compiler_params=_params("parallel", "arbitrary"),
        name=name,
    )(*args)


def _ep_headnorm(scale, rope):
    half = ROPE_DIM // 2

    def ep(y, *extra):
        if rope:
            cos_t, sin_t, gain = extra
            lane = lax.broadcasted_iota(jnp.int32, cos_t.shape, 1)
        else:
            (gain,) = extra
        outs = []
        for h in range(y.shape[1] // HEAD_DIM):
            yh = y[:, h * HEAD_DIM:(h + 1) * HEAD_DIM]
            ms = jnp.mean(yh * yh, axis=-1, keepdims=True)
            yh = yh * lax.rsqrt(ms + NORM_EPS) * gain
            if rope:
                partner = jnp.where(lane < half, pltpu.roll(yh, HEAD_DIM - half, 1),
                                    pltpu.roll(yh, half, 1))
                yh = yh * cos_t + partner * sin_t
            outs.append(yh * scale if scale != 1.0 else yh)
        return outs[0] if len(outs) == 1 else jnp.concatenate(outs, axis=1)

    return ep


def _ep_gelu(y):
    c = math.sqrt(2.0 / math.pi)
    return 0.5 * y * (1.0 + jnp.tanh(c * (y + 0.044715 * (y * y * y))))


def _ep_gdn_gates(n_heads):
    def ep(y, a_log, dt_bias):
        lane = lax.broadcasted_iota(jnp.int32, y.shape, 1)
        beta = _sigmoid(y)
        g = -jnp.exp(a_log) * _softplus(y + dt_bias)
        return jnp.where(lane < n_heads, beta, g)

    return ep


def _matmul_res_kernel(a_ref, w_ref, r_ref, o_ref):
    y = jnp.dot(a_ref[...].astype(BF16), w_ref[...], preferred_element_type=F32)
    o_ref[...] = r_ref[...] + y


def matmul_residual(a, w, res, *, tm=1024, tn=512, name="out_proj"):
    M, K = a.shape
    N = w.shape[1]
    tm = _pick(M, tm, SUBLANES)
    tn = _pick(N, tn, LANES)
    return pl.pallas_call(
        _matmul_res_kernel,
        grid=(M // tm, N // tn),
        in_specs=[
            pl.BlockSpec((tm, K), lambda i, j: (i, 0)),
            pl.BlockSpec((K, tn), lambda i, j: (0, j)),
            pl.BlockSpec((tm, tn), lambda i, j: (i, j)),
        ],
        out_specs=pl.BlockSpec((tm, tn), lambda i, j: (i, j)),
        out_shape=jax.ShapeDtypeStruct((M, N), F32),
        compiler_params=_params("parallel", "arbitrary"),
        name=name,
    )(a, w, res)


def _ffn_kernel(x_ref, g_ref, wg_ref, wu_ref, wd_ref, o_ref, xn_ref):
    @pl.when(pl.program_id(1) == 0)
    def _():
        x = x_ref[...]
        xn_ref[...] = _rms(x, g_ref[...]).astype(BF16)
        o_ref[...] = x

    xn = xn_ref[...]
    gate = jnp.dot(xn, wg_ref[...], preferred_element_type=F32)
    up = jnp.dot(xn, wu_ref[...], preferred_element_type=F32)
    h = (_silu(gate) * up).astype(BF16)
    o_ref[...] += jnp.dot(h, wd_ref[...], preferred_element_type=F32)


def ffn(x, g, w_gu, w_down, *, tm=512, tf=512):
    M, D = x.shape
    Fh = w_down.shape[0]
    tm = _pick(M, tm, SUBLANES)
    tf = _pick(Fh, tf, LANES)
    nf = Fh // tf
    return pl.pallas_call(
        _ffn_kernel,
        grid=(M // tm, nf),
        in_specs=[
            pl.BlockSpec((tm, D), lambda i, j: (i, 0)),
            pl.BlockSpec((1, D), lambda i, j: (0, 0)),
            pl.BlockSpec((D, tf), lambda i, j: (0, j)),
            pl.BlockSpec((D, tf), lambda i, j: (0, j + nf)),
            pl.BlockSpec((tf, D), lambda i, j: (j, 0)),
        ],
        out_specs=pl.BlockSpec((tm, D), lambda i, j: (i, 0)),
        out_shape=jax.ShapeDtypeStruct((M, D), F32),
        scratch_shapes=[pltpu.VMEM((tm, D), BF16)],
        compiler_params=_params("parallel", "arbitrary"),
        name="ffn",
    )(x, g.reshape(1, D), w_gu, w_gu, w_down)


SB_STOP = 88.0


def _sb_kernel(q_ref, k_ref, v_ref, o_ref, *, blk, heads, n_q):
    W = 2 * blk
    row = lax.broadcasted_iota(jnp.int32, (blk, W), 0)
    col = lax.broadcasted_iota(jnp.int32, (blk, W), 1)
    r2 = lax.broadcasted_iota(jnp.int32, (W, W), 0)
    c2 = lax.broadcasted_iota(jnp.int32, (W, W), 1)
    suffix_w = jnp.where(r2 >= c2, 1.0, 0.0).astype(BF16)
    rb = lax.broadcasted_iota(jnp.int32, (blk, blk), 0)
    cb = lax.broadcasted_iota(jnp.int32, (blk, blk), 1)
    suffix_b = jnp.where(rb >= cb, 1.0, 0.0).astype(BF16)

    def dot_nt(a, b):
        return lax.dot_general(a, b, (((1,), (1,)), ((), ())), preferred_element_type=F32)

    def suffix_sum(sp, suffix):
        hi = sp.astype(BF16)
        lo = (sp - hi.astype(F32)).astype(BF16)
        return (jnp.dot(hi, suffix, preferred_element_type=F32)
                + jnp.dot(lo, suffix, preferred_element_type=F32))

    def qblock(i, carry):
        q0 = pl.multiple_of(i * blk, blk)
        w0 = pl.multiple_of(jnp.maximum(i - 1, 0) * blk, blk)
        valid = col < row + (q0 - w0)
        qs, accs, runs = [], [], []
        for h in range(heads):
            sl = slice(h * HEAD_DIM, (h + 1) * HEAD_DIM)
            q = q_ref[pl.ds(q0, blk), sl]
            z = dot_nt(q, k_ref[pl.ds(w0, W), sl])
            sp = jnp.where(valid, _softplus(z), 0.0)
            cum = suffix_sum(sp, suffix_w)
            w = jnp.exp(jnp.where(valid, z - cum, NEG_BIG)).astype(BF16)
            qs.append(q)
            accs.append(jnp.dot(w, v_ref[pl.ds(w0, W), sl], preferred_element_type=F32))
            runs.append(cum[:, 0:1])
        for h in range(heads):
            sl = slice(h * HEAD_DIM, (h + 1) * HEAD_DIM)

            def cond(c):
                return (c[0] >= 0) & (jnp.min(c[2]) < SB_STOP)

            def body(c):
                j, acc, run = c
                s0 = pl.multiple_of(j * blk, blk)
                z = dot_nt(qs[h], k_ref[pl.ds(s0, blk), sl])
                cum = suffix_sum(_softplus(z), suffix_b)
                w = jnp.exp(z - cum - run).astype(BF16)
                acc = acc + jnp.dot(w, v_ref[pl.ds(s0, blk), sl], preferred_element_type=F32)
                return j - 1, acc, run + cum[:, 0:1]

            _, acc, _ = lax.while_loop(cond, body, (i - 2, accs[h], runs[h]))
            o_ref[pl.ds(q0, blk), sl] = acc.astype(o_ref.dtype)
        return carry

    lax.fori_loop(0, n_q, qblock, 0)


def sb_attention(q, k, v, *, blk=128, heads=2):
    B, T, HD = q.shape
    H = HD // HEAD_DIM
    assert T % blk == 0 and T >= 2 * blk and H % heads == 0
    seq = pl.BlockSpec((None, T, heads * HEAD_DIM), lambda b, h: (b, 0, h))
    return pl.pallas_call(
        functools.partial(_sb_kernel, blk=blk, heads=heads, n_q=T // blk),
        grid=(B, H // heads),
        in_specs=[seq, seq, seq],
        out_specs=seq,
        out_shape=jax.ShapeDtypeStruct((B, T, HD), BF16),
        compiler_params=_params("parallel", "parallel"),
        name="sb_attention",
    )(q, k, v)


def _conv_kernel(x_ref, halo_ref, w_ref, b_ref, o_ref, buf_ref, *, taps, tt, silu, l2_scale):
    i = pl.program_id(1)
    halo = halo_ref[...]
    buf_ref[0:SUBLANES, :] = jnp.where(i > 0, halo, 0.0)
    buf_ref[SUBLANES:SUBLANES + tt, :] = x_ref[...]
    w = w_ref[...]
    y = b_ref[...] + w[taps - 1:taps, :] * x_ref[...]
    for kk in range(taps - 1):
        shift = taps - 1 - kk
        y = y + w[kk:kk + 1, :] * buf_ref[SUBLANES - shift:SUBLANES - shift + tt, :]
    if silu:
        y = _silu(y)
    if l2_scale is not None:
        outs = []
        for h in range(y.shape[1] // HEAD_DIM):
            yh = y[:, h * HEAD_DIM:(h + 1) * HEAD_DIM]
            ss = jnp.sum(yh * yh, axis=-1, keepdims=True)
            outs.append(yh * (lax.rsqrt(ss + NORM_EPS) * l2_scale))
        y = outs[0] if len(outs) == 1 else jnp.concatenate(outs, axis=1)
    o_ref[...] = y.astype(o_ref.dtype)


def causal_conv(x, w, bias, *, col0, ncols, wcol0, silu, l2_scale, out_dtype, tt=512, tc=512):
    B, T, _ = x.shape
    taps = w.shape[0]
    tt = _pick(T, tt, SUBLANES)
    tc = _pick(ncols, tc, LANES)
    assert col0 % tc == 0 and wcol0 % tc == 0 and T % tt == 0 and tt % SUBLANES == 0
    off, woff = col0 // tc, wcol0 // tc
    rb = tt // SUBLANES
    return pl.pallas_call(
        functools.partial(_conv_kernel, taps=taps, tt=tt, silu=silu, l2_scale=l2_scale),
        grid=(B, T // tt, ncols // tc),
        in_specs=[
            pl.BlockSpec((None, tt, tc), lambda b, i, c: (b, i, c + off)),
            pl.BlockSpec((None, SUBLANES, tc), lambda b, i, c: (b, jnp.maximum(i * rb - 1, 0), c + off)),
            pl.BlockSpec((taps, tc), lambda b, i, c: (0, c + woff)),
            pl.BlockSpec((1, tc), lambda b, i, c: (0, c + woff)),
        ],
        out_specs=pl.BlockSpec((None, tt, tc), lambda b, i, c: (b, i, c)),
        out_shape=jax.ShapeDtypeStruct((B, T, ncols), out_dtype),
        scratch_shapes=[pltpu.VMEM((tt + SUBLANES, tc), F32)],
        compiler_params=_params("parallel", "parallel", "parallel"),
        name="causal_conv",
    )(x, x, w, bias)


def _bdot(a, b):
    return jnp.dot(a.astype(BF16), b.astype(BF16), preferred_element_type=F32)


def _bdot_nt(a, b):
    return lax.dot_general(a.astype(BF16), b.astype(BF16), (((1,), (1,)), ((), ())),
                           preferred_element_type=F32)


def _bdot_tn(a, b):
    return lax.dot_general(a.astype(BF16), b.astype(BF16), (((0,), (0,)), ((), ())),
                           preferred_element_type=F32)


def _gdn_kernel(q_ref, k_ref, v_ref, z_ref, g_ref, beta_ref, gain_ref, o_ref,
                u_ref, wq_ref, kd_ref, attn_ref, gl_ref, out_ref, state_ref, *, chunk, kheads, rep, group):
    C = chunk
    ri = lax.broadcasted_iota(jnp.int32, (C, C), 0)
    ci = lax.broadcasted_iota(jnp.int32, (C, C), 1)
    eye = ri == ci
    lower = ci <= ri
    strict = ci < ri
    gain = gain_ref[...]
    n_double = max(1, int(math.ceil(math.log2(C))) - 1)
    nv = kheads * rep
    heads = [slice(r * HEAD_DIM, (r + 1) * HEAD_DIM) for r in range(nv)]

    @pl.when(pl.program_id(2) == 0)
    def _():
        state_ref[...] = jnp.zeros_like(state_ref)

    G = group
    rows = G * C
    bmm = functools.partial(jnp.einsum, "gij,gjk->gik", preferred_element_type=F32)
    bmm_nt = functools.partial(jnp.einsum, "gid,gjd->gij", preferred_element_type=F32)

    def prepare(kh):
        q = q_ref[:, heads[kh]].reshape(G, C, HEAD_DIM)
        k = k_ref[:, heads[kh]].reshape(G, C, HEAD_DIM)
        k16 = k.astype(BF16)
        qk = bmm_nt(q.astype(BF16), k16)
        for r in range(kh * rep, (kh + 1) * rep):
            v = v_ref[:, heads[r]].reshape(G, C, HEAD_DIM)
            g_row = g_ref[r]
            beta_row = beta_ref[r]
            g_col = jnp.sum(jnp.where(eye, g_row, 0.0), axis=2, keepdims=True)
            beta_col = jnp.sum(jnp.where(eye, beta_row, 0.0), axis=2, keepdims=True)
            gc_col = jnp.sum(jnp.where(lower, g_row, 0.0), axis=2, keepdims=True)
            gc_row = jnp.sum(jnp.where(ri <= ci, g_col, 0.0), axis=1, keepdims=True)
            g_last = jnp.sum(g_row, axis=2, keepdims=True)
            decay = jnp.where(lower, jnp.exp(jnp.where(lower, gc_col - gc_row, 0.0)), 0.0)
            eg = jnp.exp(gc_col)

            kb = k * beta_col
            a_mat = jnp.where(strict, bmm_nt(kb.astype(BF16), k16) * decay, 0.0)
            t_mat = jnp.where(eye, 1.0, 0.0) - a_mat
            p = a_mat
            for _ in range(n_double):
                p16 = p.astype(BF16)
                p = bmm(p16, p16)
                t_mat = t_mat + bmm(t_mat.astype(BF16), p.astype(BF16))
            rhs = jnp.concatenate([v * beta_col, kb * eg], axis=2).astype(BF16)
            uw = bmm(t_mat.astype(BF16), rhs)
            u_ref[r] = uw[:, :, :HEAD_DIM].reshape(rows, HEAD_DIM)
            wq_ref[r] = jnp.concatenate([uw[:, :, HEAD_DIM:], q * eg], axis=1).astype(BF16).reshape(
                2 * rows, HEAD_DIM)
            kd_ref[r] = (k * jnp.exp(g_last - gc_col)).astype(BF16).reshape(rows, HEAD_DIM)
            attn_ref[r] = (qk * decay).astype(BF16).reshape(rows, C)
            gl_ref[r] = jnp.broadcast_to(g_last, (G, 1, HEAD_DIM))

    def recur(nl, states):
        ls = pl.multiple_of(nl * C, C)
        ls2 = pl.multiple_of(nl * 2 * C, 2 * C)
        hs = range(nv)
        s16 = [states[r].astype(BF16) for r in hs]
        ws = [jnp.dot(wq_ref[r, pl.ds(ls2, 2 * C), :], s16[r], preferred_element_type=F32) for r in hs]
        v_new = [(u_ref[r, pl.ds(ls, C), :] - ws[r][:C]).astype(BF16) for r in hs]
        upd = [lax.dot_general(kd_ref[r, pl.ds(ls, C), :], v_new[r], (((0,), (0,)), ((), ())),
                               preferred_element_type=F32) for r in hs]
        new = tuple(states[r] * jnp.exp(gl_ref[r, nl]) + upd[r] for r in hs)
        for r in hs:
            out_ref[r, pl.ds(ls, C), :] = ws[r][C:] + jnp.dot(
                attn_ref[r, pl.ds(ls, C), :], v_new[r], preferred_element_type=F32)
        return new

    for kh in range(kheads):
        prepare(kh)
    states = lax.fori_loop(0, group, recur, tuple(state_ref[r] for r in range(nv)))
    for r in range(nv):
        state_ref[r] = states[r]
        o_ref[:, heads[r]] = (_rms(out_ref[r], gain) * _silu(z_ref[:, heads[r]])).astype(o_ref.dtype)


def gdn_delta_rule(q, k, v, proj, z_col0, g, beta, gain, *, chunk=GDN_CHUNK, group=8, kheads=2):
    B, T, KD = q.shape
    VD = v.shape[2]
    Hk, Hv = KD // HEAD_DIM, VD // HEAD_DIM
    rep = Hv // Hk
    N = T // chunk
    group = _pick(N, group, SUBLANES)
    kheads = _pick(Hk, kheads, 1)
    nv = kheads * rep
    kw, vw = kheads * HEAD_DIM, nv * HEAD_DIM
    assert z_col0 % vw == 0 and N % group == 0
    zoff = z_col0 // vw
    rows = group * chunk
    tile = lambda width, f: pl.BlockSpec((None, rows, width), f)
    gate = pl.BlockSpec((None, nv, group, 1, chunk), lambda b, h, t: (b, h, t, 0, 0))
    return pl.pallas_call(
        functools.partial(_gdn_kernel, chunk=chunk, kheads=kheads, rep=rep, group=group),
        grid=(B, Hk // kheads, N // group),
        in_specs=[
            tile(kw, lambda b, h, t: (b, t, h)),
            tile(kw, lambda b, h, t: (b, t, h)),
            tile(vw, lambda b, h, t: (b, t, h)),
            tile(vw, lambda b, h, t: (b, t, h + zoff)),
            gate,
            gate,
            pl.BlockSpec((1, HEAD_DIM), lambda b, h, t: (0, 0)),
        ],
        out_specs=tile(vw, lambda b, h, t: (b, t, h)),
        out_shape=jax.ShapeDtypeStruct((B, T, VD), BF16),
        scratch_shapes=[
            pltpu.VMEM((nv, rows, HEAD_DIM), F32),
            pltpu.VMEM((nv, 2 * rows, HEAD_DIM), BF16),
            pltpu.VMEM((nv, rows, HEAD_DIM), BF16),
            pltpu.VMEM((nv, rows, chunk), BF16),
            pltpu.VMEM((nv, group, 1, HEAD_DIM), F32),
            pltpu.VMEM((nv, rows, HEAD_DIM), F32),
            pltpu.VMEM((nv, HEAD_DIM, HEAD_DIM), F32),
        ],
        compiler_params=_params("parallel", "parallel", "arbitrary"),
        name="gdn_delta_rule",
    )(q, k, v, proj, g, beta, gain)


def _rope_table_kernel(pos_ref, freq_ref, sign_ref, cos_ref, sin_ref):
    ang = pos_ref[...].astype(F32) * freq_ref[...]
    cos_ref[...] = jnp.cos(ang)
    sin_ref[...] = jnp.sin(ang) * sign_ref[...]


def rope_tables(pos, *, tm=1024):
    M = pos.shape[0]
    tm = _pick(M, tm, SUBLANES)
    half = ROPE_DIM // 2
    lane = jnp.arange(HEAD_DIM)
    inv_freq = ROPE_THETA ** (-(lane % half).astype(F32) / half)
    freq = jnp.where(lane < ROPE_DIM, inv_freq, 0.0).reshape(1, HEAD_DIM).astype(F32)
    sign = jnp.where(lane < half, -1.0, 1.0).reshape(1, HEAD_DIM).astype(F32)
    spec = pl.BlockSpec((tm, HEAD_DIM), lambda i: (i, 0))
    const = pl.BlockSpec((1, HEAD_DIM), lambda i: (0, 0))
    return pl.pallas_call(
        _rope_table_kernel,
        grid=(M // tm,),
        in_specs=[pl.BlockSpec((tm, 1), lambda i: (i, 0)), const, const],
        out_specs=[spec, spec],
        out_shape=[jax.ShapeDtypeStruct((M, HEAD_DIM), F32)] * 2,
        compiler_params=_params("parallel"),
        name="rope_tables",
    )(pos, freq, sign)


def _band_kernel(q_ref, kc_ref, kp_ref, vc_ref, vp_ref, o_ref, lse_ref, *, blk, span, heads):
    n = pl.program_id(2)
    qi = lax.broadcasted_iota(jnp.int32, (blk, 2 * blk), 0)
    kj = lax.broadcasted_iota(jnp.int32, (blk, 2 * blk), 1)
    steps = blk + qi - kj
    first_key = jnp.where(n > 0, 0, blk)
    valid = (steps >= 0) & (steps <= span) & (kj >= first_key)
    for h in range(heads):
        sl = slice(h * HEAD_DIM, (h + 1) * HEAD_DIM)
        q = q_ref[:, sl]
        kw = jnp.concatenate([kp_ref[:, sl], kc_ref[:, sl]], axis=0)
        vw = jnp.concatenate([vp_ref[:, sl], vc_ref[:, sl]], axis=0)
        s = lax.dot_general(q, kw, (((1,), (1,)), ((), ())), preferred_element_type=F32)
        s = jnp.where(valid, s, NEG_BIG)
        m = jnp.max(s, axis=-1, keepdims=True)
        p = jnp.where(valid, jnp.exp(s - m), 0.0)
        den = jnp.sum(p, axis=-1, keepdims=True)
        o = jnp.dot((p / den).astype(BF16), vw, preferred_element_type=F32)
        o_ref[:, sl] = o
        lse_ref[:, sl] = jnp.broadcast_to(m + jnp.log(den), (blk, HEAD_DIM))


def band_attention(q, k, v, group, window, dilation, *, blk=DSW_BLOCK, heads=DSW_HEADS_PER_GROUP):
    B, T, HD = q.shape
    d = dilation
    span = window // d
    assert span <= blk and T % (d * blk) == 0
    sub = T // d
    nb = sub // blk
    gw = heads * HEAD_DIM
    gpr = HD // gw
    view = lambda a: a.reshape(B, sub, d * HD)
    cur = pl.BlockSpec((None, blk, gw), lambda b, r, n: (b, n, r * gpr + group))
    prev = pl.BlockSpec((None, blk, gw), lambda b, r, n: (b, jnp.maximum(n - 1, 0), r * gpr + group))
    out = pl.BlockSpec((None, blk, gw), lambda b, r, n: (b, n, r))
    o, lse = pl.pallas_call(
        functools.partial(_band_kernel, blk=blk, span=span, heads=heads),
        grid=(B, d, nb),
        in_specs=[cur, cur, prev, cur, prev],
        out_specs=[out, out],
        out_shape=[jax.ShapeDtypeStruct((B, sub, d * gw), F32)] * 2,
        compiler_params=_params("parallel", "parallel", "arbitrary"),
        name=f"band_attention_d{d}",
    )(view(q), view(k), view(k), view(v), view(v))
    return o.reshape(B, T, gw), lse.reshape(B, T, gw)


def _combine_kernel(*refs, n):
    o_refs, l_refs, out_ref = refs[:n], refs[n:2 * n], refs[2 * n]
    ls = [r[...] for r in l_refs]
    m = functools.reduce(jnp.maximum, ls)
    es = [jnp.exp(l - m) for l in ls]
    inv = 1.0 / functools.reduce(lambda a, b: a + b, es)
    w = o_refs[0].shape[1]
    for gi in range(n):
        out_ref[:, gi * w:(gi + 1) * w] = (o_refs[gi][...] * (es[gi] * inv)).astype(out_ref.dtype)


def combine_groups(os_, lses, *, tm=512):
    n = len(os_)
    M, w = os_[0].shape
    tm = _pick(M, tm, SUBLANES)
    spec = pl.BlockSpec((tm, w), lambda i: (i, 0))
    return pl.pallas_call(
        functools.partial(_combine_kernel, n=n),
        grid=(M // tm,),
        in_specs=[spec] * (2 * n),
        out_specs=pl.BlockSpec((tm, n * w), lambda i: (i, 0)),
        out_shape=jax.ShapeDtypeStruct((M, n * w), BF16),
        compiler_params=_params("parallel"),
        name="combine_groups",
    )(*os_, *lses)


def _lru_kernel(x_ref, gate_ref, wa_ref, wx_ref, ba_ref, bx_ref, lam_ref, o_ref,
                a_ref, u_ref, h_ref, *, tt, nblk):
    @pl.when(pl.program_id(1) == 0)
    def _():
        h_ref[...] = jnp.zeros_like(h_ref)

    for nb in range(nblk):
        sl = slice(nb * LRU_BLOCK_DIM, (nb + 1) * LRU_BLOCK_DIM)
        xb = x_ref[:, sl]
        xb16 = xb.astype(BF16)
        r = _sigmoid(jnp.dot(xb16, wa_ref[nb], preferred_element_type=F32) + ba_ref[:, sl])
        ig = _sigmoid(jnp.dot(xb16, wx_ref[nb], preferred_element_type=F32) + bx_ref[:, sl])
        log_a = (-LRU_C) * r * _softplus(-lam_ref[:, sl])
        a_ref[:, sl] = jnp.exp(log_a)
        u_ref[:, sl] = jnp.sqrt(1.0 - jnp.exp(2.0 * log_a)) * (ig * xb)

    def group(gi, h):
        base = pl.multiple_of(gi * SUBLANES, SUBLANES)
        for r8 in range(SUBLANES):
            h = a_ref[pl.ds(base + r8, 1), :] * h + u_ref[pl.ds(base + r8, 1), :]
            u_ref[pl.ds(base + r8, 1), :] = h
        return h

    h_ref[...] = lax.fori_loop(0, tt // SUBLANES, group, h_ref[...])
    o_ref[...] = (u_ref[...] * gate_ref[...].astype(F32)).astype(o_ref.dtype)


def lru_scan(xc, gate, w_a, w_x, b_a, b_x, lam, *, tt=256):
    B, T, W = xc.shape
    nblk = W // LRU_BLOCK_DIM
    tt = _pick(T, tt, SUBLANES)
    tile = pl.BlockSpec((None, tt, W), lambda b, i: (b, i, 0))
    wspec = pl.BlockSpec((nblk, LRU_BLOCK_DIM, LRU_BLOCK_DIM), lambda b, i: (0, 0, 0))
    vec = pl.BlockSpec((1, W), lambda b, i: (0, 0))
    return pl.pallas_call(
        functools.partial(_lru_kernel, tt=tt, nblk=nblk),
        grid=(B, T // tt),
        in_specs=[tile, tile, wspec, wspec, vec, vec, vec],
        out_specs=tile,
        out_shape=jax.ShapeDtypeStruct((B, T, W), BF16),
        scratch_shapes=[pltpu.VMEM((tt, W), F32), pltpu.VMEM((tt, W), F32), pltpu.VMEM((1, W), F32)],
        compiler_params=_params("parallel", "arbitrary"),
        name="lru_scan",
    )(xc, gate, w_a, w_x, b_a, b_x, lam)


def _row(v):
    return v.reshape(1, -1).astype(F32)


def sb_mixer(x2, B, T, norm_g, w_in, q_norm, k_norm, w_out):
    hd = w_in.shape[1] // 3
    w_in = w_in.astype(BF16)
    scale = HEAD_DIM ** -0.5
    q = norm_matmul(x2, norm_g, w_in, col0=0, ncols=hd, epilogue=_ep_headnorm(scale, False),
                    const_extras=(_row(q_norm),), out_dtype=BF16, name="sb_proj_q")
    k = norm_matmul(x2, norm_g, w_in, col0=hd, ncols=hd, epilogue=_ep_headnorm(1.0, False),
                    const_extras=(_row(k_norm),), out_dtype=BF16, name="sb_proj_k")
    v = norm_matmul(x2, norm_g, w_in, col0=2 * hd, ncols=hd, out_dtype=BF16, name="sb_proj_v")
    shp = (B, T, hd)
    o = sb_attention(q.reshape(shp), k.reshape(shp), v.reshape(shp))
    return matmul_residual(o.reshape(B * T, hd), w_out.astype(BF16), x2, name="sb_out")


def gdn_mixer(x2, B, T, norm_g, w_in, conv_w, a_log, dt_bias, o_norm, w_out):
    M = B * T
    Hv = a_log.shape[0]
    vd = Hv * HEAD_DIM
    kd = (conv_w.shape[1] - vd) // 2
    main = 2 * kd + 2 * vd
    proj = norm_matmul(x2, norm_g, w_in.astype(BF16), col0=0, ncols=main, name="gdn_proj")
    w_ba = jnp.pad(w_in[:, main:], ((0, 0), (0, LANES - 2 * Hv))).astype(BF16)
    pad = lambda p: jnp.pad(p.astype(F32), (Hv, LANES - 2 * Hv)).reshape(1, LANES)
    gates = norm_matmul(x2, norm_g, w_ba, epilogue=_ep_gdn_gates(Hv),
                        const_extras=(pad(a_log), pad(dt_bias)), name="gdn_gates")
    proj3 = proj.reshape(B, T, main)
    zero_b = jnp.zeros((1, conv_w.shape[1]), F32)
    conv = functools.partial(causal_conv, proj3, conv_w, zero_b, silu=True, out_dtype=F32)
    q = conv(col0=0, ncols=kd, wcol0=0, l2_scale=HEAD_DIM ** -0.5)
    k = conv(col0=kd, ncols=kd, wcol0=kd, l2_scale=1.0)
    v = conv(col0=2 * kd, ncols=vd, wcol0=2 * kd, l2_scale=None)
    N = T // GDN_CHUNK
    per_head = lambda a: a.reshape(B, T, Hv).transpose(0, 2, 1).reshape(B, Hv, N, 1, GDN_CHUNK)
    beta = per_head(gates[:, :Hv])
    g = per_head(gates[:, Hv:2 * Hv])
    o = gdn_delta_rule(q, k, v, proj3, 2 * kd + vd, g, beta, _row(o_norm))
    return matmul_residual(o.reshape(M, vd), w_out.astype(BF16), x2, name="gdn_out")


def dsw_mixer(x2, B, T, norm_g, positions, w_in, q_norm, k_norm, w_out):
    M = B * T
    hd = w_in.shape[1] // 3
    w_in = w_in.astype(BF16)
    cos_t, sin_t = rope_tables(positions.reshape(M, 1).astype(jnp.int32))
    scale = HEAD_DIM ** -0.5
    tn = 3 * HEAD_DIM * 2
    q = norm_matmul(x2, norm_g, w_in, col0=0, ncols=hd, epilogue=_ep_headnorm(scale, True),
                    row_extras=(cos_t, sin_t), const_extras=(_row(q_norm),), out_dtype=BF16, tn=tn,
                    name="dsw_proj_q")
    k = norm_matmul(x2, norm_g, w_in, col0=hd, ncols=hd, epilogue=_ep_headnorm(1.0, True),
                    row_extras=(cos_t, sin_t), const_extras=(_row(k_norm),), out_dtype=BF16, tn=tn,
                    name="dsw_proj_k")
    v = norm_matmul(x2, norm_g, w_in, col0=2 * hd, ncols=hd, out_dtype=BF16, tn=tn, name="dsw_proj_v")
    shp = (B, T, hd)
    q, k, v = q.reshape(shp), k.reshape(shp), v.reshape(shp)
    os_, lses = [], []
    for gi, (window, dilation) in enumerate(DSW_GROUPS):
        o_g, lse_g = band_attention(q, k, v, gi, window, dilation)
        os_.append(o_g.reshape(M, -1))
        lses.append(lse_g.reshape(M, -1))
    o = combine_groups(os_, lses)
    return matmul_residual(o, w_out.astype(BF16), x2, name="dsw_out")


def lru_mixer(x2, B, T, norm_g, w_in, conv_w, conv_b, w_a, b_a, w_x, b_x, lam, w_out):
    W = w_in.shape[1] // 2
    w_in = w_in.astype(BF16)
    gate = norm_matmul(x2, norm_g, w_in, col0=0, ncols=W, epilogue=_ep_gelu, out_dtype=BF16,
                       name="lru_proj_gate")
    xr = norm_matmul(x2, norm_g, w_in, col0=W, ncols=W, name="lru_proj_x")
    xc = causal_conv(xr.reshape(B, T, W), conv_w, _row(conv_b), col0=0, ncols=W, wcol0=0,
                     silu=False, l2_scale=None, out_dtype=F32)
    y = lru_scan(xc, gate.reshape(B, T, W), w_a.astype(BF16), w_x.astype(BF16),
                 _row(b_a), _row(b_x), _row(lam))
    return matmul_residual(y.reshape(B * T, W), w_out.astype(BF16), x2, name="lru_out")


def kernel(x, positions, mix_norm, ffn_norm, ffn_w_gu, ffn_w_down, sb_w_in, sb_q_norm, sb_k_norm, sb_w_out, gdn_w_in, gdn_conv_w, gdn_a_log, gdn_dt_bias, gdn_o_norm, gdn_w_out, dsw_w_in, dsw_q_norm, dsw_k_norm, dsw_w_out, lru_w_in, lru_conv_w, lru_conv_b, lru_w_a, lru_b_a, lru_w_x, lru_b_x, lru_lambda, lru_w_out):
    B, T, D = x.shape
    depth = mix_norm.shape[0]
    x2 = x.reshape(B * T, D)
    for i in range(depth):
        kind, j = i % 4, i // 4
        if kind == 0:
            x2 = sb_mixer(x2, B, T, mix_norm[i], sb_w_in[j], sb_q_norm[j], sb_k_norm[j], sb_w_out[j])
        elif kind == 1:
            x2 = gdn_mixer(x2, B, T, mix_norm[i], gdn_w_in[j], gdn_conv_w[j], gdn_a_log[j],
                           gdn_dt_bias[j], gdn_o_norm[j], gdn_w_out[j])
        elif kind == 2:
            x2 = dsw_mixer(x2, B, T, mix_norm[i], positions, dsw_w_in[j], dsw_q_norm[j],
                           dsw_k_norm[j], dsw_w_out[j])
        else:
            x2 = lru_mixer(x2, B, T, mix_norm[i], lru_w_in[j], lru_conv_w[j], lru_conv_b[j],
                           lru_w_a[j], lru_b_a[j], lru_w_x[j], lru_b_x[j], lru_lambda[j], lru_w_out[j])
        x2 = ffn(x2, ffn_norm[i], ffn_w_gu[i].astype(BF16), ffn_w_down[i].astype(BF16))
    return x2.reshape(B, T, D)
```

```python
import functools
import math

import jax
import jax.numpy as jnp
from jax import lax
from jax.experimental import pallas as pl
from jax.experimental.pallas import tpu as pltpu

F32 = jnp.float32
BF16 = jnp.bfloat16

NORM_EPS = 1e-6
HEAD_DIM = 128
LANES = 128
SUBLANES = 8
VMEM_LIMIT_BYTES = 56 * 1024 * 1024

GDN_CONV = 4
GDN_CHUNK = 64
DSW_GROUPS = ((128, 1), (512, 4), (2048, 16))
DSW_HEADS_PER_GROUP = 6
DSW_BLOCK = 128
ROPE_DIM = HEAD_DIM // 4
ROPE_THETA = 500000.0
LRU_BLOCK_DIM = 256
LRU_C = 8.0
NEG_BIG = -1e30


def _pick(dim, pref, align):
    if dim <= pref:
        return dim
    t = (pref // align) * align
    while t >= align:
        if dim % t == 0:
            return t
        t -= align
    return dim


def _params(*sem):
    return pltpu.CompilerParams(dimension_semantics=sem, vmem_limit_bytes=VMEM_LIMIT_BYTES)


def _softplus(x):
    return jnp.maximum(x, 0.0) + jnp.log(1.0 + jnp.exp(-jnp.abs(x)))


def _sigmoid(x):
    return 1.0 / (1.0 + jnp.exp(-x))


def _silu(x):
    return x * _sigmoid(x)


def _rms(x, g):
    ms = jnp.mean(x * x, axis=-1, keepdims=True)
    return x * lax.rsqrt(ms + NORM_EPS) * g


def _norm_matmul_kernel(*refs, epilogue, n_extra):
    x_ref, g_ref, w_ref = refs[:3]
    extra = refs[3:3 + n_extra]
    o_ref, xn_ref = refs[3 + n_extra], refs[4 + n_extra]

    @pl.when(pl.program_id(1) == 0)
    def _():
        xn_ref[...] = _rms(x_ref[...], g_ref[...]).astype(BF16)

    if epilogue is None:
        o_ref[...] = jnp.dot(xn_ref[...], w_ref[...], preferred_element_type=F32).astype(o_ref.dtype)
        return
    xn = xn_ref[...]
    ex = [e[...] for e in extra]
    tn = w_ref.shape[1]
    sub = 2 * LANES if tn % (2 * LANES) == 0 else tn
    for c in range(tn // sub):
        cs = slice(c * sub, (c + 1) * sub)
        y = jnp.dot(xn, w_ref[:, cs], preferred_element_type=F32)
        o_ref[:, cs] = epilogue(y, *ex).astype(o_ref.dtype)


def norm_matmul(x, g, w, *, col0=0, ncols=None, epilogue=None, row_extras=(), const_extras=(),
                out_dtype=F32, tm=1024, tn=1024, name="norm_matmul"):
    M, K = x.shape
    ncols = w.shape[1] - col0 if ncols is None else ncols
    tm = _pick(M, tm, SUBLANES)
    tn = _pick(ncols, tn, LANES)
    assert col0 % tn == 0 and ncols % tn == 0 and M % tm == 0
    off = col0 // tn
    in_specs = [
        pl.BlockSpec((tm, K), lambda i, j: (i, 0)),
        pl.BlockSpec((1, K), lambda i, j: (0, 0)),
        pl.BlockSpec((K, tn), lambda i, j: (0, j + off)),
    ]
    args = [x, g.reshape(1, K), w]
    for e in row_extras:
        in_specs.append(pl.BlockSpec((tm, e.shape[1]), lambda i, j: (i, 0)))
        args.append(e)
    for e in const_extras:
        in_specs.append(pl.BlockSpec(e.shape, lambda i, j: (0, 0)))
        args.append(e)
    n_extra = len(row_extras) + len(const_extras)
    return pl.pallas_call(
        functools.partial(_norm_matmul_kernel, epilogue=epilogue, n_extra=n_extra),
        grid=(M // tm, ncols // tn),
        in_specs=in_specs,
        out_specs=pl.BlockSpec((tm, tn), lambda i, j: (i, j)),
        out_shape=jax.ShapeDtypeStruct((M, ncols), out_dtype),
        scratch_shapes=[pltpu.VMEM((tm, K), BF16)],
        compiler_params=_params("parallel", "arbitrary"),
        name=name,
    )(*args)


def _ep_headnorm(scale, rope):
    half = ROPE_DIM // 2

    def ep(y, *extra):
        if rope:
            cos_t, sin_t, gain = extra
            lane = lax.broadcasted_iota(jnp.int32, cos_t.shape, 1)
        else:
            (gain,) = extra
        outs = []
        for h in range(y.shape[1] // HEAD_DIM):
            yh = y[:, h * HEAD_DIM:(h + 1) * HEAD_DIM]
            ms = jnp.mean(yh * yh, axis=-1, keepdims=True)
            yh = yh * lax.rsqrt(ms + NORM_EPS) * gain
            if rope:
                partner = jnp.where(lane < half, pltpu.roll(yh, HEAD_DIM - half, 1),
                                    pltpu.roll(yh, half, 1))
                yh = yh * cos_t + partner * sin_t
            outs.append(yh * scale if scale != 1.0 else yh)
        return outs[0] if len(outs) == 1 else jnp.concatenate(outs, axis=1)

    return ep


def _ep_gelu(y):
    c = math.sqrt(2.0 / math.pi)
    return 0.5 * y * (1.0 + jnp.tanh(c * (y + 0.044715 * (y * y * y))))


def _ep_gdn_gates(n_heads):
    def ep(y, a_log, dt_bias):
        lane = lax.broadcasted_iota(jnp.int32, y.shape, 1)
        beta = _sigmoid(y)
        g = -jnp.exp(a_log) * _softplus(y + dt_bias)
        return jnp.where(lane < n_heads, beta, g)

    return ep


def _matmul_res_kernel(a_ref, w_ref, r_ref, o_ref):
    y = jnp.dot(a_ref[...].astype(BF16), w_ref[...], preferred_element_type=F32)
    o_ref[...] = r_ref[...] + y


def matmul_residual(a, w, res, *, tm=1024, name="out_proj"):
    M, K = a.shape
    N = w.shape[1]
    tm = _pick(M, tm, SUBLANES)
    tn = _pick(N, 1024 if K <= 2560 else 512, LANES)
    return pl.pallas_call(
        _matmul_res_kernel,
        grid=(M // tm, N // tn),
        in_specs=[
            pl.BlockSpec((tm, K), lambda i, j: (i, 0)),
            pl.BlockSpec((K, tn), lambda i, j: (0, j)),
            pl.BlockSpec((tm, tn), lambda i, j: (i, j)),
        ],
        out_specs=pl.BlockSpec((tm, tn), lambda i, j: (i, j)),
        out_shape=jax.ShapeDtypeStruct((M, N), F32),
        compiler_params=_params("parallel", "arbitrary"),
        name=name,
    )(a, w, res)


def _ffn_kernel(x_ref, g_ref, wg_ref, wu_ref, wd_ref, o_ref, xn_ref, *, sub):
    @pl.when(pl.program_id(1) == 0)
    def _():
        x = x_ref[...]
        xn_ref[...] = _rms(x, g_ref[...]).astype(BF16)
        o_ref[...] = x

    xn = xn_ref[...]
    tf = wg_ref.shape[1]
    hs = []
    for c in range(tf // sub):
        cs = slice(c * sub, (c + 1) * sub)
        gate = jnp.dot(xn, wg_ref[:, cs], preferred_element_type=F32)
        up = jnp.dot(xn, wu_ref[:, cs], preferred_element_type=F32)
        hs.append((_silu(gate) * up).astype(BF16))
    h = hs[0] if len(hs) == 1 else jnp.concatenate(hs, axis=1)
    o_ref[...] += jnp.dot(h, wd_ref[...], preferred_element_type=F32)


def ffn(x, g, w_gu, w_down, *, tm=1024, tf=512, sub=256):
    M, D = x.shape
    Fh = w_down.shape[0]
    tm = _pick(M, tm, SUBLANES)
    tf = _pick(Fh, tf, LANES)
    nf = Fh // tf
    return pl.pallas_call(
        functools.partial(_ffn_kernel, sub=min(sub, tf)),
        grid=(M // tm, nf),
        in_specs=[
            pl.BlockSpec((tm, D), lambda i, j: (i, 0), pipeline_mode=pl.Buffered(1)),
            pl.BlockSpec((1, D), lambda i, j: (0, 0)),
            pl.BlockSpec((D, tf), lambda i, j: (0, j)),
            pl.BlockSpec((D, tf), lambda i, j: (0, j + nf)),
            pl.BlockSpec((tf, D), lambda i, j: (j, 0)),
        ],
        out_specs=pl.BlockSpec((tm, D), lambda i, j: (i, 0)),
        out_shape=jax.ShapeDtypeStruct((M, D), F32),
        scratch_shapes=[pltpu.VMEM((tm, D), BF16)],
        compiler_params=_params("parallel", "arbitrary"),
        name="ffn",
    )(x, g.reshape(1, D), w_gu, w_gu, w_down)


SB_STOP = 88.0


def _sb_kernel(q_ref, k_ref, v_ref, o_ref, *, blk, heads, n_q):
    W = 2 * blk
    row = lax.broadcasted_iota(jnp.int32, (blk, W), 0)
    col = lax.broadcasted_iota(jnp.int32, (blk, W), 1)
    r2 = lax.broadcasted_iota(jnp.int32, (W, W), 0)
    c2 = lax.broadcasted_iota(jnp.int32, (W, W), 1)
    suffix_w = jnp.where(r2 >= c2, 1.0, 0.0).astype(BF16)
    rb = lax.broadcasted_iota(jnp.int32, (blk, blk), 0)
    cb = lax.broadcasted_iota(jnp.int32, (blk, blk), 1)
    suffix_b = jnp.where(rb >= cb, 1.0, 0.0).astype(BF16)

    def dot_nt(a, b):
        return lax.dot_general(a, b, (((1,), (1,)), ((), ())), preferred_element_type=F32)

    def suffix_sum(sp, suffix):
        hi = sp.astype(BF16)
        lo = (sp - hi.astype(F32)).astype(BF16)
        return (jnp.dot(hi, suffix, preferred_element_type=F32)
                + jnp.dot(lo, suffix, preferred_element_type=F32))

    def qblock(i, carry):
        q0 = pl.multiple_of(i * blk, blk)
        w0 = pl.multiple_of(jnp.maximum(i - 1, 0) * blk, blk)
        valid = col < row + (q0 - w0)
        hsl = [slice(h * HEAD_DIM, (h + 1) * HEAD_DIM) for h in range(heads)]
        qs = [q_ref[pl.ds(q0, blk), sl] for sl in hsl]
        zs = [dot_nt(q, k_ref[pl.ds(w0, W), sl]) for q, sl in zip(qs, hsl)]
        sps = [jnp.where(valid, _softplus(z), 0.0) for z in zs]
        cums = [suffix_sum(sp, suffix_w) for sp in sps]
        ws = [jnp.exp(jnp.where(valid, z - cum, NEG_BIG)).astype(BF16) for z, cum in zip(zs, cums)]
        accs = [jnp.dot(w, v_ref[pl.ds(w0, W), sl], preferred_element_type=F32)
                for w, sl in zip(ws, hsl)]
        runs = [cum[:, 0:1] for cum in cums]
        for h in range(heads):
            sl = slice(h * HEAD_DIM, (h + 1) * HEAD_DIM)

            def cond(c):
                return (c[0] >= 0) & (jnp.min(c[2]) < SB_STOP)

            def body(c):
                j, acc, run = c
                s0 = pl.multiple_of(j * blk, blk)
                z = dot_nt(qs[h], k_ref[pl.ds(s0, blk), sl])
                cum = suffix_sum(_softplus(z), suffix_b)
                w = jnp.exp(z - cum - run).astype(BF16)
                acc = acc + jnp.dot(w, v_ref[pl.ds(s0, blk), sl], preferred_element_type=F32)
                return j - 1, acc, run + cum[:, 0:1]

            _, acc, _ = lax.while_loop(cond, body, (i - 2, accs[h], runs[h]))
            o_ref[pl.ds(q0, blk), sl] = acc.astype(o_ref.dtype)
        return carry

    lax.fori_loop(0, n_q, qblock, 0)


def sb_attention(q, k, v, *, blk=128, heads=4):
    B, T, HD = q.shape
    H = HD // HEAD_DIM
    assert T % blk == 0 and T >= 2 * blk and H % heads == 0
    seq = pl.BlockSpec((None, T, heads * HEAD_DIM), lambda b, h: (b, 0, h))
    return pl.pallas_call(
        functools.partial(_sb_kernel, blk=blk, heads=heads, n_q=T // blk),
        grid=(B, H // heads),
        in_specs=[seq, seq, seq],
        out_specs=seq,
        out_shape=jax.ShapeDtypeStruct((B, T, HD), BF16),
        compiler_params=_params("parallel", "parallel"),
        name="sb_attention",
    )(q, k, v)


def _conv_kernel(x_ref, halo_ref, w_ref, b_ref, o_ref, buf_ref, *, taps, tt, silu, l2_scale):
    i = pl.program_id(1)
    halo = halo_ref[...]
    buf_ref[0:SUBLANES, :] = jnp.where(i > 0, halo, 0.0)
    buf_ref[SUBLANES:SUBLANES + tt, :] = x_ref[...]
    w = w_ref[...]
    y = b_ref[...] + w[taps - 1:taps, :] * x_ref[...]
    for kk in range(taps - 1):
        shift = taps - 1 - kk
        y = y + w[kk:kk + 1, :] * buf_ref[SUBLANES - shift:SUBLANES - shift + tt, :]
    if silu:
        y = _silu(y)
    if l2_scale is not None:
        outs = []
        for h in range(y.shape[1] // HEAD_DIM):
            yh = y[:, h * HEAD_DIM:(h + 1) * HEAD_DIM]
            ss = jnp.sum(yh * yh, axis=-1, keepdims=True)
            outs.append(yh * (lax.rsqrt(ss + NORM_EPS) * l2_scale))
        y = outs[0] if len(outs) == 1 else jnp.concatenate(outs, axis=1)
    o_ref[...] = y.astype(o_ref.dtype)


def causal_conv(x, w, bias, *, col0, ncols, wcol0, silu, l2_scale, out_dtype, tt=512, tc=512):
    B, T, _ = x.shape
    taps = w.shape[0]
    tt = _pick(T, tt, SUBLANES)
    tc = _pick(ncols, tc, LANES)
    assert col0 % tc == 0 and wcol0 % tc == 0 and T % tt == 0 and tt % SUBLANES == 0
    off, woff = col0 // tc, wcol0 // tc
    rb = tt // SUBLANES
    return pl.pallas_call(
        functools.partial(_conv_kernel, taps=taps, tt=tt, silu=silu, l2_scale=l2_scale),
        grid=(B, T // tt, ncols // tc),
        in_specs=[
            pl.BlockSpec((None, tt, tc), lambda b, i, c: (b, i, c + off)),
            pl.BlockSpec((None, SUBLANES, tc), lambda b, i, c: (b, jnp.maximum(i * rb - 1, 0), c + off)),
            pl.BlockSpec((taps, tc), lambda b, i, c: (0, c + woff)),
            pl.BlockSpec((1, tc), lambda b, i, c: (0, c + woff)),
        ],
        out_specs=pl.BlockSpec((None, tt, tc), lambda b, i, c: (b, i, c)),
        out_shape=jax.ShapeDtypeStruct((B, T, ncols), out_dtype),
        scratch_shapes=[pltpu.VMEM((tt + SUBLANES, tc), F32)],
        compiler_params=_params("parallel", "parallel", "parallel"),
        name="causal_conv",
    )(x, x, w, bias)


def _bdot(a, b):
    return jnp.dot(a.astype(BF16), b.astype(BF16), preferred_element_type=F32)


def _bdot_nt(a, b):
    return lax.dot_general(a.astype(BF16), b.astype(BF16), (((1,), (1,)), ((), ())),
                           preferred_element_type=F32)


def _bdot_tn(a, b):
    return lax.dot_general(a.astype(BF16), b.astype(BF16), (((0,), (0,)), ((), ())),
                           preferred_element_type=F32)


def _gdn_kernel(q_ref, k_ref, v_ref, z_ref, g_ref, beta_ref, gain_ref, o_ref,
                u_ref, wq_ref, kd_ref, attn_ref, gl_ref, out_ref, state_ref, *, chunk, kheads, rep, group):
    C = chunk
    ri = lax.broadcasted_iota(jnp.int32, (C, C), 0)
    ci = lax.broadcasted_iota(jnp.int32, (C, C), 1)
    eye = ri == ci
    lower = ci <= ri
    strict = ci < ri
    gain = gain_ref[...]
    n_double = max(1, int(math.ceil(math.log2(C))) - 1)
    nv = kheads * rep
    heads = [slice(r * HEAD_DIM, (r + 1) * HEAD_DIM) for r in range(nv)]

    @pl.when(pl.program_id(2) == 0)
    def _():
        state_ref[...] = jnp.zeros_like(state_ref)

    G = group
    rows = G * C
    bmm = functools.partial(jnp.einsum, "gij,gjk->gik", preferred_element_type=F32)
    bmm_nt = functools.partial(jnp.einsum, "gid,gjd->gij", preferred_element_type=F32)

    def prepare(kh):
        q = q_ref[:, heads[kh]].reshape(G, C, HEAD_DIM)
        k = k_ref[:, heads[kh]].reshape(G, C, HEAD_DIM)
        k16 = k.astype(BF16)
        qk = bmm_nt(q.astype(BF16), k16)
        for r in range(kh * rep, (kh + 1) * rep):
            v = v_ref[:, heads[r]].reshape(G, C, HEAD_DIM)
            g_row = g_ref[r]
            beta_row = beta_ref[r]
            g_col = jnp.sum(jnp.where(eye, g_row, 0.0), axis=2, keepdims=True)
            beta_col = jnp.sum(jnp.where(eye, beta_row, 0.0), axis=2, keepdims=True)
            gc_col = jnp.sum(jnp.where(lower, g_row, 0.0), axis=2, keepdims=True)
            gc_row = jnp.sum(jnp.where(ri <= ci, g_col, 0.0), axis=1, keepdims=True)
            g_last = jnp.sum(g_row, axis=2, keepdims=True)
            decay = jnp.where(lower, jnp.exp(jnp.where(lower, gc_col - gc_row, 0.0)), 0.0)
            eg = jnp.exp(gc_col)

            kb = k * beta_col
            a_mat = jnp.where(strict, bmm_nt(kb.astype(BF16), k16) * decay, 0.0)
            t_mat = jnp.where(eye, 1.0, 0.0) - a_mat
            p = a_mat
            for _ in range(n_double):
                p16 = p.astype(BF16)
                p = bmm(p16, p16)
                t_mat = t_mat + bmm(t_mat.astype(BF16), p.astype(BF16))
            rhs = jnp.concatenate([v * beta_col, kb * eg], axis=2).astype(BF16)
            uw = bmm(t_mat.astype(BF16), rhs)
            u_ref[r] = uw[:, :, :HEAD_DIM].reshape(rows, HEAD_DIM)
            wq_ref[r] = jnp.concatenate([uw[:, :, HEAD_DIM:], q * eg], axis=1).astype(BF16).reshape(
                2 * rows, HEAD_DIM)
            kd_ref[r] = (k * jnp.exp(g_last - gc_col)).astype(BF16).reshape(rows, HEAD_DIM)
            attn_ref[r] = (qk * decay).astype(BF16).reshape(rows, C)
            gl_ref[r] = jnp.broadcast_to(g_last, (G, 1, HEAD_DIM))

    def recur(nl, states):
        ls = pl.multiple_of(nl * C, C)
        ls2 = pl.multiple_of(nl * 2 * C, 2 * C)
        hs = range(nv)
        s16 = [states[r].astype(BF16) for r in hs]
        ws = [jnp.dot(wq_ref[r, pl.ds(ls2, 2 * C), :], s16[r], preferred_element_type=F32) for r in hs]
        v_new = [(u_ref[r, pl.ds(ls, C), :] - ws[r][:C]).astype(BF16) for r in hs]
        upd = [lax.dot_general(kd_ref[r, pl.ds(ls, C), :], v_new[r], (((0,), (0,)), ((), ())),
                               preferred_element_type=F32) for r in hs]
        new = tuple(states[r] * jnp.exp(gl_ref[r, nl]) + upd[r] for r in hs)
        for r in hs:
            out_ref[r, pl.ds(ls, C), :] = ws[r][C:] + jnp.dot(
                attn_ref[r, pl.ds(ls, C), :], v_new[r], preferred_element_type=F32)
        return new

    for kh in range(kheads):
        prepare(kh)
    states = lax.fori_loop(0, group, recur, tuple(state_ref[r] for r in range(nv)))
    for r in range(nv):
        state_ref[r] = states[r]
        o_ref[:, heads[r]] = (_rms(out_ref[r], gain) * _silu(z_ref[:, heads[r]])).astype(o_ref.dtype)


def gdn_delta_rule(q, k, v, proj, z_col0, g, beta, gain, *, chunk=GDN_CHUNK, group=8, kheads=4):
    B, T, KD = q.shape
    VD = v.shape[2]
    Hk, Hv = KD // HEAD_DIM, VD // HEAD_DIM
    rep = Hv // Hk
    N = T // chunk
    group = _pick(N, group, 1)
    kheads = _pick(Hk, kheads, 1)
    nv = kheads * rep
    kw, vw = kheads * HEAD_DIM, nv * HEAD_DIM
    assert z_col0 % vw == 0 and N % group == 0
    zoff = z_col0 // vw
    rows = group * chunk
    tile = lambda width, f: pl.BlockSpec((None, rows, width), f)
    gate = pl.BlockSpec((None, nv, group, 1, chunk), lambda b, h, t: (b, h, t, 0, 0))
    return pl.pallas_call(
        functools.partial(_gdn_kernel, chunk=chunk, kheads=kheads, rep=rep, group=group),
        grid=(B, Hk // kheads, N // group),
        in_specs=[
            tile(kw, lambda b, h, t: (b, t, h)),
            tile(kw, lambda b, h, t: (b, t, h)),
            tile(vw, lambda b, h, t: (b, t, h)),
            tile(vw, lambda b, h, t: (b, t, h + zoff)),
            gate,
            gate,
            pl.BlockSpec((1, HEAD_DIM), lambda b, h, t: (0, 0)),
        ],
        out_specs=tile(vw, lambda b, h, t: (b, t, h)),
        out_shape=jax.ShapeDtypeStruct((B, T, VD), BF16),
        scratch_shapes=[
            pltpu.VMEM((nv, rows, HEAD_DIM), F32),
            pltpu.VMEM((nv, 2 * rows, HEAD_DIM), BF16),
            pltpu.VMEM((nv, rows, HEAD_DIM), BF16),
            pltpu.VMEM((nv, rows, chunk), BF16),
            pltpu.VMEM((nv, group, 1, HEAD_DIM), F32),
            pltpu.VMEM((nv, rows, HEAD_DIM), F32),
            pltpu.VMEM((nv, HEAD_DIM, HEAD_DIM), F32),
        ],
        compiler_params=_params("parallel", "parallel", "arbitrary"),
        name="gdn_delta_rule",
    )(q, k, v, proj, g, beta, gain)


def _rope_table_kernel(pos_ref, freq_ref, sign_ref, cos_ref, sin_ref):
    ang = pos_ref[...].astype(F32) * freq_ref[...]
    cos_ref[...] = jnp.cos(ang)
    sin_ref[...] = jnp.sin(ang) * sign_ref[...]


def rope_tables(pos, *, tm=1024):
    M = pos.shape[0]
    tm = _pick(M, tm, SUBLANES)
    half = ROPE_DIM // 2
    lane = jnp.arange(HEAD_DIM)
    inv_freq = ROPE_THETA ** (-(lane % half).astype(F32) / half)
    freq = jnp.where(lane < ROPE_DIM, inv_freq, 0.0).reshape(1, HEAD_DIM).astype(F32)
    sign = jnp.where(lane < half, -1.0, 1.0).reshape(1, HEAD_DIM).astype(F32)
    spec = pl.BlockSpec((tm, HEAD_DIM), lambda i: (i, 0))
    const = pl.BlockSpec((1, HEAD_DIM), lambda i: (0, 0))
    return pl.pallas_call(
        _rope_table_kernel,
        grid=(M // tm,),
        in_specs=[pl.BlockSpec((tm, 1), lambda i: (i, 0)), const, const],
        out_specs=[spec, spec],
        out_shape=[jax.ShapeDtypeStruct((M, HEAD_DIM), F32)] * 2,
        compiler_params=_params("parallel"),
        name="rope_tables",
    )(pos, freq, sign)


def _band_kernel(q_ref, kc_ref, kp_ref, vc_ref, vp_ref, o_ref, lse_ref, *, blk, span, heads, d, batch):
    n = pl.program_id(2)
    qi = lax.broadcasted_iota(jnp.int32, (blk, 2 * blk), 0)
    kj = lax.broadcasted_iota(jnp.int32, (blk, 2 * blk), 1)
    steps = blk + qi - kj
    first_key = jnp.where(n > 0, 0, blk)
    valid = (steps >= 0) & (steps <= span) & (kj >= first_key)
    streams = [(r, h) for r in range(d) for h in range(heads)]

    def rows(ref, r, h):
        sl = slice(h * HEAD_DIM, (h + 1) * HEAD_DIM)
        return ref[:, sl] if d == 1 else ref[pl.ds(r, blk, stride=d), sl]

    for b0 in range(0, len(streams), batch):
        part = streams[b0:b0 + batch]
        s_list = []
        for r, h in part:
            q = rows(q_ref, r, h).astype(BF16)
            kw = jnp.concatenate([rows(kp_ref, r, h), rows(kc_ref, r, h)], axis=0).astype(BF16)
            s = lax.dot_general(q, kw, (((1,), (1,)), ((), ())), preferred_element_type=F32)
            s_list.append(jnp.where(valid, s, NEG_BIG))
        m_list = [jnp.max(s, axis=-1, keepdims=True) for s in s_list]
        p_list = [jnp.where(valid, jnp.exp(s - m), 0.0) for s, m in zip(s_list, m_list)]
        den_list = [jnp.sum(p, axis=-1, keepdims=True) for p in p_list]
        for (r, h), p, m, den in zip(part, p_list, m_list, den_list):
            vw = jnp.concatenate([rows(vp_ref, r, h), rows(vc_ref, r, h)], axis=0).astype(BF16)
            o = jnp.dot((p / den).astype(BF16), vw, preferred_element_type=F32)
            lse = jnp.broadcast_to(m + jnp.log(den), (blk, HEAD_DIM))
            sl = slice(h * HEAD_DIM, (h + 1) * HEAD_DIM)
            if d == 1:
                o_ref[:, sl] = o
                lse_ref[:, sl] = lse
            else:
                o_ref[pl.ds(r, blk, stride=d), sl] = o
                lse_ref[pl.ds(r, blk, stride=d), sl] = lse


def band_attention(q, k, v, group, window, dilation, *, blk=DSW_BLOCK, group_heads=DSW_HEADS_PER_GROUP):
    B, T, HD = q.shape
    d = dilation
    span = window // d
    unit = d * blk
    assert span <= blk and T % unit == 0
    heads = group_heads if d == 1 else 1
    assert group_heads % heads == 0
    hw = heads * HEAD_DIM
    col0 = group * group_heads // heads
    cur = pl.BlockSpec((None, unit, hw), lambda b, c, n: (b, n, col0 + c))
    prev = pl.BlockSpec((None, unit, hw), lambda b, c, n: (b, jnp.maximum(n - 1, 0), col0 + c))
    out = pl.BlockSpec((None, unit, hw), lambda b, c, n: (b, n, c))
    return pl.pallas_call(
        functools.partial(_band_kernel, blk=blk, span=span, heads=heads, d=d, batch=6 if d == 1 else 4),
        grid=(B, group_heads // heads, T // unit),
        in_specs=[cur, cur, prev, cur, prev],
        out_specs=[out, out],
        out_shape=[jax.ShapeDtypeStruct((B, T, group_heads * HEAD_DIM), F32)] * 2,
        compiler_params=_params("parallel", "parallel", "arbitrary"),
        name=f"band_attention_d{d}",
    )(q, k, k, v, v)


def _combine_kernel(*refs, n):
    o_refs, l_refs, out_ref = refs[:n], refs[n:2 * n], refs[2 * n]
    ls = [r[...] for r in l_refs]
    m = functools.reduce(jnp.maximum, ls)
    es = [jnp.exp(l - m) for l in ls]
    inv = 1.0 / functools.reduce(lambda a, b: a + b, es)
    w = o_refs[0].shape[1]
    for gi in range(n):
        out_ref[:, gi * w:(gi + 1) * w] = (o_refs[gi][...] * (es[gi] * inv)).astype(out_ref.dtype)


def combine_groups(os_, lses, *, tm=512):
    n = len(os_)
    M, w = os_[0].shape
    tm = _pick(M, tm, SUBLANES)
    spec = pl.BlockSpec((tm, w), lambda i: (i, 0))
    return pl.pallas_call(
        functools.partial(_combine_kernel, n=n),
        grid=(M // tm,),
        in_specs=[spec] * (2 * n),
        out_specs=pl.BlockSpec((tm, n * w), lambda i: (i, 0)),
        out_shape=jax.ShapeDtypeStruct((M, n * w), BF16),
        compiler_params=_params("parallel"),
        name="combine_groups",
    )(*os_, *lses)


def _lru_kernel(x_ref, gate_ref, wa_ref, wx_ref, ba_ref, bx_ref, lam_ref, o_ref,
                a_ref, u_ref, h_ref, *, tt, nblk):
    @pl.when(pl.program_id(1) == 0)
    def _():
        h_ref[...] = jnp.zeros_like(h_ref)

    for nb in range(nblk):
        sl = slice(nb * LRU_BLOCK_DIM, (nb + 1) * LRU_BLOCK_DIM)
        xb = x_ref[:, sl]
        xb16 = xb.astype(BF16)
        r = _sigmoid(jnp.dot(xb16, wa_ref[nb], preferred_element_type=F32) + ba_ref[:, sl])
        ig = _sigmoid(jnp.dot(xb16, wx_ref[nb], preferred_element_type=F32) + bx_ref[:, sl])
        log_a = (-LRU_C) * r * _softplus(-lam_ref[:, sl])
        a_ref[:, sl] = jnp.exp(log_a)
        u_ref[:, sl] = jnp.sqrt(1.0 - jnp.exp(2.0 * log_a)) * (ig * xb)

    def group(gi, h):
        base = pl.multiple_of(gi * SUBLANES, SUBLANES)
        for r8 in range(SUBLANES):
            h = a_ref[pl.ds(base + r8, 1), :] * h + u_ref[pl.ds(base + r8, 1), :]
            u_ref[pl.ds(base + r8, 1), :] = h
        return h

    h_ref[...] = lax.fori_loop(0, tt // SUBLANES, group, h_ref[...])
    o_ref[...] = (u_ref[...] * gate_ref[...].astype(F32)).astype(o_ref.dtype)


def lru_scan(xc, gate, w_a, w_x, b_a, b_x, lam, *, tt=256):
    B, T, W = xc.shape
    nblk = W // LRU_BLOCK_DIM
    tt = _pick(T, tt, SUBLANES)
    tile = pl.BlockSpec((None, tt, W), lambda b, i: (b, i, 0))
    wspec = pl.BlockSpec((nblk, LRU_BLOCK_DIM, LRU_BLOCK_DIM), lambda b, i: (0, 0, 0))
    vec = pl.BlockSpec((1, W), lambda b, i: (0, 0))
    return pl.pallas_call(
        functools.partial(_lru_kernel, tt=tt, nblk=nblk),
        grid=(B, T // tt),
        in_specs=[tile, tile, wspec, wspec, vec, vec, vec],
        out_specs=tile,
        out_shape=jax.ShapeDtypeStruct((B, T, W), BF16),
        scratch_shapes=[pltpu.VMEM((tt, W), F32), pltpu.VMEM((tt, W), F32), pltpu.VMEM((1, W), F32)],
        compiler_params=_params("parallel", "arbitrary"),
        name="lru_scan",
    )(xc, gate, w_a, w_x, b_a, b_x, lam)


def _row(v):
    return v.reshape(1, -1).astype(F32)


def sb_mixer(x2, B, T, norm_g, w_in, q_norm, k_norm, w_out):
    hd = w_in.shape[1] // 3
    w_in = w_in.astype(BF16)
    scale = HEAD_DIM ** -0.5
    q = norm_matmul(x2, norm_g, w_in, col0=0, ncols=hd, epilogue=_ep_headnorm(scale, False),
                    const_extras=(_row(q_norm),), out_dtype=BF16, name="sb_proj_q")
    k = norm_matmul(x2, norm_g, w_in, col0=hd, ncols=hd, epilogue=_ep_headnorm(1.0, False),
                    const_extras=(_row(k_norm),), out_dtype=BF16, name="sb_proj_k")
    v = norm_matmul(x2, norm_g, w_in, col0=2 * hd, ncols=hd, out_dtype=BF16, name="sb_proj_v")
    shp = (B, T, hd)
    o = sb_attention(q.reshape(shp), k.reshape(shp), v.reshape(shp))
    return matmul_residual(o.reshape(B * T, hd), w_out.astype(BF16), x2, name="sb_out")


def gdn_mixer(x2, B, T, norm_g, w_in, conv_w, a_log, dt_bias, o_norm, w_out):
    M = B * T
    Hv = a_log.shape[0]
    vd = Hv * HEAD_DIM
    kd = (conv_w.shape[1] - vd) // 2
    main = 2 * kd + 2 * vd
    proj = norm_matmul(x2, norm_g, w_in.astype(BF16), col0=0, ncols=main, name="gdn_proj")
    w_ba = jnp.pad(w_in[:, main:], ((0, 0), (0, LANES - 2 * Hv))).astype(BF16)
    pad = lambda p: jnp.pad(p.astype(F32), (Hv, LANES - 2 * Hv)).reshape(1, LANES)
    gates = norm_matmul(x2, norm_g, w_ba, epilogue=_ep_gdn_gates(Hv),
                        const_extras=(pad(a_log), pad(dt_bias)), name="gdn_gates")
    proj3 = proj.reshape(B, T, main)
    zero_b = jnp.zeros((1, conv_w.shape[1]), F32)
    conv = functools.partial(causal_conv, proj3, conv_w, zero_b, silu=True, out_dtype=F32)
    q = conv(col0=0, ncols=kd, wcol0=0, l2_scale=HEAD_DIM ** -0.5)
    k = conv(col0=kd, ncols=kd, wcol0=kd, l2_scale=1.0)
    v = conv(col0=2 * kd, ncols=vd, wcol0=2 * kd, l2_scale=None)
    N = T // GDN_CHUNK
    per_head = lambda a: a.reshape(B, T, Hv).transpose(0, 2, 1).reshape(B, Hv, N, 1, GDN_CHUNK)
    beta = per_head(gates[:, :Hv])
    g = per_head(gates[:, Hv:2 * Hv])
    o = gdn_delta_rule(q, k, v, proj3, 2 * kd + vd, g, beta, _row(o_norm))
    return matmul_residual(o.reshape(M, vd), w_out.astype(BF16), x2, name="gdn_out")


def dsw_mixer(x2, B, T, norm_g, positions, w_in, q_norm, k_norm, w_out):
    M = B * T
    hd = w_in.shape[1] // 3
    w_in = w_in.astype(BF16)
    cos_t, sin_t = rope_tables(positions.reshape(M, 1).astype(jnp.int32))
    scale = HEAD_DIM ** -0.5
    tn = 3 * HEAD_DIM * 2
    q = norm_matmul(x2, norm_g, w_in, col0=0, ncols=hd, epilogue=_ep_headnorm(scale, True),
                    row_extras=(cos_t, sin_t), const_extras=(_row(q_norm),), out_dtype=F32, tn=tn,
                    name="dsw_proj_q")
    k = norm_matmul(x2, norm_g, w_in, col0=hd, ncols=hd, epilogue=_ep_headnorm(1.0, True),
                    row_extras=(cos_t, sin_t), const_extras=(_row(k_norm),), out_dtype=F32, tn=tn,
                    name="dsw_proj_k")
    v = norm_matmul(x2, norm_g, w_in, col0=2 * hd, ncols=hd, out_dtype=F32, tn=tn, name="dsw_proj_v")
    shp = (B, T, hd)
    q, k, v = q.reshape(shp), k.reshape(shp), v.reshape(shp)
    os_, lses = [], []
    for gi, (window, dilation) in enumerate(DSW_GROUPS):
        o_g, lse_g = band_attention(q, k, v, gi, window, dilation)
        os_.append(o_g.reshape(M, -1))
        lses.append(lse_g.reshape(M, -1))
    o = combine_groups(os_, lses)
    return matmul_residual(o, w_out.astype(BF16), x2, name="dsw_out")


def lru_mixer(x2, B, T, norm_g, w_in, conv_w, conv_b, w_a, b_a, w_x, b_x, lam, w_out):
    W = w_in.shape[1] // 2
    w_in = w_in.astype(BF16)
    gate = norm_matmul(x2, norm_g, w_in, col0=0, ncols=W, epilogue=_ep_gelu, out_dtype=BF16,
                       name="lru_proj_gate")
    xr = norm_matmul(x2, norm_g, w_in, col0=W, ncols=W, name="lru_proj_x")
    xc = causal_conv(xr.reshape(B, T, W), conv_w, _row(conv_b), col0=0, ncols=W, wcol0=0,
                     silu=False, l2_scale=None, out_dtype=F32)
    y = lru_scan(xc, gate.reshape(B, T, W), w_a.astype(BF16), w_x.astype(BF16),
                 _row(b_a), _row(b_x), _row(lam))
    return matmul_residual(y.reshape(B * T, W), w_out.astype(BF16), x2, name="lru_out")


def kernel(x, positions, mix_norm, ffn_norm, ffn_w_gu, ffn_w_down, sb_w_in, sb_q_norm, sb_k_norm, sb_w_out, gdn_w_in, gdn_conv_w, gdn_a_log, gdn_dt_bias, gdn_o_norm, gdn_w_out, dsw_w_in, dsw_q_norm, dsw_k_norm, dsw_w_out, lru_w_in, lru_conv_w, lru_conv_b, lru_w_a, lru_b_a, lru_w_x, lru_b_x, lru_lambda, lru_w_out):
    B, T, D = x.shape
    depth = mix_norm.shape[0]
    x2 = x.reshape(B * T, D)
    for i in range(depth):
        kind, j = i % 4, i // 4
        if kind == 0:
            x2 = sb_mixer(x2, B, T, mix_norm[i], sb_w_in[j], sb_q_norm[j], sb_k_norm[j], sb_w_out[j])
        elif kind == 1:
            x2 = gdn_mixer(x2, B, T, mix_norm[i], gdn_w_in[j], gdn_conv_w[j], gdn_a_log[j],
                           gdn_dt_bias[j], gdn_o_norm[j], gdn_w_out[j])
        elif kind == 2:
            x2 = dsw_mixer(x2, B, T, mix_norm[i], positions, dsw_w_in[j], dsw_q_norm[j],
                           dsw_k_norm[j], dsw_w_out[j])
        else:
            x2 = lru_mixer(x2, B, T, mix_norm[i], lru_w_in[j], lru_conv_w[j], lru_conv_b[j],
                           lru_w_a[j], lru_b_a[j], lru_w_x[j], lru_b_x[j], lru_lambda[j], lru_w_out[j])
        x2 = ffn(x2, ffn_norm[i], ffn_w_gu[i].astype(BF16), ffn_w_down[i].astype(BF16))
    return x2.reshape(B, T, D)
```

```python
import functools
import math

import jax
import jax.numpy as jnp
from jax import lax
from jax.experimental import pallas as pl
from jax.experimental.pallas import tpu as pltpu

F32 = jnp.float32
BF16 = jnp.bfloat16

NORM_EPS = 1e-6
HEAD_DIM = 128
LANES = 128
SUBLANES = 8
VMEM_LIMIT_BYTES = 56 * 1024 * 1024

GDN_CONV = 4
GDN_CHUNK = 64
DSW_GROUPS = ((128, 1), (512, 4), (2048, 16))
DSW_HEADS_PER_GROUP = 6
DSW_BLOCK = 128
ROPE_DIM = HEAD_DIM // 4
ROPE_THETA = 500000.0
LRU_BLOCK_DIM = 256
LRU_C = 8.0
NEG_BIG = -1e30


def _pick(dim, pref, align):
    if dim <= pref:
        return dim
    t = (pref // align) * align
    while t >= align:
        if dim % t == 0:
            return t
        t -= align
    return dim


def _params(*sem):
    return pltpu.CompilerParams(dimension_semantics=sem, vmem_limit_bytes=VMEM_LIMIT_BYTES)


def _softplus(x):
    return jnp.maximum(x, 0.0) + jnp.log(1.0 + jnp.exp(-jnp.abs(x)))


def _sigmoid(x):
    return 1.0 / (1.0 + jnp.exp(-x))


def _silu(x):
    return x * _sigmoid(x)


def _rms(x, g):
    ms = jnp.mean(x * x, axis=-1, keepdims=True)
    return x * lax.rsqrt(ms + NORM_EPS) * g


def _norm_matmul_kernel(*refs, epilogue, n_extra):
    x_ref, g_ref, w_ref = refs[:3]
    extra = refs[3:3 + n_extra]
    o_ref, xn_ref = refs[3 + n_extra], refs[4 + n_extra]

    @pl.when(pl.program_id(1) == 0)
    def _():
        xn_ref[...] = _rms(x_ref[...], g_ref[...]).astype(BF16)

    y = jnp.dot(xn_ref[...], w_ref[...], preferred_element_type=F32)
    if epilogue is not None:
        y = epilogue(y, *[e[...] for e in extra])
    o_ref[...] = y.astype(o_ref.dtype)


def norm_matmul(x, g, w, *, col0=0, ncols=None, epilogue=None, row_extras=(), const_extras=(),
                out_dtype=F32, tm=1024, tn=1024, pipelined=False, name="norm_matmul"):
    M, K = x.shape
    ncols = w.shape[1] - col0 if ncols is None else ncols
    tm = _pick(M, tm, SUBLANES)
    tn = _pick(ncols, tn, LANES)
    assert col0 % tn == 0 and ncols % tn == 0 and M % tm == 0
    off = col0 // tn
    in_specs = [
        pl.BlockSpec((tm, K), lambda i, j: (i, 0)),
        pl.BlockSpec((1, K), lambda i, j: (0, 0)),
        pl.BlockSpec((K, tn), lambda i, j: (0, j + off)),
    ]
    args = [x, g.reshape(1, K), w]
    for e in row_extras:
        in_specs.append(pl.BlockSpec((tm, e.shape[1]), lambda i, j: (i, 0)))
        args.append(e)
    for e in const_extras:
        in_specs.append(pl.BlockSpec(e.shape, lambda i, j: (0, 0)))
        args.append(e)
    n_extra = len(row_extras) + len(const_extras)
    if pipelined:
        return _norm_matmul_pipelined(x, g, w, off, ncols, epilogue, row_extras, const_extras,
                                      out_dtype, tm, tn, name)
    return pl.pallas_call(
        functools.partial(_norm_matmul_kernel, epilogue=epilogue, n_extra=n_extra),
        grid=(M // tm, ncols // tn),
        in_specs=in_specs,
        out_specs=pl.BlockSpec((tm, tn), lambda i, j: (i, j)),
        out_shape=jax.ShapeDtypeStruct((M, ncols), out_dtype),
        scratch_shapes=[pltpu.VMEM((tm, K), BF16)],
        compiler_params=_params("parallel", "arbitrary"),
        name=name,
    )(*args)


def _norm_matmul_pipe_kernel(*refs, epilogue, n_extra, ncol, ntiles):
    x_ref, g_ref, w_ref = refs[:3]
    extra = refs[3:3 + n_extra]
    o_ref, xn_ref, y_ref = refs[3 + n_extra:6 + n_extra]
    t = pl.program_id(0)

    @pl.when(t == 0)
    def _():
        y_ref[1] = jnp.zeros(y_ref.shape[1:], F32)

    @pl.when((t % ncol == 0) & (t < ntiles))
    def _():
        xn_ref[...] = _rms(x_ref[...], g_ref[...]).astype(BF16)

    slot = t % 2
    o_ref[...] = epilogue(y_ref[1 - slot], *[e[...] for e in extra]).astype(o_ref.dtype)
    y_ref[slot] = jnp.dot(xn_ref[...], w_ref[...], preferred_element_type=F32)


def _norm_matmul_pipelined(x, g, w, off, ncols, epilogue, row_extras, const_extras, out_dtype,
                           tm, tn, name):
    M, K = x.shape
    ncol = ncols // tn
    ntiles = (M // tm) * ncol
    cur = lambda t: jnp.minimum(t, ntiles - 1)
    prv = lambda t: jnp.maximum(t - 1, 0)
    in_specs = [
        pl.BlockSpec((tm, K), lambda t: (cur(t) // ncol, 0)),
        pl.BlockSpec((1, K), lambda t: (0, 0)),
        pl.BlockSpec((K, tn), lambda t: (0, cur(t) % ncol + off)),
    ]
    args = [x, g.reshape(1, K), w]
    for e in row_extras:
        in_specs.append(pl.BlockSpec((tm, e.shape[1]), lambda t: (prv(t) // ncol, 0)))
        args.append(e)
    for e in const_extras:
        in_specs.append(pl.BlockSpec(e.shape, lambda t: (0, 0)))
        args.append(e)
    n_extra = len(row_extras) + len(const_extras)
    return pl.pallas_call(
        functools.partial(_norm_matmul_pipe_kernel, epilogue=epilogue, n_extra=n_extra,
                          ncol=ncol, ntiles=ntiles),
        grid=(ntiles + 1,),
        in_specs=in_specs,
        out_specs=pl.BlockSpec((tm, tn), lambda t: (prv(t) // ncol, prv(t) % ncol)),
        out_shape=jax.ShapeDtypeStruct((M, ncols), out_dtype),
        scratch_shapes=[pltpu.VMEM((tm, K), BF16), pltpu.VMEM((2, tm, tn), F32)],
        compiler_params=_params("arbitrary"),
        name=name,
    )(*args)


def _ep_headnorm(scale, rope):
    half = ROPE_DIM // 2

    def ep(y, *extra):
        yhs = [y[:, h * HEAD_DIM:(h + 1) * HEAD_DIM] for h in range(y.shape[1] // HEAD_DIM)]
        invs = [lax.rsqrt(jnp.mean(yh * yh, axis=-1, keepdims=True) + NORM_EPS) * scale for yh in yhs]
        if rope:
            cos_t, sin_t, gain = extra
            lane = lax.broadcasted_iota(jnp.int32, cos_t.shape, 1)
            gain_b = jnp.broadcast_to(gain, cos_t.shape)
            cg = cos_t * gain
            s_lo = jnp.where(lane < half, pltpu.roll(gain_b, HEAD_DIM - half, 1) * sin_t, 0.0)
            s_hi = jnp.where(lane >= half, pltpu.roll(gain_b, half, 1) * sin_t, 0.0)
            lo = [pltpu.roll(yh, HEAD_DIM - half, 1) for yh in yhs]
            hi = [pltpu.roll(yh, half, 1) for yh in yhs]
            outs = [(yh * cg + a * s_lo + b * s_hi) * inv for yh, a, b, inv in zip(yhs, lo, hi, invs)]
        else:
            (gain,) = extra
            outs = [yh * inv * gain for yh, inv in zip(yhs, invs)]
        return outs[0] if len(outs) == 1 else jnp.concatenate(outs, axis=1)

    return ep


def _ep_gelu(y):
    c = math.sqrt(2.0 / math.pi)
    return 0.5 * y * (1.0 + jnp.tanh(c * (y + 0.044715 * (y * y * y))))


def _ep_gdn_gates(n_heads):
    def ep(y, a_log, dt_bias):
        lane = lax.broadcasted_iota(jnp.int32, y.shape, 1)
        beta = _sigmoid(y)
        g = -jnp.exp(a_log) * _softplus(y + dt_bias)
        return jnp.where(lane < n_heads, beta, g)

    return ep


def _matmul_res_kernel(a_ref, w_ref, r_ref, o_ref):
    y = jnp.dot(a_ref[...].astype(BF16), w_ref[...], preferred_element_type=F32)
    o_ref[...] = r_ref[...] + y


def matmul_residual(a, w, res, *, tm=1024, name="out_proj"):
    M, K = a.shape
    N = w.shape[1]
    tm = _pick(M, tm, SUBLANES)
    tn = _pick(N, 1024 if K <= 2560 else 512, LANES)
    return pl.pallas_call(
        _matmul_res_kernel,
        grid=(M // tm, N // tn),
        in_specs=[
            pl.BlockSpec((tm, K), lambda i, j: (i, 0)),
            pl.BlockSpec((K, tn), lambda i, j: (0, j)),
            pl.BlockSpec((tm, tn), lambda i, j: (i, j)),
        ],
        out_specs=pl.BlockSpec((tm, tn), lambda i, j: (i, j)),
        out_shape=jax.ShapeDtypeStruct((M, N), F32),
        compiler_params=_params("parallel", "arbitrary"),
        name=name,
    )(a, w, res)


def _ffn_kernel(x_ref, g_ref, wg_ref, wu_ref, wd_ref, o_ref, xn_ref, *, sub):
    @pl.when(pl.program_id(1) == 0)
    def _():
        x = x_ref[...]
        xn_ref[...] = _rms(x, g_ref[...]).astype(BF16)
        o_ref[...] = x

    xn = xn_ref[...]
    tf = wg_ref.shape[1]
    hs = []
    for c in range(tf // sub):
        cs = slice(c * sub, (c + 1) * sub)
        gate = jnp.dot(xn, wg_ref[:, cs], preferred_element_type=F32)
        up = jnp.dot(xn, wu_ref[:, cs], preferred_element_type=F32)
        hs.append((_silu(gate) * up).astype(BF16))
    h = hs[0] if len(hs) == 1 else jnp.concatenate(hs, axis=1)
    o_ref[...] += jnp.dot(h, wd_ref[...], preferred_element_type=F32)


def ffn(x, g, w_gu, w_down, *, tm=512, tf=512, sub=256):
    M, D = x.shape
    Fh = w_down.shape[0]
    tm = _pick(M, tm, SUBLANES)
    tf = _pick(Fh, tf, LANES)
    nf = Fh // tf
    return pl.pallas_call(
        functools.partial(_ffn_kernel, sub=min(sub, tf)),
        grid=(M // tm, nf),
        in_specs=[
            pl.BlockSpec((tm, D), lambda i, j: (i, 0)),
            pl.BlockSpec((1, D), lambda i, j: (0, 0)),
            pl.BlockSpec((D, tf), lambda i, j: (0, j)),
            pl.BlockSpec((D, tf), lambda i, j: (0, j + nf)),
            pl.BlockSpec((tf, D), lambda i, j: (j, 0)),
        ],
        out_specs=pl.BlockSpec((tm, D), lambda i, j: (i, 0)),
        out_shape=jax.ShapeDtypeStruct((M, D), F32),
        scratch_shapes=[pltpu.VMEM((tm, D), BF16)],
        compiler_params=_params("parallel", "arbitrary"),
        name="ffn",
    )(x, g.reshape(1, D), w_gu, w_gu, w_down)


SB_STOP = 88.0


def _sb_kernel(q_ref, k_ref, v_ref, o_ref, *, blk, heads, n_q):
    W = 2 * blk
    row = lax.broadcasted_iota(jnp.int32, (blk, W), 0)
    col = lax.broadcasted_iota(jnp.int32, (blk, W), 1)
    r2 = lax.broadcasted_iota(jnp.int32, (W, W), 0)
    c2 = lax.broadcasted_iota(jnp.int32, (W, W), 1)
    suffix_w = jnp.where(r2 >= c2, 1.0, 0.0).astype(BF16)
    rb = lax.broadcasted_iota(jnp.int32, (blk, blk), 0)
    cb = lax.broadcasted_iota(jnp.int32, (blk, blk), 1)
    suffix_b = jnp.where(rb >= cb, 1.0, 0.0).astype(BF16)

    def dot_nt(a, b):
        return lax.dot_general(a, b, (((1,), (1,)), ((), ())), preferred_element_type=F32)

    def suffix_sum(sp, suffix):
        hi = sp.astype(BF16)
        lo = (sp - hi.astype(F32)).astype(BF16)
        return (jnp.dot(hi, suffix, preferred_element_type=F32)
                + jnp.dot(lo, suffix, preferred_element_type=F32))

    def qblock(i, carry):
        q0 = pl.multiple_of(i * blk, blk)
        w0 = pl.multiple_of(jnp.maximum(i - 1, 0) * blk, blk)
        valid = col < row + (q0 - w0)
        hsl = [slice(h * HEAD_DIM, (h + 1) * HEAD_DIM) for h in range(heads)]
        qs = [q_ref[pl.ds(q0, blk), sl] for sl in hsl]
        zs = [dot_nt(q, k_ref[pl.ds(w0, W), sl]) for q, sl in zip(qs, hsl)]
        sps = [jnp.where(valid, _softplus(z), 0.0) for z in zs]
        cums = [suffix_sum(sp, suffix_w) for sp in sps]
        ws = [jnp.exp(jnp.where(valid, z - cum, NEG_BIG)).astype(BF16) for z, cum in zip(zs, cums)]
        accs = [jnp.dot(w, v_ref[pl.ds(w0, W), sl], preferred_element_type=F32)
                for w, sl in zip(ws, hsl)]
        runs = [cum[:, 0:1] for cum in cums]
        for h in range(heads):
            sl = slice(h * HEAD_DIM, (h + 1) * HEAD_DIM)

            def cond(c):
                return (c[0] >= 0) & (jnp.min(c[2]) < SB_STOP)

            def body(c):
                j, acc, run = c
                s0 = pl.multiple_of(j * blk, blk)
                z = dot_nt(qs[h], k_ref[pl.ds(s0, blk), sl])
                cum = suffix_sum(_softplus(z), suffix_b)
                w = jnp.exp(z - cum - run).astype(BF16)
                acc = acc + jnp.dot(w, v_ref[pl.ds(s0, blk), sl], preferred_element_type=F32)
                return j - 1, acc, run + cum[:, 0:1]

            _, acc, _ = lax.while_loop(cond, body, (i - 2, accs[h], runs[h]))
            o_ref[pl.ds(q0, blk), sl] = acc.astype(o_ref.dtype)
        return carry

    lax.fori_loop(0, n_q, qblock, 0)


def sb_attention(q, k, v, *, blk=128, heads=4):
    B, T, HD = q.shape
    H = HD // HEAD_DIM
    assert T % blk == 0 and T >= 2 * blk and H % heads == 0
    seq = pl.BlockSpec((None, T, heads * HEAD_DIM), lambda b, h: (b, 0, h))
    return pl.pallas_call(
        functools.partial(_sb_kernel, blk=blk, heads=heads, n_q=T // blk),
        grid=(B, H // heads),
        in_specs=[seq, seq, seq],
        out_specs=seq,
        out_shape=jax.ShapeDtypeStruct((B, T, HD), BF16),
        compiler_params=_params("parallel", "parallel"),
        name="sb_attention",
    )(q, k, v)


def _bdot(a, b):
    return jnp.dot(a.astype(BF16), b.astype(BF16), preferred_element_type=F32)


def _bdot_nt(a, b):
    return lax.dot_general(a.astype(BF16), b.astype(BF16), (((1,), (1,)), ((), ())),
                           preferred_element_type=F32)


def _bdot_tn(a, b):
    return lax.dot_general(a.astype(BF16), b.astype(BF16), (((0,), (0,)), ((), ())),
                           preferred_element_type=F32)


def _gdn_kernel(xq_ref, xk_ref, xv_ref, z_ref, cwq_ref, cwk_ref, cwv_ref, g_ref, beta_ref, gain_ref,
                o_ref, q_ref, k_ref, v_ref, hq_ref, hk_ref, hv_ref, buf_ref,
                u_ref, wq_ref, kd_ref, attn_ref, gl_ref, out_ref, state_ref,
                *, chunk, kheads, rep, group, taps):
    C = chunk

    @pl.when(pl.program_id(2) == 0)
    def _():
        state_ref[...] = jnp.zeros_like(state_ref)
        hq_ref[...] = jnp.zeros_like(hq_ref)
        hk_ref[...] = jnp.zeros_like(hk_ref)
        hv_ref[...] = jnp.zeros_like(hv_ref)

    def conv_silu(x_ref, halo_ref, w_ref, dst_ref, l2_scale):
        n, width = x_ref.shape
        buf_ref[0:SUBLANES, 0:width] = halo_ref[...]
        buf_ref[SUBLANES:SUBLANES + n, 0:width] = x_ref[...]
        halo_ref[...] = x_ref[n - SUBLANES:n, :]
        w = w_ref[...]
        y = w[taps - 1:taps, :] * x_ref[...]
        for kk in range(taps - 1):
            shift = taps - 1 - kk
            y = y + w[kk:kk + 1, :] * buf_ref[SUBLANES - shift:SUBLANES - shift + n, 0:width]
        y = _silu(y)
        for h in range(width // HEAD_DIM):
            sl = slice(h * HEAD_DIM, (h + 1) * HEAD_DIM)
            yh = y[:, sl]
            if l2_scale is not None:
                ss = jnp.sum(yh * yh, axis=-1, keepdims=True)
                yh = yh * (lax.rsqrt(ss + NORM_EPS) * l2_scale)
            dst_ref[:, sl] = yh

    conv_silu(xq_ref, hq_ref, cwq_ref, q_ref, HEAD_DIM ** -0.5)
    conv_silu(xk_ref, hk_ref, cwk_ref, k_ref, 1.0)
    conv_silu(xv_ref, hv_ref, cwv_ref, v_ref, None)

    ri = lax.broadcasted_iota(jnp.int32, (C, C), 0)
    ci = lax.broadcasted_iota(jnp.int32, (C, C), 1)
    eye = ri == ci
    lower = ci <= ri
    strict = ci < ri
    gain = gain_ref[...]
    n_double = max(1, int(math.ceil(math.log2(C))) - 1)
    nv = kheads * rep
    heads = [slice(r * HEAD_DIM, (r + 1) * HEAD_DIM) for r in range(nv)]
    G = group
    rows = G * C
    bmm = functools.partial(jnp.einsum, "gij,gjk->gik", preferred_element_type=F32)
    bmm_nt = functools.partial(jnp.einsum, "gid,gjd->gij", preferred_element_type=F32)

    def prepare(kh):
        q = q_ref[:, heads[kh]].reshape(G, C, HEAD_DIM)
        k = k_ref[:, heads[kh]].reshape(G, C, HEAD_DIM)
        k16 = k.astype(BF16)
        qk = bmm_nt(q.astype(BF16), k16)
        for r in range(kh * rep, (kh + 1) * rep):
            v = v_ref[:, heads[r]].reshape(G, C, HEAD_DIM)
            g_row = g_ref[r]
            beta_row = beta_ref[r]
            g_col = jnp.sum(jnp.where(eye, g_row, 0.0), axis=2, keepdims=True)
            beta_col = jnp.sum(jnp.where(eye, beta_row, 0.0), axis=2, keepdims=True)
            gc_col = jnp.sum(jnp.where(lower, g_row, 0.0), axis=2, keepdims=True)
            gc_row = jnp.sum(jnp.where(ri <= ci, g_col, 0.0), axis=1, keepdims=True)
            g_last = jnp.sum(g_row, axis=2, keepdims=True)
            decay = jnp.where(lower, jnp.exp(jnp.where(lower, gc_col - gc_row, 0.0)), 0.0)
            eg = jnp.exp(gc_col)

            kb = k * beta_col
            a_mat = jnp.where(strict, bmm_nt(kb.astype(BF16), k16) * decay, 0.0)
            t_mat = jnp.where(eye, 1.0, 0.0) - a_mat
            p = a_mat
            for _ in range(n_double):
                p16 = p.astype(BF16)
                p = bmm(p16, p16)
                t_mat = t_mat + bmm(t_mat.astype(BF16), p.astype(BF16))
            rhs = jnp.concatenate([v * beta_col, kb * eg], axis=2).astype(BF16)
            uw = bmm(t_mat.astype(BF16), rhs)
            u_ref[r] = uw[:, :, :HEAD_DIM].reshape(rows, HEAD_DIM)
            wq_ref[r] = jnp.concatenate([uw[:, :, HEAD_DIM:], q * eg], axis=1).astype(BF16).reshape(
                2 * rows, HEAD_DIM)
            kd_ref[r] = (k * jnp.exp(g_last - gc_col)).astype(BF16).reshape(rows, HEAD_DIM)
            attn_ref[r] = (qk * decay).astype(BF16).reshape(rows, C)
            gl_ref[r] = jnp.broadcast_to(g_last, (G, 1, HEAD_DIM))

    def recur(nl, states):
        ls = pl.multiple_of(nl * C, C)
        ls2 = pl.multiple_of(nl * 2 * C, 2 * C)
        hs = range(nv)
        s16 = [states[r].astype(BF16) for r in hs]
        ws = [jnp.dot(wq_ref[r, pl.ds(ls2, 2 * C), :], s16[r], preferred_element_type=F32) for r in hs]
        v_new = [(u_ref[r, pl.ds(ls, C), :] - ws[r][:C]).astype(BF16) for r in hs]
        upd = [lax.dot_general(kd_ref[r, pl.ds(ls, C), :], v_new[r], (((0,), (0,)), ((), ())),
                               preferred_element_type=F32) for r in hs]
        new = tuple(states[r] * jnp.exp(gl_ref[r, nl]) + upd[r] for r in hs)
        for r in hs:
            out_ref[r, pl.ds(ls, C), :] = ws[r][C:] + jnp.dot(
                attn_ref[r, pl.ds(ls, C), :], v_new[r], preferred_element_type=F32)
        return new

    for kh in range(kheads):
        prepare(kh)
    states = lax.fori_loop(0, group, recur, tuple(state_ref[r] for r in range(nv)))
    for r in range(nv):
        state_ref[r] = states[r]
        o_ref[:, heads[r]] = (_rms(out_ref[r], gain) * _silu(z_ref[:, heads[r]])).astype(o_ref.dtype)


def gdn_delta_rule(proj, conv_w, KD, VD, g, beta, gain, *, chunk=GDN_CHUNK, group=8, kheads=4):
    B, T, _ = proj.shape
    taps = conv_w.shape[0]
    Hk, Hv = KD // HEAD_DIM, VD // HEAD_DIM
    rep = Hv // Hk
    N = T // chunk
    group = _pick(N, group, 1)
    kheads = _pick(Hk, kheads, 1)
    nv = kheads * rep
    kw, vw = kheads * HEAD_DIM, nv * HEAD_DIM
    assert KD % kw == 0 and (2 * KD) % vw == 0 and N % group == 0
    koff, voff, zoff = KD // kw, 2 * KD // vw, (2 * KD + VD) // vw
    rows = group * chunk
    assert rows % SUBLANES == 0
    tile = lambda width, f: pl.BlockSpec((None, rows, width), f)
    tapw = lambda width, f: pl.BlockSpec((taps, width), f)
    gate = pl.BlockSpec((None, nv, group, 1, chunk), lambda b, h, t: (b, h, t, 0, 0))
    return pl.pallas_call(
        functools.partial(_gdn_kernel, chunk=chunk, kheads=kheads, rep=rep, group=group, taps=taps),
        grid=(B, Hk // kheads, N // group),
        in_specs=[
            tile(kw, lambda b, h, t: (b, t, h)),
            tile(kw, lambda b, h, t: (b, t, h + koff)),
            tile(vw, lambda b, h, t: (b, t, h + voff)),
            tile(vw, lambda b, h, t: (b, t, h + zoff)),
            tapw(kw, lambda b, h, t: (0, h)),
            tapw(kw, lambda b, h, t: (0, h + koff)),
            tapw(vw, lambda b, h, t: (0, h + voff)),
            gate,
            gate,
            pl.BlockSpec((1, HEAD_DIM), lambda b, h, t: (0, 0)),
        ],
        out_specs=tile(vw, lambda b, h, t: (b, t, h)),
        out_shape=jax.ShapeDtypeStruct((B, T, VD), BF16),
        scratch_shapes=[
            pltpu.VMEM((rows, kw), F32),
            pltpu.VMEM((rows, kw), F32),
            pltpu.VMEM((rows, vw), F32),
            pltpu.VMEM((SUBLANES, kw), F32),
            pltpu.VMEM((SUBLANES, kw), F32),
            pltpu.VMEM((SUBLANES, vw), F32),
            pltpu.VMEM((rows + SUBLANES, vw), F32),
            pltpu.VMEM((nv, rows, HEAD_DIM), F32),
            pltpu.VMEM((nv, 2 * rows, HEAD_DIM), BF16),
            pltpu.VMEM((nv, rows, HEAD_DIM), BF16),
            pltpu.VMEM((nv, rows, chunk), BF16),
            pltpu.VMEM((nv, group, 1, HEAD_DIM), F32),
            pltpu.VMEM((nv, rows, HEAD_DIM), F32),
            pltpu.VMEM((nv, HEAD_DIM, HEAD_DIM), F32),
        ],
        compiler_params=_params("parallel", "parallel", "arbitrary"),
        name="gdn_delta_rule",
    )(proj, proj, proj, proj, conv_w, conv_w, conv_w, g, beta, gain)


def _rope_table_kernel(pos_ref, freq_ref, sign_ref, cos_ref, sin_ref):
    ang = pos_ref[...].astype(F32) * freq_ref[...]
    cos_ref[...] = jnp.cos(ang)
    sin_ref[...] = jnp.sin(ang) * sign_ref[...]


def rope_tables(pos, *, tm=1024):
    M = pos.shape[0]
    tm = _pick(M, tm, SUBLANES)
    half = ROPE_DIM // 2
    lane = jnp.arange(HEAD_DIM)
    inv_freq = ROPE_THETA ** (-(lane % half).astype(F32) / half)
    freq = jnp.where(lane < ROPE_DIM, inv_freq, 0.0).reshape(1, HEAD_DIM).astype(F32)
    sign = jnp.where(lane < half, -1.0, 1.0).reshape(1, HEAD_DIM).astype(F32)
    spec = pl.BlockSpec((tm, HEAD_DIM), lambda i: (i, 0))
    const = pl.BlockSpec((1, HEAD_DIM), lambda i: (0, 0))
    return pl.pallas_call(
        _rope_table_kernel,
        grid=(M // tm,),
        in_specs=[pl.BlockSpec((tm, 1), lambda i: (i, 0)), const, const],
        out_specs=[spec, spec],
        out_shape=[jax.ShapeDtypeStruct((M, HEAD_DIM), F32)] * 2,
        compiler_params=_params("parallel"),
        name="rope_tables",
    )(pos, freq, sign)


def _band_kernel(q_ref, kc_ref, kp_ref, vc_ref, vp_ref, o_ref, lse_ref, *, blk, span, heads, d, batch):
    n = pl.program_id(2)
    qi = lax.broadcasted_iota(jnp.int32, (blk, 2 * blk), 0)
    kj = lax.broadcasted_iota(jnp.int32, (blk, 2 * blk), 1)
    steps = blk + qi - kj
    first_key = jnp.where(n > 0, 0, blk)
    valid = (steps >= 0) & (steps <= span) & (kj >= first_key)
    streams = [(r, h) for r in range(d) for h in range(heads)]

    def rows(ref, r, h):
        sl = slice(h * HEAD_DIM, (h + 1) * HEAD_DIM)
        return ref[:, sl] if d == 1 else ref[pl.ds(r, blk, stride=d), sl]

    for b0 in range(0, len(streams), batch):
        part = streams[b0:b0 + batch]
        s_list = []
        for r, h in part:
            q = rows(q_ref, r, h).astype(BF16)
            kw = jnp.concatenate([rows(kp_ref, r, h), rows(kc_ref, r, h)], axis=0).astype(BF16)
            s = lax.dot_general(q, kw, (((1,), (1,)), ((), ())), preferred_element_type=F32)
            s_list.append(jnp.where(valid, s, NEG_BIG))
        m_list = [jnp.max(s, axis=-1, keepdims=True) for s in s_list]
        p_list = [jnp.where(valid, jnp.exp(s - m), 0.0) for s, m in zip(s_list, m_list)]
        den_list = [jnp.sum(p, axis=-1, keepdims=True) for p in p_list]
        for (r, h), p, m, den in zip(part, p_list, m_list, den_list):
            vw = jnp.concatenate([rows(vp_ref, r, h), rows(vc_ref, r, h)], axis=0).astype(BF16)
            o = jnp.dot((p / den).astype(BF16), vw, preferred_element_type=F32)
            lse = jnp.broadcast_to(m + jnp.log(den), (blk, HEAD_DIM))
            sl = slice(h * HEAD_DIM, (h + 1) * HEAD_DIM)
            if d == 1:
                o_ref[:, sl] = o
                lse_ref[:, sl] = lse
            else:
                o_ref[pl.ds(r, blk, stride=d), sl] = o
                lse_ref[pl.ds(r, blk, stride=d), sl] = lse


def band_attention(q, k, v, group, window, dilation, *, blk=DSW_BLOCK, group_heads=DSW_HEADS_PER_GROUP):
    B, T, HD = q.shape
    d = dilation
    span = window // d
    unit = d * blk
    assert span <= blk and T % unit == 0
    heads = group_heads if d == 1 else 1
    assert group_heads % heads == 0
    hw = heads * HEAD_DIM
    col0 = group * group_heads // heads
    cur = pl.BlockSpec((None, unit, hw), lambda b, c, n: (b, n, col0 + c))
    prev = pl.BlockSpec((None, unit, hw), lambda b, c, n: (b, jnp.maximum(n - 1, 0), col0 + c))
    out = pl.BlockSpec((None, unit, hw), lambda b, c, n: (b, n, c))
    return pl.pallas_call(
        functools.partial(_band_kernel, blk=blk, span=span, heads=heads, d=d, batch=6 if d == 1 else 4),
        grid=(B, group_heads // heads, T // unit),
        in_specs=[cur, cur, prev, cur, prev],
        out_specs=[out, out],
        out_shape=[jax.ShapeDtypeStruct((B, T, group_heads * HEAD_DIM), F32)] * 2,
        compiler_params=_params("parallel", "parallel", "arbitrary"),
        name=f"band_attention_d{d}",
    )(q, k, k, v, v)


def _combine_kernel(*refs, n):
    o_refs, l_refs, out_ref = refs[:n], refs[n:2 * n], refs[2 * n]
    ls = [r[...] for r in l_refs]
    m = functools.reduce(jnp.maximum, ls)
    es = [jnp.exp(l - m) for l in ls]
    inv = 1.0 / functools.reduce(lambda a, b: a + b, es)
    w = o_refs[0].shape[1]
    for gi in range(n):
        out_ref[:, gi * w:(gi + 1) * w] = (o_refs[gi][...] * (es[gi] * inv)).astype(out_ref.dtype)


def combine_groups(os_, lses, *, tm=512):
    n = len(os_)
    M, w = os_[0].shape
    tm = _pick(M, tm, SUBLANES)
    spec = pl.BlockSpec((tm, w), lambda i: (i, 0))
    return pl.pallas_call(
        functools.partial(_combine_kernel, n=n),
        grid=(M // tm,),
        in_specs=[spec] * (2 * n),
        out_specs=pl.BlockSpec((tm, n * w), lambda i: (i, 0)),
        out_shape=jax.ShapeDtypeStruct((M, n * w), BF16),
        compiler_params=_params("parallel"),
        name="combine_groups",
    )(*os_, *lses)


def _lru_kernel(x_ref, gate_ref, cw_ref, cb_ref, wa_ref, wx_ref, ba_ref, bx_ref, lam_ref, o_ref,
                a_ref, u_ref, h_ref, xc_ref, halo_ref, buf_ref, *, tt, nblk, taps):
    @pl.when(pl.program_id(1) == 0)
    def _():
        h_ref[...] = jnp.zeros_like(h_ref)
        halo_ref[...] = jnp.zeros_like(halo_ref)

    buf_ref[0:SUBLANES, :] = halo_ref[...]
    buf_ref[SUBLANES:SUBLANES + tt, :] = x_ref[...]
    halo_ref[...] = x_ref[tt - SUBLANES:tt, :]
    cw = cw_ref[...]
    xc = cb_ref[...] + cw[taps - 1:taps, :] * x_ref[...]
    for kk in range(taps - 1):
        shift = taps - 1 - kk
        xc = xc + cw[kk:kk + 1, :] * buf_ref[SUBLANES - shift:SUBLANES - shift + tt, :]
    xc_ref[...] = xc

    for nb in range(nblk):
        sl = slice(nb * LRU_BLOCK_DIM, (nb + 1) * LRU_BLOCK_DIM)
        xb = xc_ref[:, sl]
        xb16 = xb.astype(BF16)
        r = _sigmoid(jnp.dot(xb16, wa_ref[nb], preferred_element_type=F32) + ba_ref[:, sl])
        ig = _sigmoid(jnp.dot(xb16, wx_ref[nb], preferred_element_type=F32) + bx_ref[:, sl])
        log_a = (-LRU_C) * r * _softplus(-lam_ref[:, sl])
        a_ref[:, sl] = jnp.exp(log_a)
        u_ref[:, sl] = jnp.sqrt(1.0 - jnp.exp(2.0 * log_a)) * (ig * xb)

    def group(gi, h):
        base = pl.multiple_of(gi * SUBLANES, SUBLANES)
        for r8 in range(SUBLANES):
            h = a_ref[pl.ds(base + r8, 1), :] * h + u_ref[pl.ds(base + r8, 1), :]
            u_ref[pl.ds(base + r8, 1), :] = h
        return h

    h_ref[...] = lax.fori_loop(0, tt // SUBLANES, group, h_ref[...])
    o_ref[...] = (u_ref[...] * gate_ref[...].astype(F32)).astype(o_ref.dtype)


def lru_scan(xr, gate, conv_w, conv_b, w_a, w_x, b_a, b_x, lam, *, tt=256):
    B, T, W = xr.shape
    nblk = W // LRU_BLOCK_DIM
    taps = conv_w.shape[0]
    tt = _pick(T, tt, SUBLANES)
    tile = pl.BlockSpec((None, tt, W), lambda b, i: (b, i, 0))
    wspec = pl.BlockSpec((nblk, LRU_BLOCK_DIM, LRU_BLOCK_DIM), lambda b, i: (0, 0, 0))
    vec = pl.BlockSpec((1, W), lambda b, i: (0, 0))
    return pl.pallas_call(
        functools.partial(_lru_kernel, tt=tt, nblk=nblk, taps=taps),
        grid=(B, T // tt),
        in_specs=[tile, tile, pl.BlockSpec((taps, W), lambda b, i: (0, 0)), vec,
                  wspec, wspec, vec, vec, vec],
        out_specs=tile,
        out_shape=jax.ShapeDtypeStruct((B, T, W), BF16),
        scratch_shapes=[
            pltpu.VMEM((tt, W), F32),
            pltpu.VMEM((tt, W), F32),
            pltpu.VMEM((1, W), F32),
            pltpu.VMEM((tt, W), F32),
            pltpu.VMEM((SUBLANES, W), F32),
            pltpu.VMEM((tt + SUBLANES, W), F32),
        ],
        compiler_params=_params("parallel", "arbitrary"),
        name="lru_scan",
    )(xr, gate, conv_w, conv_b, w_a, w_x, b_a, b_x, lam)


def _row(v):
    return v.reshape(1, -1).astype(F32)


def sb_mixer(x2, B, T, norm_g, w_in, q_norm, k_norm, w_out):
    hd = w_in.shape[1] // 3
    w_in = w_in.astype(BF16)
    scale = HEAD_DIM ** -0.5
    q = norm_matmul(x2, norm_g, w_in, col0=0, ncols=hd, epilogue=_ep_headnorm(scale, False),
                    const_extras=(_row(q_norm),), out_dtype=BF16, name="sb_proj_q")
    k = norm_matmul(x2, norm_g, w_in, col0=hd, ncols=hd, epilogue=_ep_headnorm(1.0, False),
                    const_extras=(_row(k_norm),), out_dtype=BF16, name="sb_proj_k")
    v = norm_matmul(x2, norm_g, w_in, col0=2 * hd, ncols=hd, out_dtype=BF16, name="sb_proj_v")
    shp = (B, T, hd)
    o = sb_attention(q.reshape(shp), k.reshape(shp), v.reshape(shp))
    return matmul_residual(o.reshape(B * T, hd), w_out.astype(BF16), x2, name="sb_out")


def gdn_mixer(x2, B, T, norm_g, w_in, conv_w, a_log, dt_bias, o_norm, w_out):
    M = B * T
    Hv = a_log.shape[0]
    vd = Hv * HEAD_DIM
    kd = (conv_w.shape[1] - vd) // 2
    main = 2 * kd + 2 * vd
    proj = norm_matmul(x2, norm_g, w_in.astype(BF16), col0=0, ncols=main, name="gdn_proj")
    w_ba = jnp.pad(w_in[:, main:], ((0, 0), (0, LANES - 2 * Hv))).astype(BF16)
    pad = lambda p: jnp.pad(p.astype(F32), (Hv, LANES - 2 * Hv)).reshape(1, LANES)
    gates = norm_matmul(x2, norm_g, w_ba, epilogue=_ep_gdn_gates(Hv),
                        const_extras=(pad(a_log), pad(dt_bias)), name="gdn_gates")
    N = T // GDN_CHUNK
    per_head = lambda a: a.reshape(B, T, Hv).transpose(0, 2, 1).reshape(B, Hv, N, 1, GDN_CHUNK)
    beta = per_head(gates[:, :Hv])
    g = per_head(gates[:, Hv:2 * Hv])
    o = gdn_delta_rule(proj.reshape(B, T, main), conv_w, kd, vd, g, beta, _row(o_norm))
    return matmul_residual(o.reshape(M, vd), w_out.astype(BF16), x2, name="gdn_out")


def dsw_mixer(x2, B, T, norm_g, positions, w_in, q_norm, k_norm, w_out):
    M = B * T
    hd = w_in.shape[1] // 3
    w_in = w_in.astype(BF16)
    cos_t, sin_t = rope_tables(positions.reshape(M, 1).astype(jnp.int32))
    scale = HEAD_DIM ** -0.5
    tn = 3 * HEAD_DIM * 2
    q = norm_matmul(x2, norm_g, w_in, col0=0, ncols=hd, epilogue=_ep_headnorm(scale, True),
                    row_extras=(cos_t, sin_t), const_extras=(_row(q_norm),), out_dtype=F32, tn=tn,
                    pipelined=True, name="dsw_proj_q")
    k = norm_matmul(x2, norm_g, w_in, col0=hd, ncols=hd, epilogue=_ep_headnorm(1.0, True),
                    row_extras=(cos_t, sin_t), const_extras=(_row(k_norm),), out_dtype=F32, tn=tn,
                    pipelined=True, name="dsw_proj_k")
    v = norm_matmul(x2, norm_g, w_in, col0=2 * hd, ncols=hd, out_dtype=F32, tn=tn, name="dsw_proj_v")
    shp = (B, T, hd)
    q, k, v = q.reshape(shp), k.reshape(shp), v.reshape(shp)
    os_, lses = [], []
    for gi, (window, dilation) in enumerate(DSW_GROUPS):
        o_g, lse_g = band_attention(q, k, v, gi, window, dilation)
        os_.append(o_g.reshape(M, -1))
        lses.append(lse_g.reshape(M, -1))
    o = combine_groups(os_, lses)
    return matmul_residual(o, w_out.astype(BF16), x2, name="dsw_out")


def lru_mixer(x2, B, T, norm_g, w_in, conv_w, conv_b, w_a, b_a, w_x, b_x, lam, w_out):
    W = w_in.shape[1] // 2
    w_in = w_in.astype(BF16)
    gate = norm_matmul(x2, norm_g, w_in, col0=0, ncols=W, epilogue=_ep_gelu, out_dtype=BF16,
                       name="lru_proj_gate")
    xr = norm_matmul(x2, norm_g, w_in, col0=W, ncols=W, name="lru_proj_x")
    y = lru_scan(xr.reshape(B, T, W), gate.reshape(B, T, W), conv_w, _row(conv_b),
                 w_a.astype(BF16), w_x.astype(BF16), _row(b_a), _row(b_x), _row(lam))
    return matmul_residual(y.reshape(B * T, W), w_out.astype(BF16), x2, name="lru_out")


def kernel(x, positions, mix_norm, ffn_norm, ffn_w_gu, ffn_w_down, sb_w_in, sb_q_norm, sb_k_norm, sb_w_out, gdn_w_in, gdn_conv_w, gdn_a_log, gdn_dt_bias, gdn_o_norm, gdn_w_out, dsw_w_in, dsw_q_norm, dsw_k_norm, dsw_w_out, lru_w_in, lru_conv_w, lru_conv_b, lru_w_a, lru_b_a, lru_w_x, lru_b_x, lru_lambda, lru_w_out):
    B, T, D = x.shape
    depth = mix_norm.shape[0]
    x2 = x.reshape(B * T, D)
    for i in range(depth):
        kind, j = i % 4, i // 4
        if kind == 0:
            x2 = sb_mixer(x2, B, T, mix_norm[i], sb_w_in[j], sb_q_norm[j], sb_k_norm[j], sb_w_out[j])
        elif kind == 1:
            x2 = gdn_mixer(x2, B, T, mix_norm[i], gdn_w_in[j], gdn_conv_w[j], gdn_a_log[j],
                           gdn_dt_bias[j], gdn_o_norm[j], gdn_w_out[j])
        elif kind == 2:
            x2 = dsw_mixer(x2, B, T, mix_norm[i], positions, dsw_w_in[j], dsw_q_norm[j],
                           dsw_k_norm[j], dsw_w_out[j])
        else:
            x2 = lru_mixer(x2, B, T, mix_norm[i], lru_w_in[j], lru_conv_w[j], lru_conv_b[j],
                           lru_w_a[j], lru_b_a[j], lru_w_x[j], lru_b_x[j], lru_lambda[j], lru_w_out[j])
        x2 = ffn(x2, ffn_norm[i], ffn_w_gu[i].astype(BF16), ffn_w_down[i].astype(BF16))
    return x2.reshape(B, T, D)
```

```python
import functools
import math

import jax
import jax.numpy as jnp
from jax import lax
from jax.experimental import pallas as pl
from jax.experimental.pallas import tpu as pltpu

F32 = jnp.float32
BF16 = jnp.bfloat16

NORM_EPS = 1e-6
HEAD_DIM = 128
LANES = 128
SUBLANES = 8
VMEM_LIMIT_BYTES = 56 * 1024 * 1024

GDN_CONV = 4
GDN_CHUNK = 64
DSW_GROUPS = ((128, 1), (512, 4), (2048, 16))
DSW_HEADS_PER_GROUP = 6
DSW_BLOCK = 128
ROPE_DIM = HEAD_DIM // 4
ROPE_THETA = 500000.0
LRU_BLOCK_DIM = 256
LRU_C = 8.0
NEG_BIG = -1e30


def _pick(dim, pref, align):
    if dim <= pref:
        return dim
    t = (pref // align) * align
    while t >= align:
        if dim % t == 0:
            return t
        t -= align
    return dim


def _params(*sem):
    return pltpu.CompilerParams(dimension_semantics=sem, vmem_limit_bytes=VMEM_LIMIT_BYTES)


def _softplus(x):
    return jnp.maximum(x, 0.0) + jnp.log(1.0 + jnp.exp(-jnp.abs(x)))


def _sigmoid(x):
    return 1.0 / (1.0 + jnp.exp(-x))


def _silu(x):
    return x * _sigmoid(x)


def _rms(x, g):
    ms = jnp.mean(x * x, axis=-1, keepdims=True)
    return x * lax.rsqrt(ms + NORM_EPS) * g


def _norm_matmul_kernel(*refs, epilogue, n_extra):
    x_ref, g_ref, w_ref = refs[:3]
    extra = refs[3:3 + n_extra]
    o_ref, xn_ref = refs[3 + n_extra], refs[4 + n_extra]

    @pl.when(pl.program_id(1) == 0)
    def _():
        xn_ref[...] = _rms(x_ref[...], g_ref[...]).astype(BF16)

    y = jnp.dot(xn_ref[...], w_ref[...], preferred_element_type=F32)
    if epilogue is not None:
        y = epilogue(y, *[e[...] for e in extra])
    o_ref[...] = y.astype(o_ref.dtype)


def norm_matmul(x, g, w, *, col0=0, ncols=None, epilogue=None, row_extras=(), const_extras=(),
                out_dtype=F32, tm=1024, tn=1024, pipelined=False, name="norm_matmul"):
    M, K = x.shape
    ncols = w.shape[1] - col0 if ncols is None else ncols
    tm = _pick(M, tm, SUBLANES)
    tn = _pick(ncols, tn, LANES)
    assert col0 % tn == 0 and ncols % tn == 0 and M % tm == 0
    off = col0 // tn
    in_specs = [
        pl.BlockSpec((tm, K), lambda i, j: (i, 0)),
        pl.BlockSpec((1, K), lambda i, j: (0, 0)),
        pl.BlockSpec((K, tn), lambda i, j: (0, j + off)),
    ]
    args = [x, g.reshape(1, K), w]
    for e in row_extras:
        in_specs.append(pl.BlockSpec((tm, e.shape[1]), lambda i, j: (i, 0)))
        args.append(e)
    for e in const_extras:
        in_specs.append(pl.BlockSpec(e.shape, lambda i, j: (0, 0)))
        args.append(e)
    n_extra = len(row_extras) + len(const_extras)
    if pipelined:
        return _norm_matmul_pipelined(x, g, w, off, ncols, epilogue, row_extras, const_extras,
                                      out_dtype, tm, tn, name)
    return pl.pallas_call(
        functools.partial(_norm_matmul_kernel, epilogue=epilogue, n_extra=n_extra),
        grid=(M // tm, ncols // tn),
        in_specs=in_specs,
        out_specs=pl.BlockSpec((tm, tn), lambda i, j: (i, j)),
        out_shape=jax.ShapeDtypeStruct((M, ncols), out_dtype),
        scratch_shapes=[pltpu.VMEM((tm, K), BF16)],
        compiler_params=_params("parallel", "arbitrary"),
        name=name,
    )(*args)


def _norm_matmul_pipe_kernel(*refs, epilogue, n_extra, ncol, ntiles):
    x_ref, g_ref, w_ref = refs[:3]
    extra = refs[3:3 + n_extra]
    o_ref, xn_ref, y_ref = refs[3 + n_extra:6 + n_extra]
    t = pl.program_id(0)

    @pl.when(t == 0)
    def _():
        y_ref[1] = jnp.zeros(y_ref.shape[1:], F32)

    @pl.when((t % ncol == 0) & (t < ntiles))
    def _():
        xn_ref[...] = _rms(x_ref[...], g_ref[...]).astype(BF16)

    slot = t % 2
    o_ref[...] = epilogue(y_ref[1 - slot], *[e[...] for e in extra]).astype(o_ref.dtype)
    y_ref[slot] = jnp.dot(xn_ref[...], w_ref[...], preferred_element_type=F32)


def _norm_matmul_pipelined(x, g, w, off, ncols, epilogue, row_extras, const_extras, out_dtype,
                           tm, tn, name):
    M, K = x.shape
    ncol = ncols // tn
    ntiles = (M // tm) * ncol
    cur = lambda t: jnp.minimum(t, ntiles - 1)
    prv = lambda t: jnp.maximum(t - 1, 0)
    in_specs = [
        pl.BlockSpec((tm, K), lambda t: (cur(t) // ncol, 0)),
        pl.BlockSpec((1, K), lambda t: (0, 0)),
        pl.BlockSpec((K, tn), lambda t: (0, cur(t) % ncol + off)),
    ]
    args = [x, g.reshape(1, K), w]
    for e in row_extras:
        in_specs.append(pl.BlockSpec((tm, e.shape[1]), lambda t: (prv(t) // ncol, 0)))
        args.append(e)
    for e in const_extras:
        in_specs.append(pl.BlockSpec(e.shape, lambda t: (0, 0)))
        args.append(e)
    n_extra = len(row_extras) + len(const_extras)
    return pl.pallas_call(
        functools.partial(_norm_matmul_pipe_kernel, epilogue=epilogue, n_extra=n_extra,
                          ncol=ncol, ntiles=ntiles),
        grid=(ntiles + 1,),
        in_specs=in_specs,
        out_specs=pl.BlockSpec((tm, tn), lambda t: (prv(t) // ncol, prv(t) % ncol)),
        out_shape=jax.ShapeDtypeStruct((M, ncols), out_dtype),
        scratch_shapes=[pltpu.VMEM((tm, K), BF16), pltpu.VMEM((2, tm, tn), F32)],
        compiler_params=_params("arbitrary"),
        name=name,
    )(*args)


def _ep_headnorm(scale, rope):
    half = ROPE_DIM // 2

    def ep(y, *extra):
        yhs = [y[:, h * HEAD_DIM:(h + 1) * HEAD_DIM] for h in range(y.shape[1] // HEAD_DIM)]
        invs = [lax.rsqrt(jnp.mean(yh * yh, axis=-1, keepdims=True) + NORM_EPS) * scale for yh in yhs]
        if rope:
            cos_t, sin_t, gain = extra
            lane = lax.broadcasted_iota(jnp.int32, cos_t.shape, 1)
            gain_b = jnp.broadcast_to(gain, cos_t.shape)
            cg = cos_t * gain
            s_lo = jnp.where(lane < half, pltpu.roll(gain_b, HEAD_DIM - half, 1) * sin_t, 0.0)
            s_hi = jnp.where(lane >= half, pltpu.roll(gain_b, half, 1) * sin_t, 0.0)
            lo = [pltpu.roll(yh, HEAD_DIM - half, 1) for yh in yhs]
            hi = [pltpu.roll(yh, half, 1) for yh in yhs]
            outs = [(yh * cg + a * s_lo + b * s_hi) * inv for yh, a, b, inv in zip(yhs, lo, hi, invs)]
        else:
            (gain,) = extra
            outs = [yh * inv * gain for yh, inv in zip(yhs, invs)]
        return outs[0] if len(outs) == 1 else jnp.concatenate(outs, axis=1)

    return ep


def _ep_gelu(y):
    c = math.sqrt(2.0 / math.pi)
    return 0.5 * y * (1.0 + jnp.tanh(c * (y + 0.044715 * (y * y * y))))


def _ep_gdn_gates(n_heads):
    def ep(y, a_log, dt_bias):
        lane = lax.broadcasted_iota(jnp.int32, y.shape, 1)
        beta = _sigmoid(y)
        g = -jnp.exp(a_log) * _softplus(y + dt_bias)
        return jnp.where(lane < n_heads, beta, g)

    return ep


def _matmul_res_kernel(a_ref, w_ref, r_ref, o_ref):
    y = jnp.dot(a_ref[...].astype(BF16), w_ref[...], preferred_element_type=F32)
    o_ref[...] = r_ref[...] + y


def matmul_residual(a, w, res, *, tm=1024, name="out_proj"):
    M, K = a.shape
    N = w.shape[1]
    tm = _pick(M, tm, SUBLANES)
    tn = _pick(N, 1024 if K <= 2560 else 512, LANES)
    return pl.pallas_call(
        _matmul_res_kernel,
        grid=(M // tm, N // tn),
        in_specs=[
            pl.BlockSpec((tm, K), lambda i, j: (i, 0)),
            pl.BlockSpec((K, tn), lambda i, j: (0, j)),
            pl.BlockSpec((tm, tn), lambda i, j: (i, j)),
        ],
        out_specs=pl.BlockSpec((tm, tn), lambda i, j: (i, j)),
        out_shape=jax.ShapeDtypeStruct((M, N), F32),
        compiler_params=_params("parallel", "arbitrary"),
        name=name,
    )(a, w, res)


def _ffn_kernel(x_ref, g_ref, wg_ref, wu_ref, wd_ref, o_ref, xn_ref, *, sub):
    @pl.when(pl.program_id(1) == 0)
    def _():
        x = x_ref[...]
        xn_ref[...] = _rms(x, g_ref[...]).astype(BF16)
        o_ref[...] = x

    xn = xn_ref[...]
    tf = wg_ref.shape[1]
    hs = []
    for c in range(tf // sub):
        cs = slice(c * sub, (c + 1) * sub)
        gate = jnp.dot(xn, wg_ref[:, cs], preferred_element_type=F32)
        up = jnp.dot(xn, wu_ref[:, cs], preferred_element_type=F32)
        hs.append((_silu(gate) * up).astype(BF16))
    h = hs[0] if len(hs) == 1 else jnp.concatenate(hs, axis=1)
    o_ref[...] += jnp.dot(h, wd_ref[...], preferred_element_type=F32)


def ffn(x, g, w_gu, w_down, *, tm=512, tf=512, sub=256):
    M, D = x.shape
    Fh = w_down.shape[0]
    tm = _pick(M, tm, SUBLANES)
    tf = _pick(Fh, tf, LANES)
    nf = Fh // tf
    return pl.pallas_call(
        functools.partial(_ffn_kernel, sub=min(sub, tf)),
        grid=(M // tm, nf),
        in_specs=[
            pl.BlockSpec((tm, D), lambda i, j: (i, 0)),
            pl.BlockSpec((1, D), lambda i, j: (0, 0)),
            pl.BlockSpec((D, tf), lambda i, j: (0, j)),
            pl.BlockSpec((D, tf), lambda i, j: (0, j + nf)),
            pl.BlockSpec((tf, D), lambda i, j: (j, 0)),
        ],
        out_specs=pl.BlockSpec((tm, D), lambda i, j: (i, 0)),
        out_shape=jax.ShapeDtypeStruct((M, D), F32),
        scratch_shapes=[pltpu.VMEM((tm, D), BF16)],
        compiler_params=_params("parallel", "arbitrary"),
        name="ffn",
    )(x, g.reshape(1, D), w_gu, w_gu, w_down)


SB_STOP = 88.0


def _sb_kernel(q_ref, k_ref, v_ref, o_ref, *, blk, heads, n_q):
    W = 2 * blk
    row = lax.broadcasted_iota(jnp.int32, (blk, W), 0)
    col = lax.broadcasted_iota(jnp.int32, (blk, W), 1)
    r2 = lax.broadcasted_iota(jnp.int32, (W, W), 0)
    c2 = lax.broadcasted_iota(jnp.int32, (W, W), 1)
    suffix_w = jnp.where(r2 >= c2, 1.0, 0.0).astype(BF16)
    rb = lax.broadcasted_iota(jnp.int32, (blk, blk), 0)
    cb = lax.broadcasted_iota(jnp.int32, (blk, blk), 1)
    suffix_b = jnp.where(rb >= cb, 1.0, 0.0).astype(BF16)

    def dot_nt(a, b):
        return lax.dot_general(a, b, (((1,), (1,)), ((), ())), preferred_element_type=F32)

    def suffix_sum(sp, suffix):
        hi = sp.astype(BF16)
        lo = (sp - hi.astype(F32)).astype(BF16)
        return (jnp.dot(hi, suffix, preferred_element_type=F32)
                + jnp.dot(lo, suffix, preferred_element_type=F32))

    def qblock(i, carry):
        q0 = pl.multiple_of(i * blk, blk)
        w0 = pl.multiple_of(jnp.maximum(i - 1, 0) * blk, blk)
        valid = col < row + (q0 - w0)
        hsl = [slice(h * HEAD_DIM, (h + 1) * HEAD_DIM) for h in range(heads)]
        qs = [q_ref[pl.ds(q0, blk), sl] for sl in hsl]
        zs = [dot_nt(q, k_ref[pl.ds(w0, W), sl]) for q, sl in zip(qs, hsl)]
        sps = [jnp.where(valid, _softplus(z), 0.0) for z in zs]
        cums = [suffix_sum(sp, suffix_w) for sp in sps]
        ws = [jnp.exp(jnp.where(valid, z - cum, NEG_BIG)).astype(BF16) for z, cum in zip(zs, cums)]
        accs = [jnp.dot(w, v_ref[pl.ds(w0, W), sl], preferred_element_type=F32)
                for w, sl in zip(ws, hsl)]
        runs = [cum[:, 0:1] for cum in cums]
        def cond(c):
            least = functools.reduce(jnp.minimum, c[2])
            return (c[0] >= 0) & (jnp.min(least) < SB_STOP)

        def body(c):
            j, acc, run = c
            s0 = pl.multiple_of(j * blk, blk)
            z = [dot_nt(q, k_ref[pl.ds(s0, blk), sl]) for q, sl in zip(qs, hsl)]
            cum = [suffix_sum(_softplus(zh), suffix_b) for zh in z]
            w = [jnp.exp(zh - ch - rh).astype(BF16) for zh, ch, rh in zip(z, cum, run)]
            acc = tuple(a + jnp.dot(wh, v_ref[pl.ds(s0, blk), sl], preferred_element_type=F32)
                        for a, wh, sl in zip(acc, w, hsl))
            return j - 1, acc, tuple(rh + ch[:, 0:1] for rh, ch in zip(run, cum))

        _, accs, _ = lax.while_loop(cond, body, (i - 2, tuple(accs), tuple(runs)))
        for acc, sl in zip(accs, hsl):
            o_ref[pl.ds(q0, blk), sl] = acc.astype(o_ref.dtype)
        return carry

    lax.fori_loop(0, n_q, qblock, 0)


def sb_attention(q, k, v, *, blk=128, heads=4):
    B, T, HD = q.shape
    H = HD // HEAD_DIM
    assert T % blk == 0 and T >= 2 * blk and H % heads == 0
    seq = pl.BlockSpec((None, T, heads * HEAD_DIM), lambda b, h: (b, 0, h))
    return pl.pallas_call(
        functools.partial(_sb_kernel, blk=blk, heads=heads, n_q=T // blk),
        grid=(B, H // heads),
        in_specs=[seq, seq, seq],
        out_specs=seq,
        out_shape=jax.ShapeDtypeStruct((B, T, HD), BF16),
        compiler_params=_params("parallel", "parallel"),
        name="sb_attention",
    )(q, k, v)


def _bdot(a, b):
    return jnp.dot(a.astype(BF16), b.astype(BF16), preferred_element_type=F32)


def _bdot_nt(a, b):
    return lax.dot_general(a.astype(BF16), b.astype(BF16), (((1,), (1,)), ((), ())),
                           preferred_element_type=F32)


def _bdot_tn(a, b):
    return lax.dot_general(a.astype(BF16), b.astype(BF16), (((0,), (0,)), ((), ())),
                           preferred_element_type=F32)


def _gdn_kernel(xq_ref, xk_ref, xv_ref, z_ref, cwq_ref, cwk_ref, cwv_ref, g_ref, beta_ref, gain_ref,
                o_ref, q_ref, k_ref, v_ref, hq_ref, hk_ref, hv_ref, buf_ref,
                u_ref, wq_ref, kd_ref, attn_ref, gl_ref, out_ref, state_ref,
                *, chunk, kheads, rep, group, taps):
    C = chunk

    @pl.when(pl.program_id(2) == 0)
    def _():
        state_ref[...] = jnp.zeros_like(state_ref)
        hq_ref[...] = jnp.zeros_like(hq_ref)
        hk_ref[...] = jnp.zeros_like(hk_ref)
        hv_ref[...] = jnp.zeros_like(hv_ref)

    def conv_silu(x_ref, halo_ref, w_ref, dst_ref, l2_scale):
        n, width = x_ref.shape
        buf_ref[0:SUBLANES, 0:width] = halo_ref[...]
        buf_ref[SUBLANES:SUBLANES + n, 0:width] = x_ref[...]
        halo_ref[...] = x_ref[n - SUBLANES:n, :]
        w = w_ref[...]
        y = w[taps - 1:taps, :] * x_ref[...]
        for kk in range(taps - 1):
            shift = taps - 1 - kk
            y = y + w[kk:kk + 1, :] * buf_ref[SUBLANES - shift:SUBLANES - shift + n, 0:width]
        y = _silu(y)
        for h in range(width // HEAD_DIM):
            sl = slice(h * HEAD_DIM, (h + 1) * HEAD_DIM)
            yh = y[:, sl]
            if l2_scale is not None:
                ss = jnp.sum(yh * yh, axis=-1, keepdims=True)
                yh = yh * (lax.rsqrt(ss + NORM_EPS) * l2_scale)
            dst_ref[:, sl] = yh

    conv_silu(xq_ref, hq_ref, cwq_ref, q_ref, HEAD_DIM ** -0.5)
    conv_silu(xk_ref, hk_ref, cwk_ref, k_ref, 1.0)
    conv_silu(xv_ref, hv_ref, cwv_ref, v_ref, None)

    ri = lax.broadcasted_iota(jnp.int32, (C, C), 0)
    ci = lax.broadcasted_iota(jnp.int32, (C, C), 1)
    eye = ri == ci
    lower = ci <= ri
    strict = ci < ri
    gain = gain_ref[...]
    n_double = max(1, int(math.ceil(math.log2(C))) - 1)
    nv = kheads * rep
    heads = [slice(r * HEAD_DIM, (r + 1) * HEAD_DIM) for r in range(nv)]
    G = group
    rows = G * C
    bmm = functools.partial(jnp.einsum, "gij,gjk->gik", preferred_element_type=F32)
    bmm_nt = functools.partial(jnp.einsum, "gid,gjd->gij", preferred_element_type=F32)

    def prepare(khs):
        rs = [r for kh in khs for r in range(kh * rep, (kh + 1) * rep)]
        every = lambda f: [f(i) for i in range(len(rs))]
        qs = {kh: q_ref[:, heads[kh]].reshape(G, C, HEAD_DIM) for kh in khs}
        ks = {kh: k_ref[:, heads[kh]].reshape(G, C, HEAD_DIM) for kh in khs}
        k16s = {kh: ks[kh].astype(BF16) for kh in khs}
        qks = {kh: bmm_nt(qs[kh].astype(BF16), k16s[kh]) for kh in khs}
        q = every(lambda i: qs[rs[i] // rep])
        k = every(lambda i: ks[rs[i] // rep])
        k16 = every(lambda i: k16s[rs[i] // rep])
        g_row = every(lambda i: g_ref[rs[i]])
        beta_row = every(lambda i: beta_ref[rs[i]])
        g_col = every(lambda i: jnp.sum(jnp.where(eye, g_row[i], 0.0), axis=2, keepdims=True))
        beta_col = every(lambda i: jnp.sum(jnp.where(eye, beta_row[i], 0.0), axis=2, keepdims=True))
        gc_col = every(lambda i: jnp.sum(jnp.where(lower, g_row[i], 0.0), axis=2, keepdims=True))
        gc_row = every(lambda i: jnp.sum(jnp.where(ri <= ci, g_col[i], 0.0), axis=1, keepdims=True))
        g_last = every(lambda i: jnp.sum(g_row[i], axis=2, keepdims=True))
        decay = every(lambda i: jnp.where(
            lower, jnp.exp(jnp.where(lower, gc_col[i] - gc_row[i], 0.0)), 0.0))
        eg = every(lambda i: jnp.exp(gc_col[i]))
        kb = every(lambda i: k[i] * beta_col[i])
        a_mat = every(lambda i: jnp.where(strict, bmm_nt(kb[i].astype(BF16), k16[i]) * decay[i], 0.0))
        t_mat = every(lambda i: jnp.where(eye, 1.0, 0.0) - a_mat[i])
        p = a_mat
        for _ in range(n_double):
            p16 = every(lambda i: p[i].astype(BF16))
            p = every(lambda i: bmm(p16[i], p16[i]))
            t_mat = every(lambda i: t_mat[i] + bmm(t_mat[i].astype(BF16), p[i].astype(BF16)))
        rhs = every(lambda i: jnp.concatenate(
            [v_ref[:, heads[rs[i]]].reshape(G, C, HEAD_DIM) * beta_col[i], kb[i] * eg[i]],
            axis=2).astype(BF16))
        uw = every(lambda i: bmm(t_mat[i].astype(BF16), rhs[i]))
        for i, r in enumerate(rs):
            u_ref[r] = uw[i][:, :, :HEAD_DIM].reshape(rows, HEAD_DIM)
            wq_ref[r] = jnp.concatenate([uw[i][:, :, HEAD_DIM:], q[i] * eg[i]], axis=1).astype(
                BF16).reshape(2 * rows, HEAD_DIM)
            kd_ref[r] = (k[i] * jnp.exp(g_last[i] - gc_col[i])).astype(BF16).reshape(rows, HEAD_DIM)
            attn_ref[r] = (qks[r // rep] * decay[i]).astype(BF16).reshape(rows, C)
            gl_ref[r] = jnp.broadcast_to(g_last[i], (G, 1, HEAD_DIM))

    def recur(nl, states):
        ls = pl.multiple_of(nl * C, C)
        ls2 = pl.multiple_of(nl * 2 * C, 2 * C)
        hs = range(nv)
        s16 = [states[r].astype(BF16) for r in hs]
        ws = [jnp.dot(wq_ref[r, pl.ds(ls2, 2 * C), :], s16[r], preferred_element_type=F32) for r in hs]
        v_new = [(u_ref[r, pl.ds(ls, C), :] - ws[r][:C]).astype(BF16) for r in hs]
        upd = [lax.dot_general(kd_ref[r, pl.ds(ls, C), :], v_new[r], (((0,), (0,)), ((), ())),
                               preferred_element_type=F32) for r in hs]
        new = tuple(states[r] * jnp.exp(gl_ref[r, nl]) + upd[r] for r in hs)
        for r in hs:
            out_ref[r, pl.ds(ls, C), :] = ws[r][C:] + jnp.dot(
                attn_ref[r, pl.ds(ls, C), :], v_new[r], preferred_element_type=F32)
        return new

    for k0 in range(0, kheads, 2):
        prepare(list(range(k0, min(k0 + 2, kheads))))
    states = lax.fori_loop(0, group, recur, tuple(state_ref[r] for r in range(nv)))
    for r in range(nv):
        state_ref[r] = states[r]
        o_ref[:, heads[r]] = (_rms(out_ref[r], gain) * _silu(z_ref[:, heads[r]])).astype(o_ref.dtype)


def gdn_delta_rule(proj, conv_w, KD, VD, g, beta, gain, *, chunk=GDN_CHUNK, group=8, kheads=4):
    B, T, _ = proj.shape
    taps = conv_w.shape[0]
    Hk, Hv = KD // HEAD_DIM, VD // HEAD_DIM
    rep = Hv // Hk
    N = T // chunk
    group = _pick(N, group, 1)
    kheads = _pick(Hk, kheads, 1)
    nv = kheads * rep
    kw, vw = kheads * HEAD_DIM, nv * HEAD_DIM
    assert KD % kw == 0 and (2 * KD) % vw == 0 and N % group == 0
    koff, voff, zoff = KD // kw, 2 * KD // vw, (2 * KD + VD) // vw
    rows = group * chunk
    assert rows % SUBLANES == 0
    tile = lambda width, f: pl.BlockSpec((None, rows, width), f)
    tapw = lambda width, f: pl.BlockSpec((taps, width), f)
    gate = pl.BlockSpec((None, nv, group, 1, chunk), lambda b, h, t: (b, h, t, 0, 0))
    return pl.pallas_call(
        functools.partial(_gdn_kernel, chunk=chunk, kheads=kheads, rep=rep, group=group, taps=taps),
        grid=(B, Hk // kheads, N // group),
        in_specs=[
            tile(kw, lambda b, h, t: (b, t, h)),
            tile(kw, lambda b, h, t: (b, t, h + koff)),
            tile(vw, lambda b, h, t: (b, t, h + voff)),
            tile(vw, lambda b, h, t: (b, t, h + zoff)),
            tapw(kw, lambda b, h, t: (0, h)),
            tapw(kw, lambda b, h, t: (0, h + koff)),
            tapw(vw, lambda b, h, t: (0, h + voff)),
            gate,
            gate,
            pl.BlockSpec((1, HEAD_DIM), lambda b, h, t: (0, 0)),
        ],
        out_specs=tile(vw, lambda b, h, t: (b, t, h)),
        out_shape=jax.ShapeDtypeStruct((B, T, VD), BF16),
        scratch_shapes=[
            pltpu.VMEM((rows, kw), F32),
            pltpu.VMEM((rows, kw), F32),
            pltpu.VMEM((rows, vw), F32),
            pltpu.VMEM((SUBLANES, kw), F32),
            pltpu.VMEM((SUBLANES, kw), F32),
            pltpu.VMEM((SUBLANES, vw), F32),
            pltpu.VMEM((rows + SUBLANES, vw), F32),
            pltpu.VMEM((nv, rows, HEAD_DIM), F32),
            pltpu.VMEM((nv, 2 * rows, HEAD_DIM), BF16),
            pltpu.VMEM((nv, rows, HEAD_DIM), BF16),
            pltpu.VMEM((nv, rows, chunk), BF16),
            pltpu.VMEM((nv, group, 1, HEAD_DIM), F32),
            pltpu.VMEM((nv, rows, HEAD_DIM), F32),
            pltpu.VMEM((nv, HEAD_DIM, HEAD_DIM), F32),
        ],
        compiler_params=_params("parallel", "parallel", "arbitrary"),
        name="gdn_delta_rule",
    )(proj, proj, proj, proj, conv_w, conv_w, conv_w, g, beta, gain)


def _rope_table_kernel(pos_ref, freq_ref, sign_ref, cos_ref, sin_ref):
    ang = pos_ref[...].astype(F32) * freq_ref[...]
    cos_ref[...] = jnp.cos(ang)
    sin_ref[...] = jnp.sin(ang) * sign_ref[...]


def rope_tables(pos, *, tm=1024):
    M = pos.shape[0]
    tm = _pick(M, tm, SUBLANES)
    half = ROPE_DIM // 2
    lane = jnp.arange(HEAD_DIM)
    inv_freq = ROPE_THETA ** (-(lane % half).astype(F32) / half)
    freq = jnp.where(lane < ROPE_DIM, inv_freq, 0.0).reshape(1, HEAD_DIM).astype(F32)
    sign = jnp.where(lane < half, -1.0, 1.0).reshape(1, HEAD_DIM).astype(F32)
    spec = pl.BlockSpec((tm, HEAD_DIM), lambda i: (i, 0))
    const = pl.BlockSpec((1, HEAD_DIM), lambda i: (0, 0))
    return pl.pallas_call(
        _rope_table_kernel,
        grid=(M // tm,),
        in_specs=[pl.BlockSpec((tm, 1), lambda i: (i, 0)), const, const],
        out_specs=[spec, spec],
        out_shape=[jax.ShapeDtypeStruct((M, HEAD_DIM), F32)] * 2,
        compiler_params=_params("parallel"),
        name="rope_tables",
    )(pos, freq, sign)


def _band_kernel(q_ref, kc_ref, kp_ref, vc_ref, vp_ref, o_ref, lse_ref, *, blk, span, heads, d, batch):
    n = pl.program_id(2)
    qi = lax.broadcasted_iota(jnp.int32, (blk, 2 * blk), 0)
    kj = lax.broadcasted_iota(jnp.int32, (blk, 2 * blk), 1)
    steps = blk + qi - kj
    first_key = jnp.where(n > 0, 0, blk)
    valid = (steps >= 0) & (steps <= span) & (kj >= first_key)
    streams = [(r, h) for r in range(d) for h in range(heads)]

    def rows(ref, r, h):
        sl = slice(h * HEAD_DIM, (h + 1) * HEAD_DIM)
        return ref[:, sl] if d == 1 else ref[pl.ds(r, blk, stride=d), sl]

    for b0 in range(0, len(streams), batch):
        part = streams[b0:b0 + batch]
        s_list = []
        for r, h in part:
            q = rows(q_ref, r, h).astype(BF16)
            kw = jnp.concatenate([rows(kp_ref, r, h), rows(kc_ref, r, h)], axis=0).astype(BF16)
            s = lax.dot_general(q, kw, (((1,), (1,)), ((), ())), preferred_element_type=F32)
            s_list.append(jnp.where(valid, s, NEG_BIG))
        m_list = [jnp.max(s, axis=-1, keepdims=True) for s in s_list]
        p_list = [jnp.where(valid, jnp.exp(s - m), 0.0) for s, m in zip(s_list, m_list)]
        den_list = [jnp.sum(p, axis=-1, keepdims=True) for p in p_list]
        for (r, h), p, m, den in zip(part, p_list, m_list, den_list):
            vw = jnp.concatenate([rows(vp_ref, r, h), rows(vc_ref, r, h)], axis=0).astype(BF16)
            o = jnp.dot((p / den).astype(BF16), vw, preferred_element_type=F32)
            lse = jnp.broadcast_to(m + jnp.log(den), (blk, HEAD_DIM))
            sl = slice(h * HEAD_DIM, (h + 1) * HEAD_DIM)
            if d == 1:
                o_ref[:, sl] = o
                lse_ref[:, sl] = lse
            else:
                o_ref[pl.ds(r, blk, stride=d), sl] = o
                lse_ref[pl.ds(r, blk, stride=d), sl] = lse


def band_attention(q, k, v, group, window, dilation, *, blk=DSW_BLOCK, group_heads=DSW_HEADS_PER_GROUP):
    B, T, HD = q.shape
    d = dilation
    span = window // d
    unit = d * blk
    assert span <= blk and T % unit == 0
    heads = group_heads if d == 1 else 1
    assert group_heads % heads == 0
    hw = heads * HEAD_DIM
    col0 = group * group_heads // heads
    cur = pl.BlockSpec((None, unit, hw), lambda b, c, n: (b, n, col0 + c))
    prev = pl.BlockSpec((None, unit, hw), lambda b, c, n: (b, jnp.maximum(n - 1, 0), col0 + c))
    out = pl.BlockSpec((None, unit, hw), lambda b, c, n: (b, n, c))
    return pl.pallas_call(
        functools.partial(_band_kernel, blk=blk, span=span, heads=heads, d=d, batch=6 if d == 1 else 4),
        grid=(B, group_heads // heads, T // unit),
        in_specs=[cur, cur, prev, cur, prev],
        out_specs=[out, out],
        out_shape=[jax.ShapeDtypeStruct((B, T, group_heads * HEAD_DIM), F32)] * 2,
        compiler_params=_params("parallel", "parallel", "arbitrary"),
        name=f"band_attention_d{d}",
    )(q, k, k, v, v)


def _combine_kernel(*refs, n):
    o_refs, l_refs, out_ref = refs[:n], refs[n:2 * n], refs[2 * n]
    ls = [r[...] for r in l_refs]
    m = functools.reduce(jnp.maximum, ls)
    es = [jnp.exp(l - m) for l in ls]
    inv = 1.0 / functools.reduce(lambda a, b: a + b, es)
    w = o_refs[0].shape[1]
    for gi in range(n):
        out_ref[:, gi * w:(gi + 1) * w] = (o_refs[gi][...] * (es[gi] * inv)).astype(out_ref.dtype)


def combine_groups(os_, lses, *, tm=512):
    n = len(os_)
    M, w = os_[0].shape
    tm = _pick(M, tm, SUBLANES)
    spec = pl.BlockSpec((tm, w), lambda i: (i, 0))
    return pl.pallas_call(
        functools.partial(_combine_kernel, n=n),
        grid=(M // tm,),
        in_specs=[spec] * (2 * n),
        out_specs=pl.BlockSpec((tm, n * w), lambda i: (i, 0)),
        out_shape=jax.ShapeDtypeStruct((M, n * w), BF16),
        compiler_params=_params("parallel"),
        name="combine_groups",
    )(*os_, *lses)


def _lru_kernel(x_ref, gate_ref, cw_ref, cb_ref, wa_ref, wx_ref, ba_ref, bx_ref, lam_ref, o_ref,
                a_ref, u_ref, h_ref, xc_ref, halo_ref, buf_ref, *, tt, nblk, taps):
    @pl.when(pl.program_id(1) == 0)
    def _():
        h_ref[...] = jnp.zeros_like(h_ref)
        halo_ref[...] = jnp.zeros_like(halo_ref)

    buf_ref[0:SUBLANES, :] = halo_ref[...]
    buf_ref[SUBLANES:SUBLANES + tt, :] = x_ref[...]
    halo_ref[...] = x_ref[tt - SUBLANES:tt, :]
    cw = cw_ref[...]
    xc = cb_ref[...] + cw[taps - 1:taps, :] * x_ref[...]
    for kk in range(taps - 1):
        shift = taps - 1 - kk
        xc = xc + cw[kk:kk + 1, :] * buf_ref[SUBLANES - shift:SUBLANES - shift + tt, :]
    xc_ref[...] = xc

    for nb in range(nblk):
        sl = slice(nb * LRU_BLOCK_DIM, (nb + 1) * LRU_BLOCK_DIM)
        xb = xc_ref[:, sl]
        xb16 = xb.astype(BF16)
        r = _sigmoid(jnp.dot(xb16, wa_ref[nb], preferred_element_type=F32) + ba_ref[:, sl])
        ig = _sigmoid(jnp.dot(xb16, wx_ref[nb], preferred_element_type=F32) + bx_ref[:, sl])
        log_a = (-LRU_C) * r * _softplus(-lam_ref[:, sl])
        a = jnp.exp(log_a)
        a_ref[:, sl] = a
        u_ref[:, sl] = jnp.sqrt(1.0 - a * a) * (ig * xb)

    def group(gi, h):
        base = pl.multiple_of(gi * SUBLANES, SUBLANES)
        for r8 in range(SUBLANES):
            h = a_ref[pl.ds(base + r8, 1), :] * h + u_ref[pl.ds(base + r8, 1), :]
            u_ref[pl.ds(base + r8, 1), :] = h
        return h

    h_ref[...] = lax.fori_loop(0, tt // SUBLANES, group, h_ref[...])
    o_ref[...] = (u_ref[...] * gate_ref[...].astype(F32)).astype(o_ref.dtype)


def lru_scan(xr, gate, conv_w, conv_b, w_a, w_x, b_a, b_x, lam, *, tt=256):
    B, T, W = xr.shape
    nblk = W // LRU_BLOCK_DIM
    taps = conv_w.shape[0]
    tt = _pick(T, tt, SUBLANES)
    tile = pl.BlockSpec((None, tt, W), lambda b, i: (b, i, 0))
    wspec = pl.BlockSpec((nblk, LRU_BLOCK_DIM, LRU_BLOCK_DIM), lambda b, i: (0, 0, 0))
    vec = pl.BlockSpec((1, W), lambda b, i: (0, 0))
    return pl.pallas_call(
        functools.partial(_lru_kernel, tt=tt, nblk=nblk, taps=taps),
        grid=(B, T // tt),
        in_specs=[tile, tile, pl.BlockSpec((taps, W), lambda b, i: (0, 0)), vec,
                  wspec, wspec, vec, vec, vec],
        out_specs=tile,
        out_shape=jax.ShapeDtypeStruct((B, T, W), BF16),
        scratch_shapes=[
            pltpu.VMEM((tt, W), F32),
            pltpu.VMEM((tt, W), F32),
            pltpu.VMEM((1, W), F32),
            pltpu.VMEM((tt, W), F32),
            pltpu.VMEM((SUBLANES, W), F32),
            pltpu.VMEM((tt + SUBLANES, W), F32),
        ],
        compiler_params=_params("parallel", "arbitrary"),
        name="lru_scan",
    )(xr, gate, conv_w, conv_b, w_a, w_x, b_a, b_x, lam)


def _row(v):
    return v.reshape(1, -1).astype(F32)


def sb_mixer(x2, B, T, norm_g, w_in, q_norm, k_norm, w_out):
    hd = w_in.shape[1] // 3
    w_in = w_in.astype(BF16)
    scale = HEAD_DIM ** -0.5
    q = norm_matmul(x2, norm_g, w_in, col0=0, ncols=hd, epilogue=_ep_headnorm(scale, False),
                    const_extras=(_row(q_norm),), out_dtype=BF16, name="sb_proj_q")
    k = norm_matmul(x2, norm_g, w_in, col0=hd, ncols=hd, epilogue=_ep_headnorm(1.0, False),
                    const_extras=(_row(k_norm),), out_dtype=BF16, name="sb_proj_k")
    v = norm_matmul(x2, norm_g, w_in, col0=2 * hd, ncols=hd, out_dtype=BF16, name="sb_proj_v")
    shp = (B, T, hd)
    o = sb_attention(q.reshape(shp), k.reshape(shp), v.reshape(shp))
    return matmul_residual(o.reshape(B * T, hd), w_out.astype(BF16), x2, name="sb_out")


def gdn_mixer(x2, B, T, norm_g, w_in, conv_w, a_log, dt_bias, o_norm, w_out):
    M = B * T
    Hv = a_log.shape[0]
    vd = Hv * HEAD_DIM
    kd = (conv_w.shape[1] - vd) // 2
    main = 2 * kd + 2 * vd
    proj = norm_matmul(x2, norm_g, w_in.astype(BF16), col0=0, ncols=main, name="gdn_proj")
    w_ba = jnp.pad(w_in[:, main:], ((0, 0), (0, LANES - 2 * Hv))).astype(BF16)
    pad = lambda p: jnp.pad(p.astype(F32), (Hv, LANES - 2 * Hv)).reshape(1, LANES)
    gates = norm_matmul(x2, norm_g, w_ba, epilogue=_ep_gdn_gates(Hv),
                        const_extras=(pad(a_log), pad(dt_bias)), name="gdn_gates")
    N = T // GDN_CHUNK
    per_head = lambda a: a.reshape(B, T, Hv).transpose(0, 2, 1).reshape(B, Hv, N, 1, GDN_CHUNK)
    beta = per_head(gates[:, :Hv])
    g = per_head(gates[:, Hv:2 * Hv])
    o = gdn_delta_rule(proj.reshape(B, T, main), conv_w, kd, vd, g, beta, _row(o_norm))
    return matmul_residual(o.reshape(M, vd), w_out.astype(BF16), x2, name="gdn_out")


def dsw_mixer(x2, B, T, norm_g, positions, w_in, q_norm, k_norm, w_out):
    M = B * T
    hd = w_in.shape[1] // 3
    w_in = w_in.astype(BF16)
    cos_t, sin_t = rope_tables(positions.reshape(M, 1).astype(jnp.int32))
    scale = HEAD_DIM ** -0.5
    tn = 3 * HEAD_DIM * 2
    q = norm_matmul(x2, norm_g, w_in, col0=0, ncols=hd, epilogue=_ep_headnorm(scale, True),
                    row_extras=(cos_t, sin_t), const_extras=(_row(q_norm),), out_dtype=F32, tn=tn,
                    pipelined=True, name="dsw_proj_q")
    k = norm_matmul(x2, norm_g, w_in, col0=hd, ncols=hd, epilogue=_ep_headnorm(1.0, True),
                    row_extras=(cos_t, sin_t), const_extras=(_row(k_norm),), out_dtype=F32, tn=tn,
                    pipelined=True, name="dsw_proj_k")
    v = norm_matmul(x2, norm_g, w_in, col0=2 * hd, ncols=hd, out_dtype=F32, tn=tn, name="dsw_proj_v")
    shp = (B, T, hd)
    q, k, v = q.reshape(shp), k.reshape(shp), v.reshape(shp)
    os_, lses = [], []
    for gi, (window, dilation) in enumerate(DSW_GROUPS):
        o_g, lse_g = band_attention(q, k, v, gi, window, dilation)
        os_.append(o_g.reshape(M, -1))
        lses.append(lse_g.reshape(M, -1))
    o = combine_groups(os_, lses)
    return matmul_residual(o, w_out.astype(BF16), x2, name="dsw_out")


def lru_mixer(x2, B, T, norm_g, w_in, conv_w, conv_b, w_a, b_a, w_x, b_x, lam, w_out):
    W = w_in.shape[1] // 2
    w_in = w_in.astype(BF16)
    gate = norm_matmul(x2, norm_g, w_in, col0=0, ncols=W, epilogue=_ep_gelu, out_dtype=BF16,
                       name="lru_proj_gate")
    xr = norm_matmul(x2, norm_g, w_in, col0=W, ncols=W, name="lru_proj_x")
    y = lru_scan(xr.reshape(B, T, W), gate.reshape(B, T, W), conv_w, _row(conv_b),
                 w_a.astype(BF16), w_x.astype(BF16), _row(b_a), _row(b_x), _row(lam))
    return matmul_residual(y.reshape(B * T, W), w_out.astype(BF16), x2, name="lru_out")


def kernel(x, positions, mix_norm, ffn_norm, ffn_w_gu, ffn_w_down, sb_w_in, sb_q_norm, sb_k_norm, sb_w_out, gdn_w_in, gdn_conv_w, gdn_a_log, gdn_dt_bias, gdn_o_norm, gdn_w_out, dsw_w_in, dsw_q_norm, dsw_k_norm, dsw_w_out, lru_w_in, lru_conv_w, lru_conv_b, lru_w_a, lru_b_a, lru_w_x, lru_b_x, lru_lambda, lru_w_out):
    B, T, D = x.shape
    depth = mix_norm.shape[0]
    x2 = x.reshape(B * T, D)
    for i in range(depth):
        kind, j = i % 4, i // 4
        if kind == 0:
            x2 = sb_mixer(x2, B, T, mix_norm[i], sb_w_in[j], sb_q_norm[j], sb_k_norm[j], sb_w_out[j])
        elif kind == 1:
            x2 = gdn_mixer(x2, B, T, mix_norm[i], gdn_w_in[j], gdn_conv_w[j], gdn_a_log[j],
                           gdn_dt_bias[j], gdn_o_norm[j], gdn_w_out[j])
        elif kind == 2:
            x2 = dsw_mixer(x2, B, T, mix_norm[i], positions, dsw_w_in[j], dsw_q_norm[j],
                           dsw_k_norm[j], dsw_w_out[j])
        else:
            x2 = lru_mixer(x2, B, T, mix_norm[i], lru_w_in[j], lru_conv_w[j], lru_conv_b[j],
                           lru_w_a[j], lru_b_a[j], lru_w_x[j], lru_b_x[j], lru_lambda[j], lru_w_out[j])
        x2 = ffn(x2, ffn_norm[i], ffn_w_gu[i].astype(BF16), ffn_w_down[i].astype(BF16))
    return x2.reshape(B, T, D)
```

```python
import functools
import math

import jax
import jax.numpy as jnp
from jax import lax
from jax.experimental import pallas as pl
from jax.experimental.pallas import tpu as pltpu

F32 = jnp.float32
BF16 = jnp.bfloat16

NORM_EPS = 1e-6
HEAD_DIM = 128
LANES = 128
SUBLANES = 8
VMEM_LIMIT_BYTES = 56 * 1024 * 1024

GDN_CONV = 4
GDN_CHUNK = 64
DSW_GROUPS = ((128, 1), (512, 4), (2048, 16))
DSW_HEADS_PER_GROUP = 6
DSW_BLOCK = 128
ROPE_DIM = HEAD_DIM // 4
ROPE_THETA = 500000.0
LRU_BLOCK_DIM = 256
LRU_C = 8.0
NEG_BIG = -1e30


def _pick(dim, pref, align):
    if dim <= pref:
        return dim
    t = (pref // align) * align
    while t >= align:
        if dim % t == 0:
            return t
        t -= align
    return dim


def _col_tiles(w, tn):
    K, N = w.shape
    n = (N // tn) * tn
    return w[:, :n].reshape(K, n // tn, tn).transpose(1, 0, 2)


def _params(*sem):
    return pltpu.CompilerParams(dimension_semantics=sem, vmem_limit_bytes=VMEM_LIMIT_BYTES)


def _softplus(x):
    return jnp.maximum(x, 0.0) + jnp.log(1.0 + jnp.exp(-jnp.abs(x)))


def _sigmoid(x):
    return 1.0 / (1.0 + jnp.exp(-x))


def _silu(x):
    return x * _sigmoid(x)


def _rms(x, g):
    ms = jnp.mean(x * x, axis=-1, keepdims=True)
    return x * lax.rsqrt(ms + NORM_EPS) * g


def _norm_matmul_kernel(*refs, epilogue, n_extra):
    x_ref, g_ref, w_ref = refs[:3]
    extra = refs[3:3 + n_extra]
    o_ref, xn_ref = refs[3 + n_extra], refs[4 + n_extra]

    @pl.when(pl.program_id(1) == 0)
    def _():
        xn_ref[...] = _rms(x_ref[...], g_ref[...]).astype(BF16)

    y = jnp.dot(xn_ref[...], w_ref[...], preferred_element_type=F32)
    if epilogue is not None:
        y = epilogue(y, *[e[...] for e in extra])
    o_ref[...] = y.astype(o_ref.dtype)


def norm_matmul(x, g, w, *, col0=0, ncols=None, epilogue=None, row_extras=(), const_extras=(),
                out_dtype=F32, tm=1024, tn=1024, pipelined=False, name="norm_matmul"):
    M, K = x.shape
    ncols = w.shape[1] - col0 if ncols is None else ncols
    tm = _pick(M, tm, SUBLANES)
    tn = _pick(ncols, tn, LANES)
    assert col0 % tn == 0 and ncols % tn == 0 and M % tm == 0
    off = col0 // tn
    in_specs = [
        pl.BlockSpec((tm, K), lambda i, j: (i, 0)),
        pl.BlockSpec((1, K), lambda i, j: (0, 0)),
        pl.BlockSpec((None, K, tn), lambda i, j: (j + off, 0, 0)),
    ]
    w = _col_tiles(w, tn)
    args = [x, g.reshape(1, K), w]
    for e in row_extras:
        in_specs.append(pl.BlockSpec((tm, e.shape[1]), lambda i, j: (i, 0)))
        args.append(e)
    for e in const_extras:
        in_specs.append(pl.BlockSpec(e.shape, lambda i, j: (0, 0)))
        args.append(e)
    n_extra = len(row_extras) + len(const_extras)
    if pipelined:
        return _norm_matmul_pipelined(x, g, w, off, ncols, epilogue, row_extras, const_extras,
                                      out_dtype, tm, tn, name)
    return pl.pallas_call(
        functools.partial(_norm_matmul_kernel, epilogue=epilogue, n_extra=n_extra),
        grid=(M // tm, ncols // tn),
        in_specs=in_specs,
        out_specs=pl.BlockSpec((tm, tn), lambda i, j: (i, j)),
        out_shape=jax.ShapeDtypeStruct((M, ncols), out_dtype),
        scratch_shapes=[pltpu.VMEM((tm, K), BF16)],
        compiler_params=_params("parallel", "arbitrary"),
        name=name,
    )(*args)


def _norm_matmul_pipe_kernel(*refs, epilogue, n_extra, ncol, ntiles):
    x_ref, g_ref, w_ref = refs[:3]
    extra = refs[3:3 + n_extra]
    o_ref, xn_ref, y_ref = refs[3 + n_extra:6 + n_extra]
    t = pl.program_id(0)

    @pl.when(t == 0)
    def _():
        y_ref[1] = jnp.zeros(y_ref.shape[1:], F32)

    @pl.when((t % ncol == 0) & (t < ntiles))
    def _():
        xn_ref[...] = _rms(x_ref[...], g_ref[...]).astype(BF16)

    slot = t % 2
    o_ref[...] = epilogue(y_ref[1 - slot], *[e[...] for e in extra]).astype(o_ref.dtype)
    y_ref[slot] = jnp.dot(xn_ref[...], w_ref[...], preferred_element_type=F32)


def _norm_matmul_pipelined(x, g, w, off, ncols, epilogue, row_extras, const_extras, out_dtype,
                           tm, tn, name):
    M, K = x.shape
    ncol = ncols // tn
    ntiles = (M // tm) * ncol
    cur = lambda t: jnp.minimum(t, ntiles - 1)
    prv = lambda t: jnp.maximum(t - 1, 0)
    in_specs = [
        pl.BlockSpec((tm, K), lambda t: (cur(t) // ncol, 0)),
        pl.BlockSpec((1, K), lambda t: (0, 0)),
        pl.BlockSpec((None, K, tn), lambda t: (cur(t) % ncol + off, 0, 0)),
    ]
    args = [x, g.reshape(1, K), w]
    for e in row_extras:
        in_specs.append(pl.BlockSpec((tm, e.shape[1]), lambda t: (prv(t) // ncol, 0)))
        args.append(e)
    for e in const_extras:
        in_specs.append(pl.BlockSpec(e.shape, lambda t: (0, 0)))
        args.append(e)
    n_extra = len(row_extras) + len(const_extras)
    return pl.pallas_call(
        functools.partial(_norm_matmul_pipe_kernel, epilogue=epilogue, n_extra=n_extra,
                          ncol=ncol, ntiles=ntiles),
        grid=(ntiles + 1,),
        in_specs=in_specs,
        out_specs=pl.BlockSpec((tm, tn), lambda t: (prv(t) // ncol, prv(t) % ncol)),
        out_shape=jax.ShapeDtypeStruct((M, ncols), out_dtype),
        scratch_shapes=[pltpu.VMEM((tm, K), BF16), pltpu.VMEM((2, tm, tn), F32)],
        compiler_params=_params("arbitrary"),
        name=name,
    )(*args)


def _ep_headnorm(scale, rope):
    half = ROPE_DIM // 2

    def ep(y, *extra):
        yhs = [y[:, h * HEAD_DIM:(h + 1) * HEAD_DIM] for h in range(y.shape[1] // HEAD_DIM)]
        invs = [lax.rsqrt(jnp.mean(yh * yh, axis=-1, keepdims=True) + NORM_EPS) * scale for yh in yhs]
        if rope:
            cos_t, sin_t, gain = extra
            lane = lax.broadcasted_iota(jnp.int32, cos_t.shape, 1)
            gain_b = jnp.broadcast_to(gain, cos_t.shape)
            cg = cos_t * gain
            s_lo = jnp.where(lane < half, pltpu.roll(gain_b, HEAD_DIM - half, 1) * sin_t, 0.0)
            s_hi = jnp.where(lane >= half, pltpu.roll(gain_b, half, 1) * sin_t, 0.0)
            lo = [pltpu.roll(yh, HEAD_DIM - half, 1) for yh in yhs]
            hi = [pltpu.roll(yh, half, 1) for yh in yhs]
            outs = [(yh * cg + a * s_lo + b * s_hi) * inv for yh, a, b, inv in zip(yhs, lo, hi, invs)]
        else:
            (gain,) = extra
            outs = [yh * inv * gain for yh, inv in zip(yhs, invs)]
        return outs[0] if len(outs) == 1 else jnp.concatenate(outs, axis=1)

    return ep


def _ep_gelu(y):
    c = math.sqrt(2.0 / math.pi)
    return 0.5 * y * (1.0 + jnp.tanh(c * (y + 0.044715 * (y * y * y))))


def _ep_gdn_gates(n_heads):
    def ep(y, a_log, dt_bias):
        lane = lax.broadcasted_iota(jnp.int32, y.shape, 1)
        beta = _sigmoid(y)
        g = -jnp.exp(a_log) * _softplus(y + dt_bias)
        return jnp.where(lane < n_heads, beta, g)

    return ep


def _matmul_res_kernel(a_ref, w_ref, r_ref, o_ref):
    y = jnp.dot(a_ref[...].astype(BF16), w_ref[...], preferred_element_type=F32)
    o_ref[...] = r_ref[...] + y


def matmul_residual(a, w, res, *, tm=1024, name="out_proj"):
    M, K = a.shape
    N = w.shape[1]
    tm = _pick(M, tm, SUBLANES)
    tn = _pick(N, 1024 if K <= 2560 else 512, LANES)
    return pl.pallas_call(
        _matmul_res_kernel,
        grid=(M // tm, N // tn),
        in_specs=[
            pl.BlockSpec((tm, K), lambda i, j: (i, 0)),
            pl.BlockSpec((None, K, tn), lambda i, j: (j, 0, 0)),
            pl.BlockSpec((tm, tn), lambda i, j: (i, j)),
        ],
        out_specs=pl.BlockSpec((tm, tn), lambda i, j: (i, j)),
        out_shape=jax.ShapeDtypeStruct((M, N), F32),
        compiler_params=_params("parallel", "arbitrary"),
        name=name,
    )(a, _col_tiles(w, tn), res)


def _ffn_kernel(x_ref, g_ref, wg_ref, wu_ref, wd_ref, o_ref, xn_ref, *, sub):
    @pl.when(pl.program_id(1) == 0)
    def _():
        x = x_ref[...]
        xn_ref[...] = _rms(x, g_ref[...]).astype(BF16)
        o_ref[...] = x

    xn = xn_ref[...]
    tf = wg_ref.shape[1]
    hs = []
    for c in range(tf // sub):
        cs = slice(c * sub, (c + 1) * sub)
        gate = jnp.dot(xn, wg_ref[:, cs], preferred_element_type=F32)
        up = jnp.dot(xn, wu_ref[:, cs], preferred_element_type=F32)
        hs.append((_silu(gate) * up).astype(BF16))
    h = hs[0] if len(hs) == 1 else jnp.concatenate(hs, axis=1)
    o_ref[...] += jnp.dot(h, wd_ref[...], preferred_element_type=F32)


def ffn(x, g, w_gu, w_down, *, tm=512, tf=512, sub=256):
    M, D = x.shape
    Fh = w_down.shape[0]
    tm = _pick(M, tm, SUBLANES)
    tf = _pick(Fh, tf, LANES)
    nf = Fh // tf
    gu_tiles = _col_tiles(w_gu, tf)
    return pl.pallas_call(
        functools.partial(_ffn_kernel, sub=min(sub, tf)),
        grid=(M // tm, nf),
        in_specs=[
            pl.BlockSpec((tm, D), lambda i, j: (i, 0)),
            pl.BlockSpec((1, D), lambda i, j: (0, 0)),
            pl.BlockSpec((None, D, tf), lambda i, j: (j, 0, 0)),
            pl.BlockSpec((None, D, tf), lambda i, j: (j + nf, 0, 0)),
            pl.BlockSpec((tf, D), lambda i, j: (j, 0)),
        ],
        out_specs=pl.BlockSpec((tm, D), lambda i, j: (i, 0)),
        out_shape=jax.ShapeDtypeStruct((M, D), F32),
        scratch_shapes=[pltpu.VMEM((tm, D), BF16)],
        compiler_params=_params("parallel", "arbitrary"),
        name="ffn",
    )(x, g.reshape(1, D), gu_tiles, gu_tiles, w_down)


SB_STOP = 88.0


def _sb_kernel(q_ref, k_ref, v_ref, o_ref, *, blk, heads, n_q):
    W = 2 * blk
    row = lax.broadcasted_iota(jnp.int32, (blk, W), 0)
    col = lax.broadcasted_iota(jnp.int32, (blk, W), 1)
    r2 = lax.broadcasted_iota(jnp.int32, (W, W), 0)
    c2 = lax.broadcasted_iota(jnp.int32, (W, W), 1)
    suffix_w = jnp.where(r2 >= c2, 1.0, 0.0).astype(BF16)
    rb = lax.broadcasted_iota(jnp.int32, (blk, blk), 0)
    cb = lax.broadcasted_iota(jnp.int32, (blk, blk), 1)
    suffix_b = jnp.where(rb >= cb, 1.0, 0.0).astype(BF16)

    def dot_nt(a, b):
        return lax.dot_general(a, b, (((1,), (1,)), ((), ())), preferred_element_type=F32)

    def suffix_sum(sp, suffix):
        hi = sp.astype(BF16)
        lo = (sp - hi.astype(F32)).astype(BF16)
        return (jnp.dot(hi, suffix, preferred_element_type=F32)
                + jnp.dot(lo, suffix, preferred_element_type=F32))

    def qblock(i, carry):
        q0 = pl.multiple_of(i * blk, blk)
        w0 = pl.multiple_of(jnp.maximum(i - 1, 0) * blk, blk)
        valid = col < row + (q0 - w0)
        hsl = [slice(h * HEAD_DIM, (h + 1) * HEAD_DIM) for h in range(heads)]
        qs = [q_ref[pl.ds(q0, blk), sl] for sl in hsl]
        zs = [dot_nt(q, k_ref[pl.ds(w0, W), sl]) for q, sl in zip(qs, hsl)]
        sps = [jnp.where(valid, _softplus(z), 0.0) for z in zs]
        cums = [suffix_sum(sp, suffix_w) for sp in sps]
        ws = [jnp.exp(jnp.where(valid, z - cum, NEG_BIG)).astype(BF16) for z, cum in zip(zs, cums)]
        accs = [jnp.dot(w, v_ref[pl.ds(w0, W), sl], preferred_element_type=F32)
                for w, sl in zip(ws, hsl)]
        runs = [cum[:, 0:1] for cum in cums]
        def cond(c):
            least = functools.reduce(jnp.minimum, c[2])
            return (c[0] >= 0) & (jnp.min(least) < SB_STOP)

        def body(c):
            j, acc, run = c
            s0 = pl.multiple_of(j * blk, blk)
            z = [dot_nt(q, k_ref[pl.ds(s0, blk), sl]) for q, sl in zip(qs, hsl)]
            cum = [suffix_sum(_softplus(zh), suffix_b) for zh in z]
            w = [jnp.exp(zh - ch - rh).astype(BF16) for zh, ch, rh in zip(z, cum, run)]
            acc = tuple(a + jnp.dot(wh, v_ref[pl.ds(s0, blk), sl], preferred_element_type=F32)
                        for a, wh, sl in zip(acc, w, hsl))
            return j - 1, acc, tuple(rh + ch[:, 0:1] for rh, ch in zip(run, cum))

        _, accs, _ = lax.while_loop(cond, body, (i - 2, tuple(accs), tuple(runs)))
        for acc, sl in zip(accs, hsl):
            o_ref[pl.ds(q0, blk), sl] = acc.astype(o_ref.dtype)
        return carry

    lax.fori_loop(0, n_q, qblock, 0)


def sb_attention(q, k, v, *, blk=128, heads=4):
    B, T, HD = q.shape
    H = HD // HEAD_DIM
    assert T % blk == 0 and T >= 2 * blk and H % heads == 0
    seq = pl.BlockSpec((None, T, heads * HEAD_DIM), lambda b, h: (b, 0, h))
    return pl.pallas_call(
        functools.partial(_sb_kernel, blk=blk, heads=heads, n_q=T // blk),
        grid=(B, H // heads),
        in_specs=[seq, seq, seq],
        out_specs=seq,
        out_shape=jax.ShapeDtypeStruct((B, T, HD), BF16),
        compiler_params=_params("parallel", "parallel"),
        name="sb_attention",
    )(q, k, v)


def _bdot(a, b):
    return jnp.dot(a.astype(BF16), b.astype(BF16), preferred_element_type=F32)


def _bdot_nt(a, b):
    return lax.dot_general(a.astype(BF16), b.astype(BF16), (((1,), (1,)), ((), ())),
                           preferred_element_type=F32)


def _bdot_tn(a, b):
    return lax.dot_general(a.astype(BF16), b.astype(BF16), (((0,), (0,)), ((), ())),
                           preferred_element_type=F32)


def _gdn_kernel(xq_ref, xk_ref, xv_ref, z_ref, cwq_ref, cwk_ref, cwv_ref, g_ref, beta_ref, gain_ref,
                o_ref, q_ref, k_ref, v_ref, hq_ref, hk_ref, hv_ref, buf_ref,
                u_ref, wq_ref, kd_ref, attn_ref, gl_ref, out_ref, state_ref,
                *, chunk, kheads, rep, group, taps):
    C = chunk

    @pl.when(pl.program_id(2) == 0)
    def _():
        state_ref[...] = jnp.zeros_like(state_ref)
        hq_ref[...] = jnp.zeros_like(hq_ref)
        hk_ref[...] = jnp.zeros_like(hk_ref)
        hv_ref[...] = jnp.zeros_like(hv_ref)

    def conv_silu(x_ref, halo_ref, w_ref, dst_ref, l2_scale):
        n, width = x_ref.shape
        buf_ref[0:SUBLANES, 0:width] = halo_ref[...]
        buf_ref[SUBLANES:SUBLANES + n, 0:width] = x_ref[...]
        halo_ref[...] = x_ref[n - SUBLANES:n, :]
        w = w_ref[...]
        y = w[taps - 1:taps, :] * x_ref[...]
        for kk in range(taps - 1):
            shift = taps - 1 - kk
            y = y + w[kk:kk + 1, :] * buf_ref[SUBLANES - shift:SUBLANES - shift + n, 0:width]
        y = _silu(y)
        for h in range(width // HEAD_DIM):
            sl = slice(h * HEAD_DIM, (h + 1) * HEAD_DIM)
            yh = y[:, sl]
            if l2_scale is not None:
                ss = jnp.sum(yh * yh, axis=-1, keepdims=True)
                yh = yh * (lax.rsqrt(ss + NORM_EPS) * l2_scale)
            dst_ref[:, sl] = yh

    conv_silu(xq_ref, hq_ref, cwq_ref, q_ref, HEAD_DIM ** -0.5)
    conv_silu(xk_ref, hk_ref, cwk_ref, k_ref, 1.0)
    conv_silu(xv_ref, hv_ref, cwv_ref, v_ref, None)

    ri = lax.broadcasted_iota(jnp.int32, (C, C), 0)
    ci = lax.broadcasted_iota(jnp.int32, (C, C), 1)
    eye = ri == ci
    lower = ci <= ri
    strict = ci < ri
    gain = gain_ref[...]
    n_double = max(1, int(math.ceil(math.log2(C))) - 1)
    nv = kheads * rep
    heads = [slice(r * HEAD_DIM, (r + 1) * HEAD_DIM) for r in range(nv)]
    G = group
    rows = G * C
    bmm = functools.partial(jnp.einsum, "gij,gjk->gik", preferred_element_type=F32)
    bmm_nt = functools.partial(jnp.einsum, "gid,gjd->gij", preferred_element_type=F32)

    def prepare(khs):
        rs = [r for kh in khs for r in range(kh * rep, (kh + 1) * rep)]
        every = lambda f: [f(i) for i in range(len(rs))]
        qs = {kh: q_ref[:, heads[kh]].reshape(G, C, HEAD_DIM) for kh in khs}
        ks = {kh: k_ref[:, heads[kh]].reshape(G, C, HEAD_DIM) for kh in khs}
        k16s = {kh: ks[kh].astype(BF16) for kh in khs}
        qks = {kh: bmm_nt(qs[kh].astype(BF16), k16s[kh]) for kh in khs}
        q = every(lambda i: qs[rs[i] // rep])
        k = every(lambda i: ks[rs[i] // rep])
        k16 = every(lambda i: k16s[rs[i] // rep])
        g_row = every(lambda i: g_ref[rs[i]])
        beta_row = every(lambda i: beta_ref[rs[i]])
        g_col = every(lambda i: jnp.sum(jnp.where(eye, g_row[i], 0.0), axis=2, keepdims=True))
        beta_col = every(lambda i: jnp.sum(jnp.where(eye, beta_row[i], 0.0), axis=2, keepdims=True))
        gc_col = every(lambda i: jnp.sum(jnp.where(lower, g_row[i], 0.0), axis=2, keepdims=True))
        gc_row = every(lambda i: jnp.sum(jnp.where(ri <= ci, g_col[i], 0.0), axis=1, keepdims=True))
        g_last = every(lambda i: jnp.sum(g_row[i], axis=2, keepdims=True))
        decay = every(lambda i: jnp.where(
            lower, jnp.exp(jnp.where(lower, gc_col[i] - gc_row[i], 0.0)), 0.0))
        eg = every(lambda i: jnp.exp(gc_col[i]))
        kb = every(lambda i: k[i] * beta_col[i])
        a_mat = every(lambda i: jnp.where(strict, bmm_nt(kb[i].astype(BF16), k16[i]) * decay[i], 0.0))
        t_mat = every(lambda i: jnp.where(eye, 1.0, 0.0) - a_mat[i])
        p = a_mat
        for _ in range(n_double):
            p16 = every(lambda i: p[i].astype(BF16))
            p = every(lambda i: bmm(p16[i], p16[i]))
            t_mat = every(lambda i: t_mat[i] + bmm(t_mat[i].astype(BF16), p[i].astype(BF16)))
        rhs = every(lambda i: jnp.concatenate(
            [v_ref[:, heads[rs[i]]].reshape(G, C, HEAD_DIM) * beta_col[i], kb[i] * eg[i]],
            axis=2).astype(BF16))
        uw = every(lambda i: bmm(t_mat[i].astype(BF16), rhs[i]))
        for i, r in enumerate(rs):
            u_ref[r] = uw[i][:, :, :HEAD_DIM].reshape(rows, HEAD_DIM)
            wq_ref[r] = jnp.concatenate([uw[i][:, :, HEAD_DIM:], q[i] * eg[i]], axis=1).astype(
                BF16).reshape(2 * rows, HEAD_DIM)
            kd_ref[r] = (k[i] * jnp.exp(g_last[i] - gc_col[i])).astype(BF16).reshape(rows, HEAD_DIM)
            attn_ref[r] = (qks[r // rep] * decay[i]).astype(BF16).reshape(rows, C)
            gl_ref[r] = jnp.broadcast_to(g_last[i], (G, 1, HEAD_DIM))

    def recur(nl, states):
        ls = pl.multiple_of(nl * C, C)
        ls2 = pl.multiple_of(nl * 2 * C, 2 * C)
        hs = range(nv)
        s16 = [states[r].astype(BF16) for r in hs]
        ws = [jnp.dot(wq_ref[r, pl.ds(ls2, 2 * C), :], s16[r], preferred_element_type=F32) for r in hs]
        v_new = [(u_ref[r, pl.ds(ls, C), :] - ws[r][:C]).astype(BF16) for r in hs]
        upd = [lax.dot_general(kd_ref[r, pl.ds(ls, C), :], v_new[r], (((0,), (0,)), ((), ())),
                               preferred_element_type=F32) for r in hs]
        new = tuple(states[r] * jnp.exp(gl_ref[r, nl]) + upd[r] for r in hs)
        for r in hs:
            out_ref[r, pl.ds(ls, C), :] = ws[r][C:] + jnp.dot(
                attn_ref[r, pl.ds(ls, C), :], v_new[r], preferred_element_type=F32)
        return new

    for k0 in range(0, kheads, 2):
        prepare(list(range(k0, min(k0 + 2, kheads))))
    states = lax.fori_loop(0, group, recur, tuple(state_ref[r] for r in range(nv)))
    for r in range(nv):
        state_ref[r] = states[r]
        o_ref[:, heads[r]] = (_rms(out_ref[r], gain) * _silu(z_ref[:, heads[r]])).astype(o_ref.dtype)


def gdn_delta_rule(proj, conv_w, KD, VD, g, beta, gain, *, chunk=GDN_CHUNK, group=8, kheads=4):
    B, T, _ = proj.shape
    taps = conv_w.shape[0]
    Hk, Hv = KD // HEAD_DIM, VD // HEAD_DIM
    rep = Hv // Hk
    N = T // chunk
    group = _pick(N, group, 1)
    kheads = _pick(Hk, kheads, 1)
    nv = kheads * rep
    kw, vw = kheads * HEAD_DIM, nv * HEAD_DIM
    assert KD % kw == 0 and (2 * KD) % vw == 0 and N % group == 0
    koff, voff, zoff = KD // kw, 2 * KD // vw, (2 * KD + VD) // vw
    rows = group * chunk
    assert rows % SUBLANES == 0
    tile = lambda width, f: pl.BlockSpec((None, rows, width), f)
    tapw = lambda width, f: pl.BlockSpec((taps, width), f)
    gate = pl.BlockSpec((None, nv, group, 1, chunk), lambda b, h, t: (b, h, t, 0, 0))
    return pl.pallas_call(
        functools.partial(_gdn_kernel, chunk=chunk, kheads=kheads, rep=rep, group=group, taps=taps),
        grid=(B, Hk // kheads, N // group),
        in_specs=[
            tile(kw, lambda b, h, t: (b, t, h)),
            tile(kw, lambda b, h, t: (b, t, h + koff)),
            tile(vw, lambda b, h, t: (b, t, h + voff)),
            tile(vw, lambda b, h, t: (b, t, h + zoff)),
            tapw(kw, lambda b, h, t: (0, h)),
            tapw(kw, lambda b, h, t: (0, h + koff)),
            tapw(vw, lambda b, h, t: (0, h + voff)),
            gate,
            gate,
            pl.BlockSpec((1, HEAD_DIM), lambda b, h, t: (0, 0)),
        ],
        out_specs=tile(vw, lambda b, h, t: (b, t, h)),
        out_shape=jax.ShapeDtypeStruct((B, T, VD), BF16),
        scratch_shapes=[
            pltpu.VMEM((rows, kw), F32),
            pltpu.VMEM((rows, kw), F32),
            pltpu.VMEM((rows, vw), F32),
            pltpu.VMEM((SUBLANES, kw), F32),
            pltpu.VMEM((SUBLANES, kw), F32),
            pltpu.VMEM((SUBLANES, vw), F32),
            pltpu.VMEM((rows + SUBLANES, vw), F32),
            pltpu.VMEM((nv, rows, HEAD_DIM), F32),
            pltpu.VMEM((nv, 2 * rows, HEAD_DIM), BF16),
            pltpu.VMEM((nv, rows, HEAD_DIM), BF16),
            pltpu.VMEM((nv, rows, chunk), BF16),
            pltpu.VMEM((nv, group, 1, HEAD_DIM), F32),
            pltpu.VMEM((nv, rows, HEAD_DIM), F32),
            pltpu.VMEM((nv, HEAD_DIM, HEAD_DIM), F32),
        ],
        compiler_params=_params("parallel", "parallel", "arbitrary"),
        name="gdn_delta_rule",
    )(proj, proj, proj, proj, conv_w, conv_w, conv_w, g, beta, gain)


def _rope_table_kernel(pos_ref, freq_ref, sign_ref, cos_ref, sin_ref):
    ang = pos_ref[...].astype(F32) * freq_ref[...]
    cos_ref[...] = jnp.cos(ang)
    sin_ref[...] = jnp.sin(ang) * sign_ref[...]


def rope_tables(pos, *, tm=1024):
    M = pos.shape[0]
    tm = _pick(M, tm, SUBLANES)
    half = ROPE_DIM // 2
    lane = jnp.arange(HEAD_DIM)
    inv_freq = ROPE_THETA ** (-(lane % half).astype(F32) / half)
    freq = jnp.where(lane < ROPE_DIM, inv_freq, 0.0).reshape(1, HEAD_DIM).astype(F32)
    sign = jnp.where(lane < half, -1.0, 1.0).reshape(1, HEAD_DIM).astype(F32)
    spec = pl.BlockSpec((tm, HEAD_DIM), lambda i: (i, 0))
    const = pl.BlockSpec((1, HEAD_DIM), lambda i: (0, 0))
    return pl.pallas_call(
        _rope_table_kernel,
        grid=(M // tm,),
        in_specs=[pl.BlockSpec((tm, 1), lambda i: (i, 0)), const, const],
        out_specs=[spec, spec],
        out_shape=[jax.ShapeDtypeStruct((M, HEAD_DIM), F32)] * 2,
        compiler_params=_params("parallel"),
        name="rope_tables",
    )(pos, freq, sign)


def _band_kernel(q_ref, kc_ref, kp_ref, vc_ref, vp_ref, o_ref, lse_ref, *, blk, span, heads, d, batch):
    n = pl.program_id(2)
    qi = lax.broadcasted_iota(jnp.int32, (blk, 2 * blk), 0)
    kj = lax.broadcasted_iota(jnp.int32, (blk, 2 * blk), 1)
    steps = blk + qi - kj
    first_key = jnp.where(n > 0, 0, blk)
    valid = (steps >= 0) & (steps <= span) & (kj >= first_key)
    streams = [(r, h) for r in range(d) for h in range(heads)]

    def rows(ref, r, h):
        sl = slice(h * HEAD_DIM, (h + 1) * HEAD_DIM)
        return ref[:, sl] if d == 1 else ref[pl.ds(r, blk, stride=d), sl]

    for b0 in range(0, len(streams), batch):
        part = streams[b0:b0 + batch]
        s_list = []
        for r, h in part:
            q = rows(q_ref, r, h).astype(BF16)
            kw = jnp.concatenate([rows(kp_ref, r, h), rows(kc_ref, r, h)], axis=0).astype(BF16)
            s = lax.dot_general(q, kw, (((1,), (1,)), ((), ())), preferred_element_type=F32)
            s_list.append(jnp.where(valid, s, NEG_BIG))
        m_list = [jnp.max(s, axis=-1, keepdims=True) for s in s_list]
        p_list = [jnp.where(valid, jnp.exp(s - m), 0.0) for s, m in zip(s_list, m_list)]
        den_list = [jnp.sum(p, axis=-1, keepdims=True) for p in p_list]
        for (r, h), p, m, den in zip(part, p_list, m_list, den_list):
            vw = jnp.concatenate([rows(vp_ref, r, h), rows(vc_ref, r, h)], axis=0).astype(BF16)
            o = jnp.dot((p / den).astype(BF16), vw, preferred_element_type=F32)
            lse = jnp.broadcast_to(m + jnp.log(den), (blk, HEAD_DIM))
            sl = slice(h * HEAD_DIM, (h + 1) * HEAD_DIM)
            if d == 1:
                o_ref[:, sl] = o
                lse_ref[:, sl] = lse
            else:
                o_ref[pl.ds(r, blk, stride=d), sl] = o
                lse_ref[pl.ds(r, blk, stride=d), sl] = lse


def band_attention(q, k, v, group, window, dilation, *, blk=DSW_BLOCK, group_heads=DSW_HEADS_PER_GROUP):
    B, T, HD = q.shape
    d = dilation
    span = window // d
    unit = d * blk
    assert span <= blk and T % unit == 0
    heads = group_heads if d == 1 else 1
    assert group_heads % heads == 0
    hw = heads * HEAD_DIM
    col0 = group * group_heads // heads
    cur = pl.BlockSpec((None, unit, hw), lambda b, c, n: (b, n, col0 + c))
    prev = pl.BlockSpec((None, unit, hw), lambda b, c, n: (b, jnp.maximum(n - 1, 0), col0 + c))
    out = pl.BlockSpec((None, unit, hw), lambda b, c, n: (b, n, c))
    return pl.pallas_call(
        functools.partial(_band_kernel, blk=blk, span=span, heads=heads, d=d, batch=6 if d == 1 else 4),
        grid=(B, group_heads // heads, T // unit),
        in_specs=[cur, cur, prev, cur, prev],
        out_specs=[out, out],
        out_shape=[jax.ShapeDtypeStruct((B, T, group_heads * HEAD_DIM), F32)] * 2,
        compiler_params=_params("parallel", "parallel", "arbitrary"),
        name=f"band_attention_d{d}",
    )(q, k, k, v, v)


def _combine_kernel(*refs, n):
    o_refs, l_refs, out_ref = refs[:n], refs[n:2 * n], refs[2 * n]
    ls = [r[...] for r in l_refs]
    m = functools.reduce(jnp.maximum, ls)
    es = [jnp.exp(l - m) for l in ls]
    inv = 1.0 / functools.reduce(lambda a, b: a + b, es)
    w = o_refs[0].shape[1]
    for gi in range(n):
        out_ref[:, gi * w:(gi + 1) * w] = (o_refs[gi][...] * (es[gi] * inv)).astype(out_ref.dtype)


def combine_groups(os_, lses, *, tm=512):
    n = len(os_)
    M, w = os_[0].shape
    tm = _pick(M, tm, SUBLANES)
    spec = pl.BlockSpec((tm, w), lambda i: (i, 0))
    return pl.pallas_call(
        functools.partial(_combine_kernel, n=n),
        grid=(M // tm,),
        in_specs=[spec] * (2 * n),
        out_specs=pl.BlockSpec((tm, n * w), lambda i: (i, 0)),
        out_shape=jax.ShapeDtypeStruct((M, n * w), BF16),
        compiler_params=_params("parallel"),
        name="combine_groups",
    )(*os_, *lses)


def _lru_kernel(x_ref, gate_ref, cw_ref, cb_ref, wa_ref, wx_ref, ba_ref, bx_ref, lam_ref, o_ref,
                a_ref, u_ref, h_ref, xc_ref, halo_ref, buf_ref, *, tt, nblk, taps):
    @pl.when(pl.program_id(1) == 0)
    def _():
        h_ref[...] = jnp.zeros_like(h_ref)
        halo_ref[...] = jnp.zeros_like(halo_ref)

    buf_ref[0:SUBLANES, :] = halo_ref[...]
    buf_ref[SUBLANES:SUBLANES + tt, :] = x_ref[...]
    halo_ref[...] = x_ref[tt - SUBLANES:tt, :]
    cw = cw_ref[...]
    xc = cb_ref[...] + cw[taps - 1:taps, :] * x_ref[...]
    for kk in range(taps - 1):
        shift = taps - 1 - kk
        xc = xc + cw[kk:kk + 1, :] * buf_ref[SUBLANES - shift:SUBLANES - shift + tt, :]
    xc_ref[...] = xc

    for nb in range(nblk):
        sl = slice(nb * LRU_BLOCK_DIM, (nb + 1) * LRU_BLOCK_DIM)
        xb = xc_ref[:, sl]
        xb16 = xb.astype(BF16)
        r = _sigmoid(jnp.dot(xb16, wa_ref[nb], preferred_element_type=F32) + ba_ref[:, sl])
        ig = _sigmoid(jnp.dot(xb16, wx_ref[nb], preferred_element_type=F32) + bx_ref[:, sl])
        log_a = (-LRU_C) * r * _softplus(-lam_ref[:, sl])
        a = jnp.exp(log_a)
        a_ref[:, sl] = a
        u_ref[:, sl] = jnp.sqrt(1.0 - a * a) * (ig * xb)

    def group(gi, h):
        base = pl.multiple_of(gi * SUBLANES, SUBLANES)
        for r8 in range(SUBLANES):
            h = a_ref[pl.ds(base + r8, 1), :] * h + u_ref[pl.ds(base + r8, 1), :]
            u_ref[pl.ds(base + r8, 1), :] = h
        return h

    h_ref[...] = lax.fori_loop(0, tt // SUBLANES, group, h_ref[...])
    o_ref[...] = (u_ref[...] * gate_ref[...].astype(F32)).astype(o_ref.dtype)


def lru_scan(xr, gate, conv_w, conv_b, w_a, w_x, b_a, b_x, lam, *, tt=256):
    B, T, W = xr.shape
    nblk = W // LRU_BLOCK_DIM
    taps = conv_w.shape[0]
    tt = _pick(T, tt, SUBLANES)
    tile = pl.BlockSpec((None, tt, W), lambda b, i: (b, i, 0))
    wspec = pl.BlockSpec((nblk, LRU_BLOCK_DIM, LRU_BLOCK_DIM), lambda b, i: (0, 0, 0))
    vec = pl.BlockSpec((1, W), lambda b, i: (0, 0))
    return pl.pallas_call(
        functools.partial(_lru_kernel, tt=tt, nblk=nblk, taps=taps),
        grid=(B, T // tt),
        in_specs=[tile, tile, pl.BlockSpec((taps, W), lambda b, i: (0, 0)), vec,
                  wspec, wspec, vec, vec, vec],
        out_specs=tile,
        out_shape=jax.ShapeDtypeStruct((B, T, W), BF16),
        scratch_shapes=[
            pltpu.VMEM((tt, W), F32),
            pltpu.VMEM((tt, W), F32),
            pltpu.VMEM((1, W), F32),
            pltpu.VMEM((tt, W), F32),
            pltpu.VMEM((SUBLANES, W), F32),
            pltpu.VMEM((tt + SUBLANES, W), F32),
        ],
        compiler_params=_params("parallel", "arbitrary"),
        name="lru_scan",
    )(xr, gate, conv_w, conv_b, w_a, w_x, b_a, b_x, lam)


def _row(v):
    return v.reshape(1, -1).astype(F32)


def sb_mixer(x2, B, T, norm_g, w_in, q_norm, k_norm, w_out):
    hd = w_in.shape[1] // 3
    w_in = w_in.astype(BF16)
    scale = HEAD_DIM ** -0.5
    q = norm_matmul(x2, norm_g, w_in, col0=0, ncols=hd, epilogue=_ep_headnorm(scale, False),
                    const_extras=(_row(q_norm),), out_dtype=BF16, name="sb_proj_q")
    k = norm_matmul(x2, norm_g, w_in, col0=hd, ncols=hd, epilogue=_ep_headnorm(1.0, False),
                    const_extras=(_row(k_norm),), out_dtype=BF16, name="sb_proj_k")
    v = norm_matmul(x2, norm_g, w_in, col0=2 * hd, ncols=hd, out_dtype=BF16, name="sb_proj_v")
    shp = (B, T, hd)
    o = sb_attention(q.reshape(shp), k.reshape(shp), v.reshape(shp))
    return matmul_residual(o.reshape(B * T, hd), w_out.astype(BF16), x2, name="sb_out")


def gdn_mixer(x2, B, T, norm_g, w_in, conv_w, a_log, dt_bias, o_norm, w_out):
    M = B * T
    Hv = a_log.shape[0]
    vd = Hv * HEAD_DIM
    kd = (conv_w.shape[1] - vd) // 2
    main = 2 * kd + 2 * vd
    proj = norm_matmul(x2, norm_g, w_in.astype(BF16), col0=0, ncols=main, name="gdn_proj")
    w_ba = jnp.pad(w_in[:, main:], ((0, 0), (0, LANES - 2 * Hv))).astype(BF16)
    pad = lambda p: jnp.pad(p.astype(F32), (Hv, LANES - 2 * Hv)).reshape(1, LANES)
    gates = norm_matmul(x2, norm_g, w_ba, epilogue=_ep_gdn_gates(Hv),
                        const_extras=(pad(a_log), pad(dt_bias)), name="gdn_gates")
    N = T // GDN_CHUNK
    per_head = lambda a: a.reshape(B, T, Hv).transpose(0, 2, 1).reshape(B, Hv, N, 1, GDN_CHUNK)
    beta = per_head(gates[:, :Hv])
    g = per_head(gates[:, Hv:2 * Hv])
    o = gdn_delta_rule(proj.reshape(B, T, main), conv_w, kd, vd, g, beta, _row(o_norm))
    return matmul_residual(o.reshape(M, vd), w_out.astype(BF16), x2, name="gdn_out")


def dsw_mixer(x2, B, T, norm_g, positions, w_in, q_norm, k_norm, w_out):
    M = B * T
    hd = w_in.shape[1] // 3
    w_in = w_in.astype(BF16)
    cos_t, sin_t = rope_tables(positions.reshape(M, 1).astype(jnp.int32))
    scale = HEAD_DIM ** -0.5
    tn = 3 * HEAD_DIM * 2
    q = norm_matmul(x2, norm_g, w_in, col0=0, ncols=hd, epilogue=_ep_headnorm(scale, True),
                    row_extras=(cos_t, sin_t), const_extras=(_row(q_norm),), out_dtype=F32, tn=tn,
                    pipelined=True, name="dsw_proj_q")
    k = norm_matmul(x2, norm_g, w_in, col0=hd, ncols=hd, epilogue=_ep_headnorm(1.0, True),
                    row_extras=(cos_t, sin_t), const_extras=(_row(k_norm),), out_dtype=F32, tn=tn,
                    pipelined=True, name="dsw_proj_k")
    v = norm_matmul(x2, norm_g, w_in, col0=2 * hd, ncols=hd, out_dtype=F32, tn=tn, name="dsw_proj_v")
    shp = (B, T, hd)
    q, k, v = q.reshape(shp), k.reshape(shp), v.reshape(shp)
    os_, lses = [], []
    for gi, (window, dilation) in enumerate(DSW_GROUPS):
        o_g, lse_g = band_attention(q, k, v, gi, window, dilation)
        os_.append(o_g.reshape(M, -1))
        lses.append(lse_g.reshape(M, -1))
    o = combine_groups(os_, lses)
    return matmul_residual(o, w_out.astype(BF16), x2, name="dsw_out")


def lru_mixer(x2, B, T, norm_g, w_in, conv_w, conv_b, w_a, b_a, w_x, b_x, lam, w_out):
    W = w_in.shape[1] // 2
    w_in = w_in.astype(BF16)
    gate = norm_matmul(x2, norm_g, w_in, col0=0, ncols=W, epilogue=_ep_gelu, out_dtype=BF16,
                       name="lru_proj_gate")
    xr = norm_matmul(x2, norm_g, w_in, col0=W, ncols=W, name="lru_proj_x")
    y = lru_scan(xr.reshape(B, T, W), gate.reshape(B, T, W), conv_w, _row(conv_b),
                 w_a.astype(BF16), w_x.astype(BF16), _row(b_a), _row(b_x), _row(lam))
    return matmul_residual(y.reshape(B * T, W), w_out.astype(BF16), x2, name="lru_out")


def kernel(x, positions, mix_norm, ffn_norm, ffn_w_gu, ffn_w_down, sb_w_in, sb_q_norm, sb_k_norm, sb_w_out, gdn_w_in, gdn_conv_w, gdn_a_log, gdn_dt_bias, gdn_o_norm, gdn_w_out, dsw_w_in, dsw_q_norm, dsw_k_norm, dsw_w_out, lru_w_in, lru_conv_w, lru_conv_b, lru_w_a, lru_b_a, lru_w_x, lru_b_x, lru_lambda, lru_w_out):
    B, T, D = x.shape
    depth = mix_norm.shape[0]
    x2 = x.reshape(B * T, D)
    for i in range(depth):
        kind, j = i % 4, i // 4
        if kind == 0:
            x2 = sb_mixer(x2, B, T, mix_norm[i], sb_w_in[j], sb_q_norm[j], sb_k_norm[j], sb_w_out[j])
        elif kind == 1:
            x2 = gdn_mixer(x2, B, T, mix_norm[i], gdn_w_in[j], gdn_conv_w[j], gdn_a_log[j],
                           gdn_dt_bias[j], gdn_o_norm[j], gdn_w_out[j])
        elif kind == 2:
            x2 = dsw_mixer(x2, B, T, mix_norm[i], positions, dsw_w_in[j], dsw_q_norm[j],
                           dsw_k_norm[j], dsw_w_out[j])
        else:
            x2 = lru_mixer(x2, B, T, mix_norm[i], lru_w_in[j], lru_conv_w[j], lru_conv_b[j],
                           lru_w_a[j], lru_b_a[j], lru_w_x[j], lru_b_x[j], lru_lambda[j], lru_w_out[j])
        x2 = ffn(x2, ffn_norm[i], ffn_w_gu[i].astype(BF16), ffn_w_down[i].astype(BF16))
    return x2.reshape(B, T, D)
```

```python
import functools
import math

import jax
import jax.numpy as jnp
from jax import lax
from jax.experimental import pallas as pl
from jax.experimental.pallas import tpu as pltpu

F32 = jnp.float32
BF16 = jnp.bfloat16

NORM_EPS = 1e-6
HEAD_DIM = 128
LANES = 128
SUBLANES = 8
VMEM_LIMIT_BYTES = 56 * 1024 * 1024

GDN_CONV = 4
GDN_CHUNK = 64
DSW_GROUPS = ((128, 1), (512, 4), (2048, 16))
DSW_HEADS_PER_GROUP = 6
DSW_BLOCK = 128
ROPE_DIM = HEAD_DIM // 4
ROPE_THETA = 500000.0
LRU_BLOCK_DIM = 256
LRU_C = 8.0
NEG_BIG = -1e30


def _pick(dim, pref, align):
    if dim <= pref:
        return dim
    t = (pref // align) * align
    while t >= align:
        if dim % t == 0:
            return t
        t -= align
    return dim


def _params(*sem):
    return pltpu.CompilerParams(dimension_semantics=sem, vmem_limit_bytes=VMEM_LIMIT_BYTES)


def _softplus(x):
    return jnp.maximum(x, 0.0) + jnp.log(1.0 + jnp.exp(-jnp.abs(x)))


def _sigmoid(x):
    return 1.0 / (1.0 + jnp.exp(-x))


def _silu(x):
    return x * _sigmoid(x)


def _rms(x, g):
    ms = jnp.mean(x * x, axis=-1, keepdims=True)
    return x * lax.rsqrt(ms + NORM_EPS) * g


def _norm_matmul_kernel(*refs, epilogue, n_extra):
    x_ref, g_ref, w_ref = refs[:3]
    extra = refs[3:3 + n_extra]
    o_ref, xn_ref = refs[3 + n_extra], refs[4 + n_extra]
    j = pl.program_id(1)

    @pl.when(j == 0)
    def _():
        xn_ref[...] = _rms(x_ref[...], g_ref[...]).astype(BF16)

    y = jnp.dot(xn_ref[...], w_ref[...], preferred_element_type=F32)
    ex = [e[...] for e in extra]
    if isinstance(epilogue, list):
        for lo, hi, fn in epilogue:
            @pl.when((j >= lo) & (j < hi))
            def _(fn=fn):
                o_ref[...] = (y if fn is None else fn(y, *ex)).astype(o_ref.dtype)
        return
    if epilogue is not None:
        y = epilogue(y, *ex)
    o_ref[...] = y.astype(o_ref.dtype)


def norm_matmul(x, g, w, *, col0=0, ncols=None, epilogue=None, row_extras=(), const_extras=(),
                out_dtype=F32, tm=1024, tn=1024, pipelined=False, name="norm_matmul"):
    M, K = x.shape
    ncols = w.shape[1] - col0 if ncols is None else ncols
    tm = _pick(M, tm, SUBLANES)
    tn = _pick(ncols, tn, LANES)
    assert col0 % tn == 0 and ncols % tn == 0 and M % tm == 0
    off = col0 // tn
    in_specs = [
        pl.BlockSpec((tm, K), lambda i, j: (i, 0)),
        pl.BlockSpec((1, K), lambda i, j: (0, 0)),
        pl.BlockSpec((K, tn), lambda i, j: (0, j + off)),
    ]
    args = [x, g.reshape(1, K), w]
    for e in row_extras:
        in_specs.append(pl.BlockSpec((tm, e.shape[1]), lambda i, j: (i, 0)))
        args.append(e)
    for e in const_extras:
        in_specs.append(pl.BlockSpec(e.shape, lambda i, j: (0, 0)))
        args.append(e)
    n_extra = len(row_extras) + len(const_extras)
    if pipelined:
        return _norm_matmul_pipelined(x, g, w, off, ncols, epilogue, row_extras, const_extras,
                                      out_dtype, tm, tn, name)
    return pl.pallas_call(
        functools.partial(_norm_matmul_kernel, epilogue=epilogue, n_extra=n_extra),
        grid=(M // tm, ncols // tn),
        in_specs=in_specs,
        out_specs=pl.BlockSpec((tm, tn), lambda i, j: (i, j)),
        out_shape=jax.ShapeDtypeStruct((M, ncols), out_dtype),
        scratch_shapes=[pltpu.VMEM((tm, K), BF16)],
        compiler_params=_params("parallel", "arbitrary"),
        name=name,
    )(*args)


def _norm_matmul_pipe_kernel(*refs, epilogue, n_extra, ncol, ntiles):
    x_ref, g_ref, w_ref = refs[:3]
    extra = refs[3:3 + n_extra]
    o_ref, xn_ref, y_ref = refs[3 + n_extra:6 + n_extra]
    t = pl.program_id(0)

    @pl.when(t == 0)
    def _():
        y_ref[1] = jnp.zeros(y_ref.shape[1:], F32)

    @pl.when((t % ncol == 0) & (t < ntiles))
    def _():
        xn_ref[...] = _rms(x_ref[...], g_ref[...]).astype(BF16)

    slot = t % 2
    o_ref[...] = epilogue(y_ref[1 - slot], *[e[...] for e in extra]).astype(o_ref.dtype)
    y_ref[slot] = jnp.dot(xn_ref[...], w_ref[...], preferred_element_type=F32)


def _norm_matmul_pipelined(x, g, w, off, ncols, epilogue, row_extras, const_extras, out_dtype,
                           tm, tn, name):
    M, K = x.shape
    ncol = ncols // tn
    ntiles = (M // tm) * ncol
    cur = lambda t: jnp.minimum(t, ntiles - 1)
    prv = lambda t: jnp.maximum(t - 1, 0)
    in_specs = [
        pl.BlockSpec((tm, K), lambda t: (cur(t) // ncol, 0)),
        pl.BlockSpec((1, K), lambda t: (0, 0)),
        pl.BlockSpec((K, tn), lambda t: (0, cur(t) % ncol + off)),
    ]
    args = [x, g.reshape(1, K), w]
    for e in row_extras:
        in_specs.append(pl.BlockSpec((tm, e.shape[1]), lambda t: (prv(t) // ncol, 0)))
        args.append(e)
    for e in const_extras:
        in_specs.append(pl.BlockSpec(e.shape, lambda t: (0, 0)))
        args.append(e)
    n_extra = len(row_extras) + len(const_extras)
    return pl.pallas_call(
        functools.partial(_norm_matmul_pipe_kernel, epilogue=epilogue, n_extra=n_extra,
                          ncol=ncol, ntiles=ntiles),
        grid=(ntiles + 1,),
        in_specs=in_specs,
        out_specs=pl.BlockSpec((tm, tn), lambda t: (prv(t) // ncol, prv(t) % ncol)),
        out_shape=jax.ShapeDtypeStruct((M, ncols), out_dtype),
        scratch_shapes=[pltpu.VMEM((tm, K), BF16), pltpu.VMEM((2, tm, tn), F32)],
        compiler_params=_params("arbitrary"),
        name=name,
    )(*args)


def _ep_headnorm(scale, rope):
    half = ROPE_DIM // 2

    def ep(y, *extra):
        yhs = [y[:, h * HEAD_DIM:(h + 1) * HEAD_DIM] for h in range(y.shape[1] // HEAD_DIM)]
        invs = [lax.rsqrt(jnp.mean(yh * yh, axis=-1, keepdims=True) + NORM_EPS) * scale for yh in yhs]
        if rope:
            cos_t, sin_t, gain = extra
            lane = lax.broadcasted_iota(jnp.int32, cos_t.shape, 1)
            gain_b = jnp.broadcast_to(gain, cos_t.shape)
            cg = cos_t * gain
            s_lo = jnp.where(lane < half, pltpu.roll(gain_b, HEAD_DIM - half, 1) * sin_t, 0.0)
            s_hi = jnp.where(lane >= half, pltpu.roll(gain_b, half, 1) * sin_t, 0.0)
            lo = [pltpu.roll(yh, HEAD_DIM - half, 1) for yh in yhs]
            hi = [pltpu.roll(yh, half, 1) for yh in yhs]
            outs = [(yh * cg + a * s_lo + b * s_hi) * inv for yh, a, b, inv in zip(yhs, lo, hi, invs)]
        else:
            (gain,) = extra
            outs = [yh * inv * gain for yh, inv in zip(yhs, invs)]
        return outs[0] if len(outs) == 1 else jnp.concatenate(outs, axis=1)

    return ep


def _ep_gelu(y):
    c = math.sqrt(2.0 / math.pi)
    return 0.5 * y * (1.0 + jnp.tanh(c * (y + 0.044715 * (y * y * y))))


def _ep_gdn_gates(n_heads):
    def ep(y, a_log, dt_bias):
        lane = lax.broadcasted_iota(jnp.int32, y.shape, 1)
        beta = _sigmoid(y)
        g = -jnp.exp(a_log) * _softplus(y + dt_bias)
        return jnp.where(lane < n_heads, beta, g)

    return ep


def _matmul_res_kernel(a_ref, w_ref, r_ref, o_ref):
    y = jnp.dot(a_ref[...].astype(BF16), w_ref[...], preferred_element_type=F32)
    o_ref[...] = r_ref[...] + y


def matmul_residual(a, w, res, *, tm=1024, name="out_proj"):
    M, K = a.shape
    N = w.shape[1]
    tm = _pick(M, tm, SUBLANES)
    tn = _pick(N, 1024 if K <= 2560 else 512, LANES)
    return pl.pallas_call(
        _matmul_res_kernel,
        grid=(M // tm, N // tn),
        in_specs=[
            pl.BlockSpec((tm, K), lambda i, j: (i, 0)),
            pl.BlockSpec((K, tn), lambda i, j: (0, j)),
            pl.BlockSpec((tm, tn), lambda i, j: (i, j)),
        ],
        out_specs=pl.BlockSpec((tm, tn), lambda i, j: (i, j)),
        out_shape=jax.ShapeDtypeStruct((M, N), F32),
        compiler_params=_params("parallel", "arbitrary"),
        name=name,
    )(a, w, res)


def _ffn_kernel(x_ref, g_ref, wg_ref, wu_ref, wd_ref, o_ref, xn_ref, *, sub):
    @pl.when(pl.program_id(1) == 0)
    def _():
        x = x_ref[...]
        xn_ref[...] = _rms(x, g_ref[...]).astype(BF16)
        o_ref[...] = x

    xn = xn_ref[...]
    tf = wg_ref.shape[1]
    hs = []
    for c in range(tf // sub):
        cs = slice(c * sub, (c + 1) * sub)
        gate = jnp.dot(xn, wg_ref[:, cs], preferred_element_type=F32)
        up = jnp.dot(xn, wu_ref[:, cs], preferred_element_type=F32)
        hs.append((_silu(gate) * up).astype(BF16))
    h = hs[0] if len(hs) == 1 else jnp.concatenate(hs, axis=1)
    o_ref[...] += jnp.dot(h, wd_ref[...], preferred_element_type=F32)


def ffn(x, g, w_gu, w_down, *, tm=512, tf=512, sub=256):
    M, D = x.shape
    Fh = w_down.shape[0]
    tm = _pick(M, tm, SUBLANES)
    tf = _pick(Fh, tf, LANES)
    nf = Fh // tf
    return pl.pallas_call(
        functools.partial(_ffn_kernel, sub=min(sub, tf)),
        grid=(M // tm, nf),
        in_specs=[
            pl.BlockSpec((tm, D), lambda i, j: (i, 0)),
            pl.BlockSpec((1, D), lambda i, j: (0, 0)),
            pl.BlockSpec((D, tf), lambda i, j: (0, j)),
            pl.BlockSpec((D, tf), lambda i, j: (0, j + nf)),
            pl.BlockSpec((tf, D), lambda i, j: (j, 0)),
        ],
        out_specs=pl.BlockSpec((tm, D), lambda i, j: (i, 0)),
        out_shape=jax.ShapeDtypeStruct((M, D), F32),
        scratch_shapes=[pltpu.VMEM((tm, D), BF16)],
        compiler_params=_params("parallel", "arbitrary"),
        name="ffn",
    )(x, g.reshape(1, D), w_gu, w_gu, w_down)


SB_STOP = 88.0


def _sb_kernel(q_ref, k_ref, v_ref, o_ref, *, blk, heads, n_q):
    W = 2 * blk
    row = lax.broadcasted_iota(jnp.int32, (blk, W), 0)
    col = lax.broadcasted_iota(jnp.int32, (blk, W), 1)
    r2 = lax.broadcasted_iota(jnp.int32, (W, W), 0)
    c2 = lax.broadcasted_iota(jnp.int32, (W, W), 1)
    suffix_w = jnp.where(r2 >= c2, 1.0, 0.0).astype(BF16)
    rb = lax.broadcasted_iota(jnp.int32, (blk, blk), 0)
    cb = lax.broadcasted_iota(jnp.int32, (blk, blk), 1)
    suffix_b = jnp.where(rb >= cb, 1.0, 0.0).astype(BF16)

    def dot_nt(a, b):
        return lax.dot_general(a, b, (((1,), (1,)), ((), ())), preferred_element_type=F32)

    def suffix_sum(sp, suffix):
        hi = sp.astype(BF16)
        lo = (sp - hi.astype(F32)).astype(BF16)
        return (jnp.dot(hi, suffix, preferred_element_type=F32)
                + jnp.dot(lo, suffix, preferred_element_type=F32))

    def qblock(i, carry):
        q0 = pl.multiple_of(i * blk, blk)
        w0 = pl.multiple_of(jnp.maximum(i - 1, 0) * blk, blk)
        valid = col < row + (q0 - w0)
        hsl = [slice(h * HEAD_DIM, (h + 1) * HEAD_DIM) for h in range(heads)]
        qs = [q_ref[pl.ds(q0, blk), sl] for sl in hsl]
        zs = [dot_nt(q, k_ref[pl.ds(w0, W), sl]) for q, sl in zip(qs, hsl)]
        sps = [jnp.where(valid, _softplus(z), 0.0) for z in zs]
        cums = [suffix_sum(sp, suffix_w) for sp in sps]
        ws = [jnp.exp(jnp.where(valid, z - cum, NEG_BIG)).astype(BF16) for z, cum in zip(zs, cums)]
        accs = [jnp.dot(w, v_ref[pl.ds(w0, W), sl], preferred_element_type=F32)
                for w, sl in zip(ws, hsl)]
        runs = [cum[:, 0:1] for cum in cums]
        def cond(c):
            least = functools.reduce(jnp.minimum, c[2])
            return (c[0] >= 0) & (jnp.min(least) < SB_STOP)

        def body(c):
            j, acc, run = c
            s0 = pl.multiple_of(j * blk, blk)
            z = [dot_nt(q, k_ref[pl.ds(s0, blk), sl]) for q, sl in zip(qs, hsl)]
            cum = [suffix_sum(_softplus(zh), suffix_b) for zh in z]
            w = [jnp.exp(zh - ch - rh).astype(BF16) for zh, ch, rh in zip(z, cum, run)]
            acc = tuple(a + jnp.dot(wh, v_ref[pl.ds(s0, blk), sl], preferred_element_type=F32)
                        for a, wh, sl in zip(acc, w, hsl))
            return j - 1, acc, tuple(rh + ch[:, 0:1] for rh, ch in zip(run, cum))

        _, accs, _ = lax.while_loop(cond, body, (i - 2, tuple(accs), tuple(runs)))
        for acc, sl in zip(accs, hsl):
            o_ref[pl.ds(q0, blk), sl] = acc.astype(o_ref.dtype)
        return carry

    lax.fori_loop(0, n_q, qblock, 0)


def sb_attention(qkv, *, blk=128, heads=4):
    B, T, HD3 = qkv.shape
    HD = HD3 // 3
    H = HD // HEAD_DIM
    assert T % blk == 0 and T >= 2 * blk and H % heads == 0
    nh = H // heads
    seq = lambda off: pl.BlockSpec((None, T, heads * HEAD_DIM), lambda b, h: (b, 0, h + off))
    return pl.pallas_call(
        functools.partial(_sb_kernel, blk=blk, heads=heads, n_q=T // blk),
        grid=(B, nh),
        in_specs=[seq(0), seq(nh), seq(2 * nh)],
        out_specs=seq(0),
        out_shape=jax.ShapeDtypeStruct((B, T, HD), BF16),
        compiler_params=_params("parallel", "parallel"),
        name="sb_attention",
    )(qkv, qkv, qkv)


def _bdot(a, b):
    return jnp.dot(a.astype(BF16), b.astype(BF16), preferred_element_type=F32)


def _bdot_nt(a, b):
    return lax.dot_general(a.astype(BF16), b.astype(BF16), (((1,), (1,)), ((), ())),
                           preferred_element_type=F32)


def _bdot_tn(a, b):
    return lax.dot_general(a.astype(BF16), b.astype(BF16), (((0,), (0,)), ((), ())),
                           preferred_element_type=F32)


def _gdn_kernel(xq_ref, xk_ref, xv_ref, z_ref, cwq_ref, cwk_ref, cwv_ref, g_ref, beta_ref, gain_ref,
                o_ref, q_ref, k_ref, v_ref, hq_ref, hk_ref, hv_ref, buf_ref,
                u_ref, wq_ref, kd_ref, attn_ref, gl_ref, out_ref, state_ref,
                *, chunk, kheads, rep, group, taps):
    C = chunk

    @pl.when(pl.program_id(2) == 0)
    def _():
        state_ref[...] = jnp.zeros_like(state_ref)
        hq_ref[...] = jnp.zeros_like(hq_ref)
        hk_ref[...] = jnp.zeros_like(hk_ref)
        hv_ref[...] = jnp.zeros_like(hv_ref)

    def conv_silu(x_ref, halo_ref, w_ref, dst_ref, l2_scale):
        n, width = x_ref.shape
        buf_ref[0:SUBLANES, 0:width] = halo_ref[...]
        buf_ref[SUBLANES:SUBLANES + n, 0:width] = x_ref[...]
        halo_ref[...] = x_ref[n - SUBLANES:n, :]
        w = w_ref[...]
        y = w[taps - 1:taps, :] * x_ref[...]
        for kk in range(taps - 1):
            shift = taps - 1 - kk
            y = y + w[kk:kk + 1, :] * buf_ref[SUBLANES - shift:SUBLANES - shift + n, 0:width]
        y = _silu(y)
        for h in range(width // HEAD_DIM):
            sl = slice(h * HEAD_DIM, (h + 1) * HEAD_DIM)
            yh = y[:, sl]
            if l2_scale is not None:
                ss = jnp.sum(yh * yh, axis=-1, keepdims=True)
                yh = yh * (lax.rsqrt(ss + NORM_EPS) * l2_scale)
            dst_ref[:, sl] = yh

    conv_silu(xq_ref, hq_ref, cwq_ref, q_ref, HEAD_DIM ** -0.5)
    conv_silu(xk_ref, hk_ref, cwk_ref, k_ref, 1.0)
    conv_silu(xv_ref, hv_ref, cwv_ref, v_ref, None)

    ri = lax.broadcasted_iota(jnp.int32, (C, C), 0)
    ci = lax.broadcasted_iota(jnp.int32, (C, C), 1)
    eye = ri == ci
    lower = ci <= ri
    strict = ci < ri
    gain = gain_ref[...]
    n_double = max(1, int(math.ceil(math.log2(C))) - 1)
    nv = kheads * rep
    heads = [slice(r * HEAD_DIM, (r + 1) * HEAD_DIM) for r in range(nv)]
    G = group
    rows = G * C
    bmm = functools.partial(jnp.einsum, "gij,gjk->gik", preferred_element_type=F32)
    bmm_nt = functools.partial(jnp.einsum, "gid,gjd->gij", preferred_element_type=F32)

    def prepare(khs):
        rs = [r for kh in khs for r in range(kh * rep, (kh + 1) * rep)]
        every = lambda f: [f(i) for i in range(len(rs))]
        qs = {kh: q_ref[:, heads[kh]].reshape(G, C, HEAD_DIM) for kh in khs}
        ks = {kh: k_ref[:, heads[kh]].reshape(G, C, HEAD_DIM) for kh in khs}
        k16s = {kh: ks[kh].astype(BF16) for kh in khs}
        qks = {kh: bmm_nt(qs[kh].astype(BF16), k16s[kh]) for kh in khs}
        q = every(lambda i: qs[rs[i] // rep])
        k = every(lambda i: ks[rs[i] // rep])
        k16 = every(lambda i: k16s[rs[i] // rep])
        g_row = every(lambda i: g_ref[rs[i]])
        beta_row = every(lambda i: beta_ref[rs[i]])
        g_col = every(lambda i: jnp.sum(jnp.where(eye, g_row[i], 0.0), axis=2, keepdims=True))
        beta_col = every(lambda i: jnp.sum(jnp.where(eye, beta_row[i], 0.0), axis=2, keepdims=True))
        gc_col = every(lambda i: jnp.sum(jnp.where(lower, g_row[i], 0.0), axis=2, keepdims=True))
        gc_row = every(lambda i: jnp.sum(jnp.where(ri <= ci, g_col[i], 0.0), axis=1, keepdims=True))
        g_last = every(lambda i: jnp.sum(g_row[i], axis=2, keepdims=True))
        decay = every(lambda i: jnp.where(
            lower, jnp.exp(jnp.where(lower, gc_col[i] - gc_row[i], 0.0)), 0.0))
        eg = every(lambda i: jnp.exp(gc_col[i]))
        kb = every(lambda i: k[i] * beta_col[i])
        a_mat = every(lambda i: jnp.where(strict, bmm_nt(kb[i].astype(BF16), k16[i]) * decay[i], 0.0))
        t_mat = every(lambda i: jnp.where(eye, 1.0, 0.0) - a_mat[i])
        p = a_mat
        for _ in range(n_double):
            p16 = every(lambda i: p[i].astype(BF16))
            p = every(lambda i: bmm(p16[i], p16[i]))
            t_mat = every(lambda i: t_mat[i] + bmm(t_mat[i].astype(BF16), p[i].astype(BF16)))
        rhs = every(lambda i: jnp.concatenate(
            [v_ref[:, heads[rs[i]]].reshape(G, C, HEAD_DIM) * beta_col[i], kb[i] * eg[i]],
            axis=2).astype(BF16))
        uw = every(lambda i: bmm(t_mat[i].astype(BF16), rhs[i]))
        for i, r in enumerate(rs):
            u_ref[r] = uw[i][:, :, :HEAD_DIM].reshape(rows, HEAD_DIM)
            wq_ref[r] = jnp.concatenate([uw[i][:, :, HEAD_DIM:], q[i] * eg[i]], axis=1).astype(
                BF16).reshape(2 * rows, HEAD_DIM)
            kd_ref[r] = (k[i] * jnp.exp(g_last[i] - gc_col[i])).astype(BF16).reshape(rows, HEAD_DIM)
            attn_ref[r] = (qks[r // rep] * decay[i]).astype(BF16).reshape(rows, C)
            gl_ref[r] = jnp.broadcast_to(g_last[i], (G, 1, HEAD_DIM))

    def recur(nl, states):
        ls = pl.multiple_of(nl * C, C)
        ls2 = pl.multiple_of(nl * 2 * C, 2 * C)
        hs = range(nv)
        s16 = [states[r].astype(BF16) for r in hs]
        ws = [jnp.dot(wq_ref[r, pl.ds(ls2, 2 * C), :], s16[r], preferred_element_type=F32) for r in hs]
        v_new = [(u_ref[r, pl.ds(ls, C), :] - ws[r][:C]).astype(BF16) for r in hs]
        upd = [lax.dot_general(kd_ref[r, pl.ds(ls, C), :], v_new[r], (((0,), (0,)), ((), ())),
                               preferred_element_type=F32) for r in hs]
        new = tuple(states[r] * jnp.exp(gl_ref[r, nl]) + upd[r] for r in hs)
        for r in hs:
            out_ref[r, pl.ds(ls, C), :] = ws[r][C:] + jnp.dot(
                attn_ref[r, pl.ds(ls, C), :], v_new[r], preferred_element_type=F32)
        return new

    for k0 in range(0, kheads, 2):
        prepare(list(range(k0, min(k0 + 2, kheads))))
    states = lax.fori_loop(0, group, recur, tuple(state_ref[r] for r in range(nv)))
    for r in range(nv):
        state_ref[r] = states[r]
        o_ref[:, heads[r]] = (_rms(out_ref[r], gain) * _silu(z_ref[:, heads[r]])).astype(o_ref.dtype)


def gdn_delta_rule(proj, conv_w, KD, VD, g, beta, gain, *, chunk=GDN_CHUNK, group=8, kheads=4):
    B, T, _ = proj.shape
    taps = conv_w.shape[0]
    Hk, Hv = KD // HEAD_DIM, VD // HEAD_DIM
    rep = Hv // Hk
    N = T // chunk
    group = _pick(N, group, 1)
    kheads = _pick(Hk, kheads, 1)
    nv = kheads * rep
    kw, vw = kheads * HEAD_DIM, nv * HEAD_DIM
    assert KD % kw == 0 and (2 * KD) % vw == 0 and N % group == 0
    koff, voff, zoff = KD // kw, 2 * KD // vw, (2 * KD + VD) // vw
    rows = group * chunk
    assert rows % SUBLANES == 0
    tile = lambda width, f: pl.BlockSpec((None, rows, width), f)
    tapw = lambda width, f: pl.BlockSpec((taps, width), f)
    gate = pl.BlockSpec((None, nv, group, 1, chunk), lambda b, h, t: (b, h, t, 0, 0))
    return pl.pallas_call(
        functools.partial(_gdn_kernel, chunk=chunk, kheads=kheads, rep=rep, group=group, taps=taps),
        grid=(B, Hk // kheads, N // group),
        in_specs=[
            tile(kw, lambda b, h, t: (b, t, h)),
            tile(kw, lambda b, h, t: (b, t, h + koff)),
            tile(vw, lambda b, h, t: (b, t, h + voff)),
            tile(vw, lambda b, h, t: (b, t, h + zoff)),
            tapw(kw, lambda b, h, t: (0, h)),
            tapw(kw, lambda b, h, t: (0, h + koff)),
            tapw(vw, lambda b, h, t: (0, h + voff)),
            gate,
            gate,
            pl.BlockSpec((1, HEAD_DIM), lambda b, h, t: (0, 0)),
        ],
        out_specs=tile(vw, lambda b, h, t: (b, t, h)),
        out_shape=jax.ShapeDtypeStruct((B, T, VD), BF16),
        scratch_shapes=[
            pltpu.VMEM((rows, kw), F32),
            pltpu.VMEM((rows, kw), F32),
            pltpu.VMEM((rows, vw), F32),
            pltpu.VMEM((SUBLANES, kw), F32),
            pltpu.VMEM((SUBLANES, kw), F32),
            pltpu.VMEM((SUBLANES, vw), F32),
            pltpu.VMEM((rows + SUBLANES, vw), F32),
            pltpu.VMEM((nv, rows, HEAD_DIM), F32),
            pltpu.VMEM((nv, 2 * rows, HEAD_DIM), BF16),
            pltpu.VMEM((nv, rows, HEAD_DIM), BF16),
            pltpu.VMEM((nv, rows, chunk), BF16),
            pltpu.VMEM((nv, group, 1, HEAD_DIM), F32),
            pltpu.VMEM((nv, rows, HEAD_DIM), F32),
            pltpu.VMEM((nv, HEAD_DIM, HEAD_DIM), F32),
        ],
        compiler_params=_params("parallel", "parallel", "arbitrary"),
        name="gdn_delta_rule",
    )(proj, proj, proj, proj, conv_w, conv_w, conv_w, g, beta, gain)


def _rope_table_kernel(pos_ref, freq_ref, sign_ref, cos_ref, sin_ref):
    ang = pos_ref[...].astype(F32) * freq_ref[...]
    cos_ref[...] = jnp.cos(ang)
    sin_ref[...] = jnp.sin(ang) * sign_ref[...]


def rope_tables(pos, *, tm=1024):
    M = pos.shape[0]
    tm = _pick(M, tm, SUBLANES)
    half = ROPE_DIM // 2
    lane = jnp.arange(HEAD_DIM)
    inv_freq = ROPE_THETA ** (-(lane % half).astype(F32) / half)
    freq = jnp.where(lane < ROPE_DIM, inv_freq, 0.0).reshape(1, HEAD_DIM).astype(F32)
    sign = jnp.where(lane < half, -1.0, 1.0).reshape(1, HEAD_DIM).astype(F32)
    spec = pl.BlockSpec((tm, HEAD_DIM), lambda i: (i, 0))
    const = pl.BlockSpec((1, HEAD_DIM), lambda i: (0, 0))
    return pl.pallas_call(
        _rope_table_kernel,
        grid=(M // tm,),
        in_specs=[pl.BlockSpec((tm, 1), lambda i: (i, 0)), const, const],
        out_specs=[spec, spec],
        out_shape=[jax.ShapeDtypeStruct((M, HEAD_DIM), F32)] * 2,
        compiler_params=_params("parallel"),
        name="rope_tables",
    )(pos, freq, sign)


def _band_kernel(q_ref, kc_ref, kp_ref, vc_ref, vp_ref, o_ref, lse_ref, *, blk, span, heads, d, batch):
    n = pl.program_id(2)
    qi = lax.broadcasted_iota(jnp.int32, (blk, 2 * blk), 0)
    kj = lax.broadcasted_iota(jnp.int32, (blk, 2 * blk), 1)
    steps = blk + qi - kj
    first_key = jnp.where(n > 0, 0, blk)
    valid = (steps >= 0) & (steps <= span) & (kj >= first_key)
    streams = [(r, h) for r in range(d) for h in range(heads)]

    def rows(ref, r, h):
        sl = slice(h * HEAD_DIM, (h + 1) * HEAD_DIM)
        return ref[:, sl] if d == 1 else ref[pl.ds(r, blk, stride=d), sl]

    for b0 in range(0, len(streams), batch):
        part = streams[b0:b0 + batch]
        s_list = []
        for r, h in part:
            q = rows(q_ref, r, h).astype(BF16)
            kw = jnp.concatenate([rows(kp_ref, r, h), rows(kc_ref, r, h)], axis=0).astype(BF16)
            s = lax.dot_general(q, kw, (((1,), (1,)), ((), ())), preferred_element_type=F32)
            s_list.append(jnp.where(valid, s, NEG_BIG))
        m_list = [jnp.max(s, axis=-1, keepdims=True) for s in s_list]
        p_list = [jnp.where(valid, jnp.exp(s - m), 0.0) for s, m in zip(s_list, m_list)]
        den_list = [jnp.sum(p, axis=-1, keepdims=True) for p in p_list]
        for (r, h), p, m, den in zip(part, p_list, m_list, den_list):
            vw = jnp.concatenate([rows(vp_ref, r, h), rows(vc_ref, r, h)], axis=0).astype(BF16)
            o = jnp.dot((p / den).astype(BF16), vw, preferred_element_type=F32)
            lse = jnp.broadcast_to(m + jnp.log(den), (blk, HEAD_DIM))
            sl = slice(h * HEAD_DIM, (h + 1) * HEAD_DIM)
            if d == 1:
                o_ref[:, sl] = o
                lse_ref[:, sl] = lse
            else:
                o_ref[pl.ds(r, blk, stride=d), sl] = o
                lse_ref[pl.ds(r, blk, stride=d), sl] = lse


def band_attention(q, k, v, group, window, dilation, *, blk=DSW_BLOCK, group_heads=DSW_HEADS_PER_GROUP):
    B, T, HD = q.shape
    d = dilation
    span = window // d
    unit = d * blk
    assert span <= blk and T % unit == 0
    heads = group_heads if d == 1 else 1
    assert group_heads % heads == 0
    hw = heads * HEAD_DIM
    col0 = group * group_heads // heads
    cur = pl.BlockSpec((None, unit, hw), lambda b, c, n: (b, n, col0 + c))
    prev = pl.BlockSpec((None, unit, hw), lambda b, c, n: (b, jnp.maximum(n - 1, 0), col0 + c))
    out = pl.BlockSpec((None, unit, hw), lambda b, c, n: (b, n, c))
    return pl.pallas_call(
        functools.partial(_band_kernel, blk=blk, span=span, heads=heads, d=d, batch=6 if d == 1 else 4),
        grid=(B, group_heads // heads, T // unit),
        in_specs=[cur, cur, prev, cur, prev],
        out_specs=[out, out],
        out_shape=[jax.ShapeDtypeStruct((B, T, group_heads * HEAD_DIM), F32)] * 2,
        compiler_params=_params("parallel", "parallel", "arbitrary"),
        name=f"band_attention_d{d}",
    )(q, k, k, v, v)


def _combine_kernel(*refs, n):
    o_refs, l_refs, out_ref = refs[:n], refs[n:2 * n], refs[2 * n]
    ls = [r[...] for r in l_refs]
    m = functools.reduce(jnp.maximum, ls)
    es = [jnp.exp(l - m) for l in ls]
    inv = 1.0 / functools.reduce(lambda a, b: a + b, es)
    w = o_refs[0].shape[1]
    for gi in range(n):
        out_ref[:, gi * w:(gi + 1) * w] = (o_refs[gi][...] * (es[gi] * inv)).astype(out_ref.dtype)


def combine_groups(os_, lses, *, tm=512):
    n = len(os_)
    M, w = os_[0].shape
    tm = _pick(M, tm, SUBLANES)
    spec = pl.BlockSpec((tm, w), lambda i: (i, 0))
    return pl.pallas_call(
        functools.partial(_combine_kernel, n=n),
        grid=(M // tm,),
        in_specs=[spec] * (2 * n),
        out_specs=pl.BlockSpec((tm, n * w), lambda i: (i, 0)),
        out_shape=jax.ShapeDtypeStruct((M, n * w), BF16),
        compiler_params=_params("parallel"),
        name="combine_groups",
    )(*os_, *lses)


def _lru_kernel(x_ref, gate_ref, cw_ref, cb_ref, wa_ref, wx_ref, ba_ref, bx_ref, lam_ref, o_ref,
                a_ref, u_ref, h_ref, xc_ref, halo_ref, buf_ref, *, tt, nblk, taps):
    @pl.when(pl.program_id(1) == 0)
    def _():
        h_ref[...] = jnp.zeros_like(h_ref)
        halo_ref[...] = jnp.zeros_like(halo_ref)

    buf_ref[0:SUBLANES, :] = halo_ref[...]
    buf_ref[SUBLANES:SUBLANES + tt, :] = x_ref[...]
    halo_ref[...] = x_ref[tt - SUBLANES:tt, :]
    cw = cw_ref[...]
    xc = cb_ref[...] + cw[taps - 1:taps, :] * x_ref[...]
    for kk in range(taps - 1):
        shift = taps - 1 - kk
        xc = xc + cw[kk:kk + 1, :] * buf_ref[SUBLANES - shift:SUBLANES - shift + tt, :]
    xc_ref[...] = xc

    for nb in range(nblk):
        sl = slice(nb * LRU_BLOCK_DIM, (nb + 1) * LRU_BLOCK_DIM)
        xb = xc_ref[:, sl]
        xb16 = xb.astype(BF16)
        r = _sigmoid(jnp.dot(xb16, wa_ref[nb], preferred_element_type=F32) + ba_ref[:, sl])
        ig = _sigmoid(jnp.dot(xb16, wx_ref[nb], preferred_element_type=F32) + bx_ref[:, sl])
        log_a = (-LRU_C) * r * _softplus(-lam_ref[:, sl])
        a = jnp.exp(log_a)
        a_ref[:, sl] = a
        u_ref[:, sl] = jnp.sqrt(1.0 - a * a) * (ig * xb)

    def group(gi, h):
        base = pl.multiple_of(gi * SUBLANES, SUBLANES)
        for r8 in range(SUBLANES):
            h = a_ref[pl.ds(base + r8, 1), :] * h + u_ref[pl.ds(base + r8, 1), :]
            u_ref[pl.ds(base + r8, 1), :] = h
        return h

    h_ref[...] = lax.fori_loop(0, tt // SUBLANES, group, h_ref[...])
    o_ref[...] = (u_ref[...] * gate_ref[...].astype(F32)).astype(o_ref.dtype)


def lru_scan(proj, conv_w, conv_b, w_a, w_x, b_a, b_x, lam, *, tt=256):
    B, T, W2 = proj.shape
    W = W2 // 2
    nblk = W // LRU_BLOCK_DIM
    taps = conv_w.shape[0]
    tt = _pick(T, tt, SUBLANES)
    tile = pl.BlockSpec((None, tt, W), lambda b, i: (b, i, 0))
    xtile = pl.BlockSpec((None, tt, W), lambda b, i: (b, i, 1))
    wspec = pl.BlockSpec((nblk, LRU_BLOCK_DIM, LRU_BLOCK_DIM), lambda b, i: (0, 0, 0))
    vec = pl.BlockSpec((1, W), lambda b, i: (0, 0))
    return pl.pallas_call(
        functools.partial(_lru_kernel, tt=tt, nblk=nblk, taps=taps),
        grid=(B, T // tt),
        in_specs=[xtile, tile, pl.BlockSpec((taps, W), lambda b, i: (0, 0)), vec,
                  wspec, wspec, vec, vec, vec],
        out_specs=tile,
        out_shape=jax.ShapeDtypeStruct((B, T, W), BF16),
        scratch_shapes=[
            pltpu.VMEM((tt, W), F32),
            pltpu.VMEM((tt, W), F32),
            pltpu.VMEM((1, W), F32),
            pltpu.VMEM((tt, W), F32),
            pltpu.VMEM((SUBLANES, W), F32),
            pltpu.VMEM((tt + SUBLANES, W), F32),
        ],
        compiler_params=_params("parallel", "arbitrary"),
        name="lru_scan",
    )(proj, proj, conv_w, conv_b, w_a, w_x, b_a, b_x, lam)


def _row(v):
    return v.reshape(1, -1).astype(F32)


def sb_mixer(x2, B, T, norm_g, w_in, q_norm, k_norm, w_out):
    hd = w_in.shape[1] // 3
    w_in = w_in.astype(BF16)
    scale = HEAD_DIM ** -0.5
    tn = _pick(hd, 1024, LANES)
    nt = hd // tn
    q_ep, k_ep = _ep_headnorm(scale, False), _ep_headnorm(1.0, False)
    qkv = norm_matmul(x2, norm_g, w_in, tn=tn, out_dtype=BF16, name="sb_proj",
                      const_extras=(_row(q_norm), _row(k_norm)),
                      epilogue=[(0, nt, lambda y, gq, gk: q_ep(y, gq)),
                                (nt, 2 * nt, lambda y, gq, gk: k_ep(y, gk)),
                                (2 * nt, 3 * nt, None)])
    o = sb_attention(qkv.reshape(B, T, 3 * hd))
    return matmul_residual(o.reshape(B * T, hd), w_out.astype(BF16), x2, name="sb_out")


def gdn_mixer(x2, B, T, norm_g, w_in, conv_w, a_log, dt_bias, o_norm, w_out):
    M = B * T
    Hv = a_log.shape[0]
    vd = Hv * HEAD_DIM
    kd = (conv_w.shape[1] - vd) // 2
    main = 2 * kd + 2 * vd
    proj = norm_matmul(x2, norm_g, w_in.astype(BF16), col0=0, ncols=main, name="gdn_proj")
    w_ba = jnp.pad(w_in[:, main:], ((0, 0), (0, LANES - 2 * Hv))).astype(BF16)
    pad = lambda p: jnp.pad(p.astype(F32), (Hv, LANES - 2 * Hv)).reshape(1, LANES)
    gates = norm_matmul(x2, norm_g, w_ba, epilogue=_ep_gdn_gates(Hv),
                        const_extras=(pad(a_log), pad(dt_bias)), name="gdn_gates")
    N = T // GDN_CHUNK
    per_head = lambda a: a.reshape(B, T, Hv).transpose(0, 2, 1).reshape(B, Hv, N, 1, GDN_CHUNK)
    beta = per_head(gates[:, :Hv])
    g = per_head(gates[:, Hv:2 * Hv])
    o = gdn_delta_rule(proj.reshape(B, T, main), conv_w, kd, vd, g, beta, _row(o_norm))
    return matmul_residual(o.reshape(M, vd), w_out.astype(BF16), x2, name="gdn_out")


def dsw_mixer(x2, B, T, norm_g, positions, w_in, q_norm, k_norm, w_out):
    M = B * T
    hd = w_in.shape[1] // 3
    w_in = w_in.astype(BF16)
    cos_t, sin_t = rope_tables(positions.reshape(M, 1).astype(jnp.int32))
    scale = HEAD_DIM ** -0.5
    tn = 3 * HEAD_DIM * 2
    q = norm_matmul(x2, norm_g, w_in, col0=0, ncols=hd, epilogue=_ep_headnorm(scale, True),
                    row_extras=(cos_t, sin_t), const_extras=(_row(q_norm),), out_dtype=F32, tn=tn,
                    pipelined=True, name="dsw_proj_q")
    k = norm_matmul(x2, norm_g, w_in, col0=hd, ncols=hd, epilogue=_ep_headnorm(1.0, True),
                    row_extras=(cos_t, sin_t), const_extras=(_row(k_norm),), out_dtype=F32, tn=tn,
                    pipelined=True, name="dsw_proj_k")
    v = norm_matmul(x2, norm_g, w_in, col0=2 * hd, ncols=hd, out_dtype=F32, tn=tn, name="dsw_proj_v")
    shp = (B, T, hd)
    q, k, v = q.reshape(shp), k.reshape(shp), v.reshape(shp)
    os_, lses = [], []
    for gi, (window, dilation) in enumerate(DSW_GROUPS):
        o_g, lse_g = band_attention(q, k, v, gi, window, dilation)
        os_.append(o_g.reshape(M, -1))
        lses.append(lse_g.reshape(M, -1))
    o = combine_groups(os_, lses)
    return matmul_residual(o, w_out.astype(BF16), x2, name="dsw_out")


def lru_mixer(x2, B, T, norm_g, w_in, conv_w, conv_b, w_a, b_a, w_x, b_x, lam, w_out):
    W = w_in.shape[1] // 2
    w_in = w_in.astype(BF16)
    tn = _pick(W, 1024, LANES)
    nt = W // tn
    proj = norm_matmul(x2, norm_g, w_in, tn=tn, name="lru_proj",
                       epilogue=[(0, nt, _ep_gelu), (nt, 2 * nt, None)])
    y = lru_scan(proj.reshape(B, T, 2 * W), conv_w, _row(conv_b),
                 w_a.astype(BF16), w_x.astype(BF16), _row(b_a), _row(b_x), _row(lam))
    return matmul_residual(y.reshape(B * T, W), w_out.astype(BF16), x2, name="lru_out")


def kernel(x, positions, mix_norm, ffn_norm, ffn_w_gu, ffn_w_down, sb_w_in, sb_q_norm, sb_k_norm, sb_w_out, gdn_w_in, gdn_conv_w, gdn_a_log, gdn_dt_bias, gdn_o_norm, gdn_w_out, dsw_w_in, dsw_q_norm, dsw_k_norm, dsw_w_out, lru_w_in, lru_conv_w, lru_conv_b, lru_w_a, lru_b_a, lru_w_x, lru_b_x, lru_lambda, lru_w_out):
    B, T, D = x.shape
    depth = mix_norm.shape[0]
    x2 = x.reshape(B * T, D)
    for i in range(depth):
        kind, j = i % 4, i // 4
        if kind == 0:
            x2 = sb_mixer(x2, B, T, mix_norm[i], sb_w_in[j], sb_q_norm[j], sb_k_norm[j], sb_w_out[j])
        elif kind == 1:
            x2 = gdn_mixer(x2, B, T, mix_norm[i], gdn_w_in[j], gdn_conv_w[j], gdn_a_log[j],
                           gdn_dt_bias[j], gdn_o_norm[j], gdn_w_out[j])
        elif kind == 2:
            x2 = dsw_mixer(x2, B, T, mix_norm[i], positions, dsw_w_in[j], dsw_q_norm[j],
                           dsw_k_norm[j], dsw_w_out[j])
        else:
            x2 = lru_mixer(x2, B, T, mix_norm[i], lru_w_in[j], lru_conv_w[j], lru_conv_b[j],
                           lru_w_a[j], lru_b_a[j], lru_w_x[j], lru_b_x[j], lru_lambda[j], lru_w_out[j])
        x2 = ffn(x2, ffn_norm[i], ffn_w_gu[i].astype(BF16), ffn_w_down[i].astype(BF16))
    return x2.reshape(B, T, D)
```

```python
import functools
import math

import jax
import jax.numpy as jnp
from jax import lax
from jax.experimental import pallas as pl
from jax.experimental.pallas import tpu as pltpu

F32 = jnp.float32
BF16 = jnp.bfloat16

NORM_EPS = 1e-6
HEAD_DIM = 128
LANES = 128
SUBLANES = 8
VMEM_LIMIT_BYTES = 56 * 1024 * 1024

GDN_CONV = 4
GDN_CHUNK = 64
DSW_GROUPS = ((128, 1), (512, 4), (2048, 16))
DSW_HEADS_PER_GROUP = 6
DSW_BLOCK = 128
ROPE_DIM = HEAD_DIM // 4
ROPE_THETA = 500000.0
LRU_BLOCK_DIM = 256
LRU_C = 8.0
NEG_BIG = -1e30


def _pick(dim, pref, align):
    if dim <= pref:
        return dim
    t = (pref // align) * align
    while t >= align:
        if dim % t == 0:
            return t
        t -= align
    return dim


def _params(*sem):
    return pltpu.CompilerParams(dimension_semantics=sem, vmem_limit_bytes=VMEM_LIMIT_BYTES)


def _softplus(x):
    return jnp.maximum(x, 0.0) + jnp.log(1.0 + jnp.exp(-jnp.abs(x)))


def _sigmoid(x):
    return 1.0 / (1.0 + jnp.exp(-x))


def _silu(x):
    return x * _sigmoid(x)


def _rms(x, g):
    ms = jnp.mean(x * x, axis=-1, keepdims=True)
    return x * lax.rsqrt(ms + NORM_EPS) * g


def _norm_matmul_kernel(*refs, epilogue, n_extra):
    x_ref, g_ref, w_ref = refs[:3]
    extra = refs[3:3 + n_extra]
    o_ref, xn_ref = refs[3 + n_extra], refs[4 + n_extra]
    j = pl.program_id(1)

    @pl.when(j == 0)
    def _():
        xn_ref[...] = _rms(x_ref[...], g_ref[...]).astype(BF16)

    y = jnp.dot(xn_ref[...], w_ref[...], preferred_element_type=F32)
    ex = [e[...] for e in extra]
    if isinstance(epilogue, list):
        for lo, hi, fn in epilogue:
            @pl.when((j >= lo) & (j < hi))
            def _(fn=fn):
                o_ref[...] = (y if fn is None else fn(y, *ex)).astype(o_ref.dtype)
        return
    if epilogue is not None:
        y = epilogue(y, *ex)
    o_ref[...] = y.astype(o_ref.dtype)


def norm_matmul(x, g, w, *, col0=0, ncols=None, epilogue=None, row_extras=(), const_extras=(),
                out_dtype=F32, tm=1024, tn=1024, pipelined=False, name="norm_matmul"):
    M, K = x.shape
    ncols = w.shape[1] - col0 if ncols is None else ncols
    tm = _pick(M, tm, SUBLANES)
    tn = _pick(ncols, tn, LANES)
    assert col0 % tn == 0 and ncols % tn == 0 and M % tm == 0
    off = col0 // tn
    in_specs = [
        pl.BlockSpec((tm, K), lambda i, j: (i, 0)),
        pl.BlockSpec((1, K), lambda i, j: (0, 0)),
        pl.BlockSpec((K, tn), lambda i, j: (0, j + off)),
    ]
    args = [x, g.reshape(1, K), w]
    for e in row_extras:
        in_specs.append(pl.BlockSpec((tm, e.shape[1]), lambda i, j: (i, 0)))
        args.append(e)
    for e in const_extras:
        in_specs.append(pl.BlockSpec(e.shape, lambda i, j: (0, 0)))
        args.append(e)
    n_extra = len(row_extras) + len(const_extras)
    if pipelined:
        return _norm_matmul_pipelined(x, g, w, off, ncols, epilogue, row_extras, const_extras,
                                      out_dtype, tm, tn, name)
    return pl.pallas_call(
        functools.partial(_norm_matmul_kernel, epilogue=epilogue, n_extra=n_extra),
        grid=(M // tm, ncols // tn),
        in_specs=in_specs,
        out_specs=pl.BlockSpec((tm, tn), lambda i, j: (i, j)),
        out_shape=jax.ShapeDtypeStruct((M, ncols), out_dtype),
        scratch_shapes=[pltpu.VMEM((tm, K), BF16)],
        compiler_params=_params("parallel", "arbitrary"),
        name=name,
    )(*args)


def _norm_matmul_pipe_kernel(*refs, epilogue, n_extra, ncol, ntiles):
    x_ref, g_ref, w_ref = refs[:3]
    extra = refs[3:3 + n_extra]
    o_ref, xn_ref, y_ref = refs[3 + n_extra:6 + n_extra]
    t = pl.program_id(0)

    @pl.when(t == 0)
    def _():
        y_ref[1] = jnp.zeros(y_ref.shape[1:], F32)

    @pl.when((t % ncol == 0) & (t < ntiles))
    def _():
        xn_ref[...] = _rms(x_ref[...], g_ref[...]).astype(BF16)

    slot = t % 2
    o_ref[...] = epilogue(y_ref[1 - slot], *[e[...] for e in extra]).astype(o_ref.dtype)
    y_ref[slot] = jnp.dot(xn_ref[...], w_ref[...], preferred_element_type=F32)


def _norm_matmul_pipelined(x, g, w, off, ncols, epilogue, row_extras, const_extras, out_dtype,
                           tm, tn, name):
    M, K = x.shape
    ncol = ncols // tn
    ntiles = (M // tm) * ncol
    cur = lambda t: jnp.minimum(t, ntiles - 1)
    prv = lambda t: jnp.maximum(t - 1, 0)
    in_specs = [
        pl.BlockSpec((tm, K), lambda t: (cur(t) // ncol, 0)),
        pl.BlockSpec((1, K), lambda t: (0, 0)),
        pl.BlockSpec((K, tn), lambda t: (0, cur(t) % ncol + off)),
    ]
    args = [x, g.reshape(1, K), w]
    for e in row_extras:
        in_specs.append(pl.BlockSpec((tm, e.shape[1]), lambda t: (prv(t) // ncol, 0)))
        args.append(e)
    for e in const_extras:
        in_specs.append(pl.BlockSpec(e.shape, lambda t: (0, 0)))
        args.append(e)
    n_extra = len(row_extras) + len(const_extras)
    return pl.pallas_call(
        functools.partial(_norm_matmul_pipe_kernel, epilogue=epilogue, n_extra=n_extra,
                          ncol=ncol, ntiles=ntiles),
        grid=(ntiles + 1,),
        in_specs=in_specs,
        out_specs=pl.BlockSpec((tm, tn), lambda t: (prv(t) // ncol, prv(t) % ncol)),
        out_shape=jax.ShapeDtypeStruct((M, ncols), out_dtype),
        scratch_shapes=[pltpu.VMEM((tm, K), BF16), pltpu.VMEM((2, tm, tn), F32)],
        compiler_params=_params("arbitrary"),
        name=name,
    )(*args)


def _ep_headnorm(scale, rope):
    half = ROPE_DIM // 2

    def ep(y, *extra):
        yhs = [y[:, h * HEAD_DIM:(h + 1) * HEAD_DIM] for h in range(y.shape[1] // HEAD_DIM)]
        invs = [lax.rsqrt(jnp.mean(yh * yh, axis=-1, keepdims=True) + NORM_EPS) * scale for yh in yhs]
        if rope:
            cos_t, sin_t, gain = extra
            lane = lax.broadcasted_iota(jnp.int32, cos_t.shape, 1)
            gain_b = jnp.broadcast_to(gain, cos_t.shape)
            cg = cos_t * gain
            s_lo = jnp.where(lane < half, pltpu.roll(gain_b, HEAD_DIM - half, 1) * sin_t, 0.0)
            s_hi = jnp.where(lane >= half, pltpu.roll(gain_b, half, 1) * sin_t, 0.0)
            lo = [pltpu.roll(yh, HEAD_DIM - half, 1) for yh in yhs]
            hi = [pltpu.roll(yh, half, 1) for yh in yhs]
            outs = [(yh * cg + a * s_lo + b * s_hi) * inv for yh, a, b, inv in zip(yhs, lo, hi, invs)]
        else:
            (gain,) = extra
            outs = [yh * inv * gain for yh, inv in zip(yhs, invs)]
        return outs[0] if len(outs) == 1 else jnp.concatenate(outs, axis=1)

    return ep


def _ep_gelu(y):
    c = math.sqrt(2.0 / math.pi)
    return 0.5 * y * (1.0 + jnp.tanh(c * (y + 0.044715 * (y * y * y))))


def _ep_gdn_gates(n_heads):
    def ep(y, a_log, dt_bias):
        lane = lax.broadcasted_iota(jnp.int32, y.shape, 1)
        beta = _sigmoid(y)
        g = -jnp.exp(a_log) * _softplus(y + dt_bias)
        return jnp.where(lane < n_heads, beta, g)

    return ep


def _matmul_res_kernel(a_ref, w_ref, r_ref, o_ref):
    y = jnp.dot(a_ref[...].astype(BF16), w_ref[...], preferred_element_type=F32)
    o_ref[...] = r_ref[...] + y


def matmul_residual(a, w, res, *, tm=1024, name="out_proj"):
    M, K = a.shape
    N = w.shape[1]
    tm = _pick(M, tm, SUBLANES)
    tn = _pick(N, 1024 if K <= 2560 else 512, LANES)
    return pl.pallas_call(
        _matmul_res_kernel,
        grid=(M // tm, N // tn),
        in_specs=[
            pl.BlockSpec((tm, K), lambda i, j: (i, 0)),
            pl.BlockSpec((K, tn), lambda i, j: (0, j)),
            pl.BlockSpec((tm, tn), lambda i, j: (i, j)),
        ],
        out_specs=pl.BlockSpec((tm, tn), lambda i, j: (i, j)),
        out_shape=jax.ShapeDtypeStruct((M, N), F32),
        compiler_params=_params("parallel", "arbitrary"),
        name=name,
    )(a, w, res)


def _ffn_kernel(x_ref, g_ref, wg_ref, wu_ref, wd_ref, o_ref, xn_ref, *, sub):
    @pl.when(pl.program_id(1) == 0)
    def _():
        x = x_ref[...]
        xn_ref[...] = _rms(x, g_ref[...]).astype(BF16)
        o_ref[...] = x

    xn = xn_ref[...]
    tf = wg_ref.shape[1]
    hs = []
    for c in range(tf // sub):
        cs = slice(c * sub, (c + 1) * sub)
        gate = jnp.dot(xn, wg_ref[:, cs], preferred_element_type=F32)
        up = jnp.dot(xn, wu_ref[:, cs], preferred_element_type=F32)
        hs.append((_silu(gate) * up).astype(BF16))
    h = hs[0] if len(hs) == 1 else jnp.concatenate(hs, axis=1)
    o_ref[...] += jnp.dot(h, wd_ref[...], preferred_element_type=F32)


def ffn(x, g, w_gu, w_down, *, tm=512, tf=512, sub=256):
    M, D = x.shape
    Fh = w_down.shape[0]
    tm = _pick(M, tm, SUBLANES)
    tf = _pick(Fh, tf, LANES)
    nf = Fh // tf
    return pl.pallas_call(
        functools.partial(_ffn_kernel, sub=min(sub, tf)),
        grid=(M // tm, nf),
        in_specs=[
            pl.BlockSpec((tm, D), lambda i, j: (i, 0)),
            pl.BlockSpec((1, D), lambda i, j: (0, 0)),
            pl.BlockSpec((D, tf), lambda i, j: (0, j)),
            pl.BlockSpec((D, tf), lambda i, j: (0, j + nf)),
            pl.BlockSpec((tf, D), lambda i, j: (j, 0)),
        ],
        out_specs=pl.BlockSpec((tm, D), lambda i, j: (i, 0)),
        out_shape=jax.ShapeDtypeStruct((M, D), F32),
        scratch_shapes=[pltpu.VMEM((tm, D), BF16)],
        compiler_params=_params("parallel", "arbitrary"),
        name="ffn",
    )(x, g.reshape(1, D), w_gu, w_gu, w_down)


SB_STOP = 88.0


def _sb_kernel(q_ref, k_ref, v_ref, o_ref, *, blk, heads, n_q):
    W = 2 * blk
    row = lax.broadcasted_iota(jnp.int32, (blk, W), 0)
    col = lax.broadcasted_iota(jnp.int32, (blk, W), 1)
    r2 = lax.broadcasted_iota(jnp.int32, (W, W), 0)
    c2 = lax.broadcasted_iota(jnp.int32, (W, W), 1)
    suffix_w = jnp.where(r2 >= c2, 1.0, 0.0).astype(BF16)
    rb = lax.broadcasted_iota(jnp.int32, (blk, blk), 0)
    cb = lax.broadcasted_iota(jnp.int32, (blk, blk), 1)
    suffix_b = jnp.where(rb >= cb, 1.0, 0.0).astype(BF16)

    def dot_nt(a, b):
        return lax.dot_general(a, b, (((1,), (1,)), ((), ())), preferred_element_type=F32)

    def suffix_sum(sp, suffix):
        hi = sp.astype(BF16)
        lo = (sp - hi.astype(F32)).astype(BF16)
        return (jnp.dot(hi, suffix, preferred_element_type=F32)
                + jnp.dot(lo, suffix, preferred_element_type=F32))

    def qblock(i, carry):
        q0 = pl.multiple_of(i * blk, blk)
        w0 = pl.multiple_of(jnp.maximum(i - 1, 0) * blk, blk)
        valid = col < row + (q0 - w0)
        hsl = [slice(h * HEAD_DIM, (h + 1) * HEAD_DIM) for h in range(heads)]
        qs = [q_ref[pl.ds(q0, blk), sl] for sl in hsl]
        zs = [dot_nt(q, k_ref[pl.ds(w0, W), sl]) for q, sl in zip(qs, hsl)]
        sps = [jnp.where(valid, _softplus(z), 0.0) for z in zs]
        cums = [suffix_sum(sp, suffix_w) for sp in sps]
        ws = [jnp.exp(jnp.where(valid, z - cum, NEG_BIG)).astype(BF16) for z, cum in zip(zs, cums)]
        accs = [jnp.dot(w, v_ref[pl.ds(w0, W), sl], preferred_element_type=F32)
                for w, sl in zip(ws, hsl)]
        runs = [cum[:, 0:1] for cum in cums]
        def cond(c):
            least = functools.reduce(jnp.minimum, c[2])
            return (c[0] >= 0) & (jnp.min(least) < SB_STOP)

        def body(c):
            j, acc, run = c
            s0 = pl.multiple_of(j * blk, blk)
            z = [dot_nt(q, k_ref[pl.ds(s0, blk), sl]) for q, sl in zip(qs, hsl)]
            cum = [suffix_sum(_softplus(zh), suffix_b) for zh in z]
            w = [jnp.exp(zh - ch - rh).astype(BF16) for zh, ch, rh in zip(z, cum, run)]
            acc = tuple(a + jnp.dot(wh, v_ref[pl.ds(s0, blk), sl], preferred_element_type=F32)
                        for a, wh, sl in zip(acc, w, hsl))
            return j - 1, acc, tuple(rh + ch[:, 0:1] for rh, ch in zip(run, cum))

        _, accs, _ = lax.while_loop(cond, body, (i - 2, tuple(accs), tuple(runs)))
        for acc, sl in zip(accs, hsl):
            o_ref[pl.ds(q0, blk), sl] = acc.astype(o_ref.dtype)
        return carry

    lax.fori_loop(0, n_q, qblock, 0)


def sb_attention(qkv, *, blk=128, heads=4):
    B, T, HD3 = qkv.shape
    HD = HD3 // 3
    H = HD // HEAD_DIM
    assert T % blk == 0 and T >= 2 * blk and H % heads == 0
    nh = H // heads
    seq = lambda off: pl.BlockSpec((None, T, heads * HEAD_DIM), lambda b, h: (b, 0, h + off))
    return pl.pallas_call(
        functools.partial(_sb_kernel, blk=blk, heads=heads, n_q=T // blk),
        grid=(B, nh),
        in_specs=[seq(0), seq(nh), seq(2 * nh)],
        out_specs=seq(0),
        out_shape=jax.ShapeDtypeStruct((B, T, HD), BF16),
        compiler_params=_params("parallel", "parallel"),
        name="sb_attention",
    )(qkv, qkv, qkv)


def _bdot(a, b):
    return jnp.dot(a.astype(BF16), b.astype(BF16), preferred_element_type=F32)


def _bdot_nt(a, b):
    return lax.dot_general(a.astype(BF16), b.astype(BF16), (((1,), (1,)), ((), ())),
                           preferred_element_type=F32)


def _bdot_tn(a, b):
    return lax.dot_general(a.astype(BF16), b.astype(BF16), (((0,), (0,)), ((), ())),
                           preferred_element_type=F32)


def _gdn_kernel(xq_ref, xk_ref, xv_ref, z_ref, cwq_ref, cwk_ref, cwv_ref, g_ref, beta_ref, gain_ref,
                o_ref, q_ref, k_ref, v_ref, hq_ref, hk_ref, hv_ref, buf_ref,
                u_ref, wq_ref, kd_ref, attn_ref, gl_ref, out_ref, state_ref,
                *, chunk, kheads, rep, group, taps):
    C = chunk

    @pl.when(pl.program_id(2) == 0)
    def _():
        state_ref[...] = jnp.zeros_like(state_ref)
        hq_ref[...] = jnp.zeros_like(hq_ref)
        hk_ref[...] = jnp.zeros_like(hk_ref)
        hv_ref[...] = jnp.zeros_like(hv_ref)

    def conv_silu(x_ref, halo_ref, w_ref, dst_ref, l2_scale):
        n, width = x_ref.shape
        buf_ref[0:SUBLANES, 0:width] = halo_ref[...]
        buf_ref[SUBLANES:SUBLANES + n, 0:width] = x_ref[...]
        halo_ref[...] = x_ref[n - SUBLANES:n, :]
        w = w_ref[...]
        y = w[taps - 1:taps, :] * x_ref[...]
        for kk in range(taps - 1):
            shift = taps - 1 - kk
            y = y + w[kk:kk + 1, :] * buf_ref[SUBLANES - shift:SUBLANES - shift + n, 0:width]
        y = _silu(y)
        for h in range(width // HEAD_DIM):
            sl = slice(h * HEAD_DIM, (h + 1) * HEAD_DIM)
            yh = y[:, sl]
            if l2_scale is not None:
                ss = jnp.sum(yh * yh, axis=-1, keepdims=True)
                yh = yh * (lax.rsqrt(ss + NORM_EPS) * l2_scale)
            dst_ref[:, sl] = yh

    conv_silu(xq_ref, hq_ref, cwq_ref, q_ref, HEAD_DIM ** -0.5)
    conv_silu(xk_ref, hk_ref, cwk_ref, k_ref, 1.0)
    conv_silu(xv_ref, hv_ref, cwv_ref, v_ref, None)

    ri = lax.broadcasted_iota(jnp.int32, (C, C), 0)
    ci = lax.broadcasted_iota(jnp.int32, (C, C), 1)
    eye = ri == ci
    lower = ci <= ri
    strict = ci < ri
    gain = gain_ref[...]
    n_double = max(1, int(math.ceil(math.log2(C))) - 1)
    nv = kheads * rep
    heads = [slice(r * HEAD_DIM, (r + 1) * HEAD_DIM) for r in range(nv)]
    G = group
    rows = G * C
    bmm = functools.partial(jnp.einsum, "gij,gjk->gik", preferred_element_type=F32)
    bmm_nt = functools.partial(jnp.einsum, "gid,gjd->gij", preferred_element_type=F32)

    def prepare(khs):
        rs = [r for kh in khs for r in range(kh * rep, (kh + 1) * rep)]
        every = lambda f: [f(i) for i in range(len(rs))]
        qs = {kh: q_ref[:, heads[kh]].reshape(G, C, HEAD_DIM) for kh in khs}
        ks = {kh: k_ref[:, heads[kh]].reshape(G, C, HEAD_DIM) for kh in khs}
        k16s = {kh: ks[kh].astype(BF16) for kh in khs}
        qks = {kh: bmm_nt(qs[kh].astype(BF16), k16s[kh]) for kh in khs}
        q = every(lambda i: qs[rs[i] // rep])
        k = every(lambda i: ks[rs[i] // rep])
        k16 = every(lambda i: k16s[rs[i] // rep])
        g_row = every(lambda i: g_ref[rs[i]])
        beta_row = every(lambda i: beta_ref[rs[i]])
        g_col = every(lambda i: jnp.sum(jnp.where(eye, g_row[i], 0.0), axis=2, keepdims=True))
        beta_col = every(lambda i: jnp.sum(jnp.where(eye, beta_row[i], 0.0), axis=2, keepdims=True))
        gc_col = every(lambda i: jnp.sum(jnp.where(lower, g_row[i], 0.0), axis=2, keepdims=True))
        gc_row = every(lambda i: jnp.sum(jnp.where(ri <= ci, g_col[i], 0.0), axis=1, keepdims=True))
        g_last = every(lambda i: jnp.sum(g_row[i], axis=2, keepdims=True))
        decay = every(lambda i: jnp.where(
            lower, jnp.exp(jnp.where(lower, gc_col[i] - gc_row[i], 0.0)), 0.0))
        eg = every(lambda i: jnp.exp(gc_col[i]))
        kb = every(lambda i: k[i] * beta_col[i])
        a_mat = every(lambda i: jnp.where(strict, bmm_nt(kb[i].astype(BF16), k16[i]) * decay[i], 0.0))
        t_mat = every(lambda i: jnp.where(eye, 1.0, 0.0) - a_mat[i])
        p = a_mat
        for _ in range(n_double):
            p16 = every(lambda i: p[i].astype(BF16))
            p = every(lambda i: bmm(p16[i], p16[i]))
            t_mat = every(lambda i: t_mat[i] + bmm(t_mat[i].astype(BF16), p[i].astype(BF16)))
        rhs = every(lambda i: jnp.concatenate(
            [v_ref[:, heads[rs[i]]].reshape(G, C, HEAD_DIM) * beta_col[i], kb[i] * eg[i]],
            axis=2).astype(BF16))
        uw = every(lambda i: bmm(t_mat[i].astype(BF16), rhs[i]))
        for i, r in enumerate(rs):
            u_ref[r] = uw[i][:, :, :HEAD_DIM].reshape(rows, HEAD_DIM)
            wq_ref[r] = jnp.concatenate([uw[i][:, :, HEAD_DIM:], q[i] * eg[i]], axis=1).astype(
                BF16).reshape(2 * rows, HEAD_DIM)
            kd_ref[r] = (k[i] * jnp.exp(g_last[i] - gc_col[i])).astype(BF16).reshape(rows, HEAD_DIM)
            attn_ref[r] = (qks[r // rep] * decay[i]).astype(BF16).reshape(rows, C)
            gl_ref[r] = jnp.broadcast_to(g_last[i], (G, 1, HEAD_DIM))

    def recur(nl, states):
        ls = pl.multiple_of(nl * C, C)
        ls2 = pl.multiple_of(nl * 2 * C, 2 * C)
        hs = range(nv)
        s16 = [states[r].astype(BF16) for r in hs]
        ws = [jnp.dot(wq_ref[r, pl.ds(ls2, 2 * C), :], s16[r], preferred_element_type=F32) for r in hs]
        v_new = [(u_ref[r, pl.ds(ls, C), :] - ws[r][:C]).astype(BF16) for r in hs]
        upd = [lax.dot_general(kd_ref[r, pl.ds(ls, C), :], v_new[r], (((0,), (0,)), ((), ())),
                               preferred_element_type=F32) for r in hs]
        new = tuple(states[r] * jnp.exp(gl_ref[r, nl]) + upd[r] for r in hs)
        for r in hs:
            out_ref[r, pl.ds(ls, C), :] = ws[r][C:] + jnp.dot(
                attn_ref[r, pl.ds(ls, C), :], v_new[r], preferred_element_type=F32)
        return new

    for k0 in range(0, kheads, 2):
        prepare(list(range(k0, min(k0 + 2, kheads))))
    states = lax.fori_loop(0, group, recur, tuple(state_ref[r] for r in range(nv)))
    for r in range(nv):
        state_ref[r] = states[r]
        o_ref[:, heads[r]] = (_rms(out_ref[r], gain) * _silu(z_ref[:, heads[r]])).astype(o_ref.dtype)


def gdn_delta_rule(proj, conv_w, KD, VD, g, beta, gain, *, chunk=GDN_CHUNK, group=8, kheads=4):
    B, T, _ = proj.shape
    taps = conv_w.shape[0]
    Hk, Hv = KD // HEAD_DIM, VD // HEAD_DIM
    rep = Hv // Hk
    N = T // chunk
    group = _pick(N, group, 1)
    kheads = _pick(Hk, kheads, 1)
    nv = kheads * rep
    kw, vw = kheads * HEAD_DIM, nv * HEAD_DIM
    assert KD % kw == 0 and (2 * KD) % vw == 0 and N % group == 0
    koff, voff, zoff = KD // kw, 2 * KD // vw, (2 * KD + VD) // vw
    rows = group * chunk
    assert rows % SUBLANES == 0
    tile = lambda width, f: pl.BlockSpec((None, rows, width), f)
    tapw = lambda width, f: pl.BlockSpec((taps, width), f)
    gate = pl.BlockSpec((None, nv, group, 1, chunk), lambda b, h, t: (b, h, t, 0, 0))
    return pl.pallas_call(
        functools.partial(_gdn_kernel, chunk=chunk, kheads=kheads, rep=rep, group=group, taps=taps),
        grid=(B, Hk // kheads, N // group),
        in_specs=[
            tile(kw, lambda b, h, t: (b, t, h)),
            tile(kw, lambda b, h, t: (b, t, h + koff)),
            tile(vw, lambda b, h, t: (b, t, h + voff)),
            tile(vw, lambda b, h, t: (b, t, h + zoff)),
            tapw(kw, lambda b, h, t: (0, h)),
            tapw(kw, lambda b, h, t: (0, h + koff)),
            tapw(vw, lambda b, h, t: (0, h + voff)),
            gate,
            gate,
            pl.BlockSpec((1, HEAD_DIM), lambda b, h, t: (0, 0)),
        ],
        out_specs=tile(vw, lambda b, h, t: (b, t, h)),
        out_shape=jax.ShapeDtypeStruct((B, T, VD), BF16),
        scratch_shapes=[
            pltpu.VMEM((rows, kw), F32),
            pltpu.VMEM((rows, kw), F32),
            pltpu.VMEM((rows, vw), F32),
            pltpu.VMEM((SUBLANES, kw), F32),
            pltpu.VMEM((SUBLANES, kw), F32),
            pltpu.VMEM((SUBLANES, vw), F32),
            pltpu.VMEM((rows + SUBLANES, vw), F32),
            pltpu.VMEM((nv, rows, HEAD_DIM), F32),
            pltpu.VMEM((nv, 2 * rows, HEAD_DIM), BF16),
            pltpu.VMEM((nv, rows, HEAD_DIM), BF16),
            pltpu.VMEM((nv, rows, chunk), BF16),
            pltpu.VMEM((nv, group, 1, HEAD_DIM), F32),
            pltpu.VMEM((nv, rows, HEAD_DIM), F32),
            pltpu.VMEM((nv, HEAD_DIM, HEAD_DIM), F32),
        ],
        compiler_params=_params("parallel", "parallel", "arbitrary"),
        name="gdn_delta_rule",
    )(proj, proj, proj, proj, conv_w, conv_w, conv_w, g, beta, gain)


def _rope_table_kernel(pos_ref, freq_ref, sign_ref, cos_ref, sin_ref):
    ang = pos_ref[...].astype(F32) * freq_ref[...]
    cos_ref[...] = jnp.cos(ang)
    sin_ref[...] = jnp.sin(ang) * sign_ref[...]


def rope_tables(pos, *, tm=1024):
    M = pos.shape[0]
    tm = _pick(M, tm, SUBLANES)
    half = ROPE_DIM // 2
    lane = jnp.arange(HEAD_DIM)
    inv_freq = ROPE_THETA ** (-(lane % half).astype(F32) / half)
    freq = jnp.where(lane < ROPE_DIM, inv_freq, 0.0).reshape(1, HEAD_DIM).astype(F32)
    sign = jnp.where(lane < half, -1.0, 1.0).reshape(1, HEAD_DIM).astype(F32)
    spec = pl.BlockSpec((tm, HEAD_DIM), lambda i: (i, 0))
    const = pl.BlockSpec((1, HEAD_DIM), lambda i: (0, 0))
    return pl.pallas_call(
        _rope_table_kernel,
        grid=(M // tm,),
        in_specs=[pl.BlockSpec((tm, 1), lambda i: (i, 0)), const, const],
        out_specs=[spec, spec],
        out_shape=[jax.ShapeDtypeStruct((M, HEAD_DIM), F32)] * 2,
        compiler_params=_params("parallel"),
        name="rope_tables",
    )(pos, freq, sign)


def _band_kernel(q_ref, kc_ref, kp_ref, vc_ref, vp_ref, o_ref, lse_ref, *stage, blk, span, heads, d, batch):
    n = pl.program_id(2)
    qi = lax.broadcasted_iota(jnp.int32, (blk, 2 * blk), 0)
    kj = lax.broadcasted_iota(jnp.int32, (blk, 2 * blk), 1)
    steps = blk + qi - kj
    first_key = jnp.where(n > 0, 0, blk)
    valid = (steps >= 0) & (steps <= span) & (kj >= first_key)
    ins = (q_ref, kc_ref, kp_ref, vc_ref, vp_ref)

    def attend(part, load, store):
        s_list = []
        for key in part:
            q = load(0, key).astype(BF16)
            kw = jnp.concatenate([load(2, key), load(1, key)], axis=0).astype(BF16)
            s = lax.dot_general(q, kw, (((1,), (1,)), ((), ())), preferred_element_type=F32)
            s_list.append(jnp.where(valid, s, NEG_BIG))
        m_list = [jnp.max(s, axis=-1, keepdims=True) for s in s_list]
        p_list = [jnp.where(valid, jnp.exp(s - m), 0.0) for s, m in zip(s_list, m_list)]
        den_list = [jnp.sum(p, axis=-1, keepdims=True) for p in p_list]
        for key, p, m, den in zip(part, p_list, m_list, den_list):
            vw = jnp.concatenate([load(4, key), load(3, key)], axis=0).astype(BF16)
            o = jnp.dot((p / den).astype(BF16), vw, preferred_element_type=F32)
            store(key, o, jnp.broadcast_to(m + jnp.log(den), (blk, HEAD_DIM)))

    if d == 1:
        def load(a, h):
            return ins[a][:, h * HEAD_DIM:(h + 1) * HEAD_DIM]

        def store(h, o, lse):
            o_ref[:, h * HEAD_DIM:(h + 1) * HEAD_DIM] = o
            lse_ref[:, h * HEAD_DIM:(h + 1) * HEAD_DIM] = lse

        for b0 in range(0, heads, batch):
            attend(list(range(b0, min(b0 + batch, heads))), load, store)
        return

    stage_in, stage_o, stage_l = stage[:5], stage[5], stage[6]
    for h in range(heads):
        sl = slice(h * HEAD_DIM, (h + 1) * HEAD_DIM)
        for a in range(5):
            stage_in[a][...] = ins[a][:, sl]

        def load(a, r):
            return stage_in[a][pl.ds(r, blk, stride=d), :]

        def store(r, o, lse):
            stage_o[pl.ds(r, blk, stride=d), :] = o
            stage_l[pl.ds(r, blk, stride=d), :] = lse

        for b0 in range(0, d, batch):
            attend(list(range(b0, min(b0 + batch, d))), load, store)
        o_ref[:, sl] = stage_o[...]
        lse_ref[:, sl] = stage_l[...]


def band_attention(q, k, v, group, window, dilation, *, blk=DSW_BLOCK, group_heads=DSW_HEADS_PER_GROUP):
    B, T, HD = q.shape
    d = dilation
    span = window // d
    unit = d * blk
    assert span <= blk and T % unit == 0
    heads = group_heads if unit <= 512 else 2
    assert group_heads % heads == 0
    hw = heads * HEAD_DIM
    col0 = group * group_heads // heads
    cur = pl.BlockSpec((None, unit, hw), lambda b, c, n: (b, n, col0 + c))
    prev = pl.BlockSpec((None, unit, hw), lambda b, c, n: (b, jnp.maximum(n - 1, 0), col0 + c))
    out = pl.BlockSpec((None, unit, hw), lambda b, c, n: (b, n, c))
    return pl.pallas_call(
        functools.partial(_band_kernel, blk=blk, span=span, heads=heads, d=d, batch=6 if d == 1 else 4),
        grid=(B, group_heads // heads, T // unit),
        in_specs=[cur, cur, prev, cur, prev],
        out_specs=[out, out],
        out_shape=[jax.ShapeDtypeStruct((B, T, group_heads * HEAD_DIM), F32)] * 2,
        scratch_shapes=[] if d == 1 else [pltpu.VMEM((unit, HEAD_DIM), F32)] * 7,
        compiler_params=_params("parallel", "parallel", "arbitrary"),
        name=f"band_attention_d{d}",
    )(q, k, k, v, v)


def _combine_kernel(*refs, n):
    o_refs, l_refs, out_ref = refs[:n], refs[n:2 * n], refs[2 * n]
    ls = [r[...] for r in l_refs]
    m = functools.reduce(jnp.maximum, ls)
    es = [jnp.exp(l - m) for l in ls]
    inv = 1.0 / functools.reduce(lambda a, b: a + b, es)
    w = o_refs[0].shape[1]
    for gi in range(n):
        out_ref[:, gi * w:(gi + 1) * w] = (o_refs[gi][...] * (es[gi] * inv)).astype(out_ref.dtype)


def combine_groups(os_, lses, *, tm=512):
    n = len(os_)
    M, w = os_[0].shape
    tm = _pick(M, tm, SUBLANES)
    spec = pl.BlockSpec((tm, w), lambda i: (i, 0))
    return pl.pallas_call(
        functools.partial(_combine_kernel, n=n),
        grid=(M // tm,),
        in_specs=[spec] * (2 * n),
        out_specs=pl.BlockSpec((tm, n * w), lambda i: (i, 0)),
        out_shape=jax.ShapeDtypeStruct((M, n * w), BF16),
        compiler_params=_params("parallel"),
        name="combine_groups",
    )(*os_, *lses)


def _lru_kernel(x_ref, gate_ref, cw_ref, cb_ref, wa_ref, wx_ref, ba_ref, bx_ref, lam_ref, o_ref,
                a_ref, u_ref, h_ref, xc_ref, halo_ref, buf_ref, *, tt, nblk, taps):
    @pl.when(pl.program_id(1) == 0)
    def _():
        h_ref[...] = jnp.zeros_like(h_ref)
        halo_ref[...] = jnp.zeros_like(halo_ref)

    buf_ref[0:SUBLANES, :] = halo_ref[...]
    buf_ref[SUBLANES:SUBLANES + tt, :] = x_ref[...]
    halo_ref[...] = x_ref[tt - SUBLANES:tt, :]
    cw = cw_ref[...]
    xc = cb_ref[...] + cw[taps - 1:taps, :] * x_ref[...]
    for kk in range(taps - 1):
        shift = taps - 1 - kk
        xc = xc + cw[kk:kk + 1, :] * buf_ref[SUBLANES - shift:SUBLANES - shift + tt, :]
    xc_ref[...] = xc

    for nb in range(nblk):
        sl = slice(nb * LRU_BLOCK_DIM, (nb + 1) * LRU_BLOCK_DIM)
        xb = xc_ref[:, sl]
        xb16 = xb.astype(BF16)
        r = _sigmoid(jnp.dot(xb16, wa_ref[nb], preferred_element_type=F32) + ba_ref[:, sl])
        ig = _sigmoid(jnp.dot(xb16, wx_ref[nb], preferred_element_type=F32) + bx_ref[:, sl])
        log_a = (-LRU_C) * r * _softplus(-lam_ref[:, sl])
        a = jnp.exp(log_a)
        a_ref[:, sl] = a
        u_ref[:, sl] = jnp.sqrt(1.0 - a * a) * (ig * xb)

    def group(gi, h):
        base = pl.multiple_of(gi * SUBLANES, SUBLANES)
        for r8 in range(SUBLANES):
            h = a_ref[pl.ds(base + r8, 1), :] * h + u_ref[pl.ds(base + r8, 1), :]
            u_ref[pl.ds(base + r8, 1), :] = h
        return h

    h_ref[...] = lax.fori_loop(0, tt // SUBLANES, group, h_ref[...])
    o_ref[...] = (u_ref[...] * gate_ref[...].astype(F32)).astype(o_ref.dtype)


def lru_scan(proj, conv_w, conv_b, w_a, w_x, b_a, b_x, lam, *, tt=256):
    B, T, W2 = proj.shape
    W = W2 // 2
    nblk = W // LRU_BLOCK_DIM
    taps = conv_w.shape[0]
    tt = _pick(T, tt, SUBLANES)
    tile = pl.BlockSpec((None, tt, W), lambda b, i: (b, i, 0))
    xtile = pl.BlockSpec((None, tt, W), lambda b, i: (b, i, 1))
    wspec = pl.BlockSpec((nblk, LRU_BLOCK_DIM, LRU_BLOCK_DIM), lambda b, i: (0, 0, 0))
    vec = pl.BlockSpec((1, W), lambda b, i: (0, 0))
    return pl.pallas_call(
        functools.partial(_lru_kernel, tt=tt, nblk=nblk, taps=taps),
        grid=(B, T // tt),
        in_specs=[xtile, tile, pl.BlockSpec((taps, W), lambda b, i: (0, 0)), vec,
                  wspec, wspec, vec, vec, vec],
        out_specs=tile,
        out_shape=jax.ShapeDtypeStruct((B, T, W), BF16),
        scratch_shapes=[
            pltpu.VMEM((tt, W), F32),
            pltpu.VMEM((tt, W), F32),
            pltpu.VMEM((1, W), F32),
            pltpu.VMEM((tt, W), F32),
            pltpu.VMEM((SUBLANES, W), F32),
            pltpu.VMEM((tt + SUBLANES, W), F32),
        ],
        compiler_params=_params("parallel", "arbitrary"),
        name="lru_scan",
    )(proj, proj, conv_w, conv_b, w_a, w_x, b_a, b_x, lam)


def _row(v):
    return v.reshape(1, -1).astype(F32)


def sb_mixer(x2, B, T, norm_g, w_in, q_norm, k_norm, w_out):
    hd = w_in.shape[1] // 3
    w_in = w_in.astype(BF16)
    scale = HEAD_DIM ** -0.5
    tn = _pick(hd, 1024, LANES)
    nt = hd // tn
    q_ep, k_ep = _ep_headnorm(scale, False), _ep_headnorm(1.0, False)
    qkv = norm_matmul(x2, norm_g, w_in, tn=tn, out_dtype=BF16, name="sb_proj",
                      const_extras=(_row(q_norm), _row(k_norm)),
                      epilogue=[(0, nt, lambda y, gq, gk: q_ep(y, gq)),
                                (nt, 2 * nt, lambda y, gq, gk: k_ep(y, gk)),
                                (2 * nt, 3 * nt, None)])
    o = sb_attention(qkv.reshape(B, T, 3 * hd))
    return matmul_residual(o.reshape(B * T, hd), w_out.astype(BF16), x2, name="sb_out")


def gdn_mixer(x2, B, T, norm_g, w_in, conv_w, a_log, dt_bias, o_norm, w_out):
    M = B * T
    Hv = a_log.shape[0]
    vd = Hv * HEAD_DIM
    kd = (conv_w.shape[1] - vd) // 2
    main = 2 * kd + 2 * vd
    proj = norm_matmul(x2, norm_g, w_in.astype(BF16), col0=0, ncols=main, name="gdn_proj")
    w_ba = jnp.pad(w_in[:, main:], ((0, 0), (0, LANES - 2 * Hv))).astype(BF16)
    pad = lambda p: jnp.pad(p.astype(F32), (Hv, LANES - 2 * Hv)).reshape(1, LANES)
    gates = norm_matmul(x2, norm_g, w_ba, epilogue=_ep_gdn_gates(Hv),
                        const_extras=(pad(a_log), pad(dt_bias)), name="gdn_gates")
    N = T // GDN_CHUNK
    per_head = lambda a: a.reshape(B, T, Hv).transpose(0, 2, 1).reshape(B, Hv, N, 1, GDN_CHUNK)
    beta = per_head(gates[:, :Hv])
    g = per_head(gates[:, Hv:2 * Hv])
    o = gdn_delta_rule(proj.reshape(B, T, main), conv_w, kd, vd, g, beta, _row(o_norm))
    return matmul_residual(o.reshape(M, vd), w_out.astype(BF16), x2, name="gdn_out")


def dsw_mixer(x2, B, T, norm_g, positions, w_in, q_norm, k_norm, w_out):
    M = B * T
    hd = w_in.shape[1] // 3
    w_in = w_in.astype(BF16)
    cos_t, sin_t = rope_tables(positions.reshape(M, 1).astype(jnp.int32))
    scale = HEAD_DIM ** -0.5
    tn = 3 * HEAD_DIM * 2
    q = norm_matmul(x2, norm_g, w_in, col0=0, ncols=hd, epilogue=_ep_headnorm(scale, True),
                    row_extras=(cos_t, sin_t), const_extras=(_row(q_norm),), out_dtype=F32, tn=tn,
                    pipelined=True, name="dsw_proj_q")
    k = norm_matmul(x2, norm_g, w_in, col0=hd, ncols=hd, epilogue=_ep_headnorm(1.0, True),
                    row_extras=(cos_t, sin_t), const_extras=(_row(k_norm),), out_dtype=F32, tn=tn,
                    pipelined=True, name="dsw_proj_k")
    v = norm_matmul(x2, norm_g, w_in, col0=2 * hd, ncols=hd, out_dtype=F32, tn=tn, name="dsw_proj_v")
    shp = (B, T, hd)
    q, k, v = q.reshape(shp), k.reshape(shp), v.reshape(shp)
    os_, lses = [], []
    for gi, (window, dilation) in enumerate(DSW_GROUPS):
        o_g, lse_g = band_attention(q, k, v, gi, window, dilation)
        os_.append(o_g.reshape(M, -1))
        lses.append(lse_g.reshape(M, -1))
    o = combine_groups(os_, lses)
    return matmul_residual(o, w_out.astype(BF16), x2, name="dsw_out")


def lru_mixer(x2, B, T, norm_g, w_in, conv_w, conv_b, w_a, b_a, w_x, b_x, lam, w_out):
    W = w_in.shape[1] // 2
    w_in = w_in.astype(BF16)
    tn = _pick(W, 1024, LANES)
    nt = W // tn
    proj = norm_matmul(x2, norm_g, w_in, tn=tn, name="lru_proj",
                       epilogue=[(0, nt, _ep_gelu), (nt, 2 * nt, None)])
    y = lru_scan(proj.reshape(B, T, 2 * W), conv_w, _row(conv_b),
                 w_a.astype(BF16), w_x.astype(BF16), _row(b_a), _row(b_x), _row(lam))
    return matmul_residual(y.reshape(B * T, W), w_out.astype(BF16), x2, name="lru_out")


def kernel(x, positions, mix_norm, ffn_norm, ffn_w_gu, ffn_w_down, sb_w_in, sb_q_norm, sb_k_norm, sb_w_out, gdn_w_in, gdn_conv_w, gdn_a_log, gdn_dt_bias, gdn_o_norm, gdn_w_out, dsw_w_in, dsw_q_norm, dsw_k_norm, dsw_w_out, lru_w_in, lru_conv_w, lru_conv_b, lru_w_a, lru_b_a, lru_w_x, lru_b_x, lru_lambda, lru_w_out):
    B, T, D = x.shape
    depth = mix_norm.shape[0]
    x2 = x.reshape(B * T, D)
    for i in range(depth):
        kind, j = i % 4, i // 4
        if kind == 0:
            x2 = sb_mixer(x2, B, T, mix_norm[i], sb_w_in[j], sb_q_norm[j], sb_k_norm[j], sb_w_out[j])
        elif kind == 1:
            x2 = gdn_mixer(x2, B, T, mix_norm[i], gdn_w_in[j], gdn_conv_w[j], gdn_a_log[j],
                           gdn_dt_bias[j], gdn_o_norm[j], gdn_w_out[j])
        elif kind == 2:
            x2 = dsw_mixer(x2, B, T, mix_norm[i], positions, dsw_w_in[j], dsw_q_norm[j],
                           dsw_k_norm[j], dsw_w_out[j])
        else:
            x2 = lru_mixer(x2, B, T, mix_norm[i], lru_w_in[j], lru_conv_w[j], lru_conv_b[j],
                           lru_w_a[j], lru_b_a[j], lru_w_x[j], lru_b_x[j], lru_lambda[j], lru_w_out[j])
        x2 = ffn(x2, ffn_norm[i], ffn_w_gu[i].astype(BF16), ffn_w_down[i].astype(BF16))
    return x2.reshape(B, T, D)
```

```python
import functools
import math

import jax
import jax.numpy as jnp
from jax import lax
from jax.experimental import pallas as pl
from jax.experimental.pallas import tpu as pltpu

F32 = jnp.float32
BF16 = jnp.bfloat16

NORM_EPS = 1e-6
HEAD_DIM = 128
LANES = 128
SUBLANES = 8
VMEM_LIMIT_BYTES = 56 * 1024 * 1024

GDN_CONV = 4
GDN_CHUNK = 64
DSW_GROUPS = ((128, 1), (512, 4), (2048, 16))
DSW_HEADS_PER_GROUP = 6
DSW_BLOCK = 128
ROPE_DIM = HEAD_DIM // 4
ROPE_THETA = 500000.0
LRU_BLOCK_DIM = 256
LRU_C = 8.0
NEG_BIG = -1e30


def _pick(dim, pref, align):
    if dim <= pref:
        return dim
    t = (pref // align) * align
    while t >= align:
        if dim % t == 0:
            return t
        t -= align
    return dim


def _params(*sem):
    return pltpu.CompilerParams(dimension_semantics=sem, vmem_limit_bytes=VMEM_LIMIT_BYTES)


def _softplus(x):
    return jnp.maximum(x, 0.0) + jnp.log(1.0 + jnp.exp(-jnp.abs(x)))


def _sigmoid(x):
    return 1.0 / (1.0 + jnp.exp(-x))


def _silu(x):
    return x * _sigmoid(x)


def _rms(x, g):
    ms = jnp.mean(x * x, axis=-1, keepdims=True)
    return x * lax.rsqrt(ms + NORM_EPS) * g


def _norm_matmul_kernel(*refs, epilogue, n_extra):
    x_ref, g_ref, w_ref = refs[:3]
    extra = refs[3:3 + n_extra]
    o_ref, xn_ref = refs[3 + n_extra], refs[4 + n_extra]
    j = pl.program_id(1)

    @pl.when(j == 0)
    def _():
        xn_ref[...] = _rms(x_ref[...], g_ref[...]).astype(BF16)

    y = jnp.dot(xn_ref[...], w_ref[...], preferred_element_type=F32)
    ex = [e[...] for e in extra]
    if isinstance(epilogue, list):
        for lo, hi, fn in epilogue:
            @pl.when((j >= lo) & (j < hi))
            def _(fn=fn):
                o_ref[...] = (y if fn is None else fn(y, *ex)).astype(o_ref.dtype)
        return
    if epilogue is not None:
        y = epilogue(y, *ex)
    o_ref[...] = y.astype(o_ref.dtype)


def norm_matmul(x, g, w, *, col0=0, ncols=None, epilogue=None, row_extras=(), const_extras=(),
                out_dtype=F32, tm=1024, tn=1024, pipelined=False, name="norm_matmul"):
    M, K = x.shape
    ncols = w.shape[1] - col0 if ncols is None else ncols
    tm = _pick(M, tm, SUBLANES)
    tn = _pick(ncols, tn, LANES)
    assert col0 % tn == 0 and ncols % tn == 0 and M % tm == 0
    off = col0 // tn
    in_specs = [
        pl.BlockSpec((tm, K), lambda i, j: (i, 0)),
        pl.BlockSpec((1, K), lambda i, j: (0, 0)),
        pl.BlockSpec((K, tn), lambda i, j: (0, j + off)),
    ]
    args = [x, g.reshape(1, K), w]
    for e in row_extras:
        in_specs.append(pl.BlockSpec((tm, e.shape[1]), lambda i, j: (i, 0)))
        args.append(e)
    for e in const_extras:
        in_specs.append(pl.BlockSpec(e.shape, lambda i, j: (0, 0)))
        args.append(e)
    n_extra = len(row_extras) + len(const_extras)
    if pipelined:
        return _norm_matmul_pipelined(x, g, w, off, ncols, epilogue, row_extras, const_extras,
                                      out_dtype, tm, tn, name)
    return pl.pallas_call(
        functools.partial(_norm_matmul_kernel, epilogue=epilogue, n_extra=n_extra),
        grid=(M // tm, ncols // tn),
        in_specs=in_specs,
        out_specs=pl.BlockSpec((tm, tn), lambda i, j: (i, j)),
        out_shape=jax.ShapeDtypeStruct((M, ncols), out_dtype),
        scratch_shapes=[pltpu.VMEM((tm, K), BF16)],
        compiler_params=_params("parallel", "arbitrary"),
        name=name,
    )(*args)


def _norm_matmul_pipe_kernel(*refs, epilogue, n_extra, ncol, ntiles):
    x_ref, g_ref, w_ref = refs[:3]
    extra = refs[3:3 + n_extra]
    o_ref, xn_ref, y_ref = refs[3 + n_extra:6 + n_extra]
    t = pl.program_id(0)

    @pl.when(t == 0)
    def _():
        y_ref[1] = jnp.zeros(y_ref.shape[1:], F32)

    @pl.when((t % ncol == 0) & (t < ntiles))
    def _():
        xn_ref[...] = _rms(x_ref[...], g_ref[...]).astype(BF16)

    slot = t % 2
    o_ref[...] = epilogue(y_ref[1 - slot], *[e[...] for e in extra]).astype(o_ref.dtype)
    y_ref[slot] = jnp.dot(xn_ref[...], w_ref[...], preferred_element_type=F32)


def _norm_matmul_pipelined(x, g, w, off, ncols, epilogue, row_extras, const_extras, out_dtype,
                           tm, tn, name):
    M, K = x.shape
    ncol = ncols // tn
    ntiles = (M // tm) * ncol
    cur = lambda t: jnp.minimum(t, ntiles - 1)
    prv = lambda t: jnp.maximum(t - 1, 0)
    in_specs = [
        pl.BlockSpec((tm, K), lambda t: (cur(t) // ncol, 0)),
        pl.BlockSpec((1, K), lambda t: (0, 0)),
        pl.BlockSpec((K, tn), lambda t: (0, cur(t) % ncol + off)),
    ]
    args = [x, g.reshape(1, K), w]
    for e in row_extras:
        in_specs.append(pl.BlockSpec((tm, e.shape[1]), lambda t: (prv(t) // ncol, 0)))
        args.append(e)
    for e in const_extras:
        in_specs.append(pl.BlockSpec(e.shape, lambda t: (0, 0)))
        args.append(e)
    n_extra = len(row_extras) + len(const_extras)
    return pl.pallas_call(
        functools.partial(_norm_matmul_pipe_kernel, epilogue=epilogue, n_extra=n_extra,
                          ncol=ncol, ntiles=ntiles),
        grid=(ntiles + 1,),
        in_specs=in_specs,
        out_specs=pl.BlockSpec((tm, tn), lambda t: (prv(t) // ncol, prv(t) % ncol)),
        out_shape=jax.ShapeDtypeStruct((M, ncols), out_dtype),
        scratch_shapes=[pltpu.VMEM((tm, K), BF16), pltpu.VMEM((2, tm, tn), F32)],
        compiler_params=_params("arbitrary"),
        name=name,
    )(*args)


def _ep_headnorm(scale, rope):
    half = ROPE_DIM // 2

    def ep(y, *extra):
        yhs = [y[:, h * HEAD_DIM:(h + 1) * HEAD_DIM] for h in range(y.shape[1] // HEAD_DIM)]
        invs = [lax.rsqrt(jnp.mean(yh * yh, axis=-1, keepdims=True) + NORM_EPS) * scale for yh in yhs]
        if rope:
            cos_t, sin_t, gain = extra
            lane = lax.broadcasted_iota(jnp.int32, cos_t.shape, 1)
            gain_b = jnp.broadcast_to(gain, cos_t.shape)
            cg = cos_t * gain
            s_lo = jnp.where(lane < half, pltpu.roll(gain_b, HEAD_DIM - half, 1) * sin_t, 0.0)
            s_hi = jnp.where(lane >= half, pltpu.roll(gain_b, half, 1) * sin_t, 0.0)
            lo = [pltpu.roll(yh, HEAD_DIM - half, 1) for yh in yhs]
            hi = [pltpu.roll(yh, half, 1) for yh in yhs]
            outs = [(yh * cg + a * s_lo + b * s_hi) * inv for yh, a, b, inv in zip(yhs, lo, hi, invs)]
        else:
            (gain,) = extra
            outs = [yh * inv * gain for yh, inv in zip(yhs, invs)]
        return outs[0] if len(outs) == 1 else jnp.concatenate(outs, axis=1)

    return ep


def _ep_gelu(y):
    c = math.sqrt(2.0 / math.pi)
    return 0.5 * y * (1.0 + jnp.tanh(c * (y + 0.044715 * (y * y * y))))


def _ep_gdn_gates(n_heads):
    def ep(y, a_log, dt_bias):
        lane = lax.broadcasted_iota(jnp.int32, y.shape, 1)
        beta = _sigmoid(y)
        g = -jnp.exp(a_log) * _softplus(y + dt_bias)
        return jnp.where(lane < n_heads, beta, g)

    return ep


def _matmul_res_kernel(a_ref, w_ref, r_ref, o_ref):
    y = jnp.dot(a_ref[...].astype(BF16), w_ref[...], preferred_element_type=F32)
    o_ref[...] = r_ref[...] + y


def matmul_residual(a, w, res, *, tm=1024, name="out_proj"):
    M, K = a.shape
    N = w.shape[1]
    tm = _pick(M, tm, SUBLANES)
    tn = _pick(N, 1024 if K <= 2560 else 512, LANES)
    return pl.pallas_call(
        _matmul_res_kernel,
        grid=(M // tm, N // tn),
        in_specs=[
            pl.BlockSpec((tm, K), lambda i, j: (i, 0)),
            pl.BlockSpec((K, tn), lambda i, j: (0, j)),
            pl.BlockSpec((tm, tn), lambda i, j: (i, j)),
        ],
        out_specs=pl.BlockSpec((tm, tn), lambda i, j: (i, j)),
        out_shape=jax.ShapeDtypeStruct((M, N), F32),
        compiler_params=_params("parallel", "arbitrary"),
        name=name,
    )(a, w, res)


def _ffn_kernel(x_ref, g_ref, wg_ref, wu_ref, wd_ref, o_ref, xn_ref, *, sub):
    @pl.when(pl.program_id(1) == 0)
    def _():
        x = x_ref[...]
        xn_ref[...] = _rms(x, g_ref[...]).astype(BF16)
        o_ref[...] = x

    xn = xn_ref[...]
    tf = wg_ref.shape[1]
    hs = []
    for c in range(tf // sub):
        cs = slice(c * sub, (c + 1) * sub)
        gate = jnp.dot(xn, wg_ref[:, cs], preferred_element_type=F32)
        up = jnp.dot(xn, wu_ref[:, cs], preferred_element_type=F32)
        hs.append((_silu(gate) * up).astype(BF16))
    h = hs[0] if len(hs) == 1 else jnp.concatenate(hs, axis=1)
    o_ref[...] += jnp.dot(h, wd_ref[...], preferred_element_type=F32)


def ffn(x, g, w_gu, w_down, *, tm=512, tf=512, sub=256):
    M, D = x.shape
    Fh = w_down.shape[0]
    tm = _pick(M, tm, SUBLANES)
    tf = _pick(Fh, tf, LANES)
    nf = Fh // tf
    return pl.pallas_call(
        functools.partial(_ffn_kernel, sub=min(sub, tf)),
        grid=(M // tm, nf),
        in_specs=[
            pl.BlockSpec((tm, D), lambda i, j: (i, 0)),
            pl.BlockSpec((1, D), lambda i, j: (0, 0)),
            pl.BlockSpec((D, tf), lambda i, j: (0, j)),
            pl.BlockSpec((D, tf), lambda i, j: (0, j + nf)),
            pl.BlockSpec((tf, D), lambda i, j: (j, 0)),
        ],
        out_specs=pl.BlockSpec((tm, D), lambda i, j: (i, 0)),
        out_shape=jax.ShapeDtypeStruct((M, D), F32),
        scratch_shapes=[pltpu.VMEM((tm, D), BF16)],
        compiler_params=_params("parallel", "arbitrary"),
        name="ffn",
    )(x, g.reshape(1, D), w_gu, w_gu, w_down)


SB_STOP = 88.0


def _sb_kernel(q_ref, k_ref, v_ref, o_ref, *, blk, heads, n_q):
    W = 2 * blk
    row = lax.broadcasted_iota(jnp.int32, (blk, W), 0)
    col = lax.broadcasted_iota(jnp.int32, (blk, W), 1)
    r2 = lax.broadcasted_iota(jnp.int32, (W, W), 0)
    c2 = lax.broadcasted_iota(jnp.int32, (W, W), 1)
    suffix_w = jnp.where(r2 >= c2, 1.0, 0.0).astype(BF16)
    rb = lax.broadcasted_iota(jnp.int32, (blk, blk), 0)
    cb = lax.broadcasted_iota(jnp.int32, (blk, blk), 1)
    suffix_b = jnp.where(rb >= cb, 1.0, 0.0).astype(BF16)

    def dot_nt(a, b):
        return lax.dot_general(a, b, (((1,), (1,)), ((), ())), preferred_element_type=F32)

    def suffix_sum(sp, suffix):
        hi = sp.astype(BF16)
        lo = (sp - hi.astype(F32)).astype(BF16)
        return (jnp.dot(hi, suffix, preferred_element_type=F32)
                + jnp.dot(lo, suffix, preferred_element_type=F32))

    def qblock(i, carry):
        q0 = pl.multiple_of(i * blk, blk)
        w0 = pl.multiple_of(jnp.maximum(i - 1, 0) * blk, blk)
        valid = col < row + (q0 - w0)
        hsl = [slice(h * HEAD_DIM, (h + 1) * HEAD_DIM) for h in range(heads)]
        qs = [q_ref[pl.ds(q0, blk), sl] for sl in hsl]
        zs = [dot_nt(q, k_ref[pl.ds(w0, W), sl]) for q, sl in zip(qs, hsl)]
        sps = [jnp.where(valid, _softplus(z), 0.0) for z in zs]
        cums = [suffix_sum(sp, suffix_w) for sp in sps]
        ws = [jnp.exp(jnp.where(valid, z - cum, NEG_BIG)).astype(BF16) for z, cum in zip(zs, cums)]
        accs = [jnp.dot(w, v_ref[pl.ds(w0, W), sl], preferred_element_type=F32)
                for w, sl in zip(ws, hsl)]
        runs = [cum[:, 0:1] for cum in cums]
        def cond(c):
            least = functools.reduce(jnp.minimum, c[2])
            return (c[0] >= 0) & (jnp.min(least) < SB_STOP)

        def body(c):
            j, acc, run = c
            s0 = pl.multiple_of(j * blk, blk)
            z = [dot_nt(q, k_ref[pl.ds(s0, blk), sl]) for q, sl in zip(qs, hsl)]
            cum = [suffix_sum(_softplus(zh), suffix_b) for zh in z]
            w = [jnp.exp(zh - ch - rh).astype(BF16) for zh, ch, rh in zip(z, cum, run)]
            acc = tuple(a + jnp.dot(wh, v_ref[pl.ds(s0, blk), sl], preferred_element_type=F32)
                        for a, wh, sl in zip(acc, w, hsl))
            return j - 1, acc, tuple(rh + ch[:, 0:1] for rh, ch in zip(run, cum))

        _, accs, _ = lax.while_loop(cond, body, (i - 2, tuple(accs), tuple(runs)))
        for acc, sl in zip(accs, hsl):
            o_ref[pl.ds(q0, blk), sl] = acc.astype(o_ref.dtype)
        return carry

    lax.fori_loop(0, n_q, qblock, 0)


def sb_attention(qkv, *, blk=128, heads=4):
    B, T, HD3 = qkv.shape
    HD = HD3 // 3
    H = HD // HEAD_DIM
    assert T % blk == 0 and T >= 2 * blk and H % heads == 0
    nh = H // heads
    seq = lambda off: pl.BlockSpec((None, T, heads * HEAD_DIM), lambda b, h: (b, 0, h + off))
    return pl.pallas_call(
        functools.partial(_sb_kernel, blk=blk, heads=heads, n_q=T // blk),
        grid=(B, nh),
        in_specs=[seq(0), seq(nh), seq(2 * nh)],
        out_specs=seq(0),
        out_shape=jax.ShapeDtypeStruct((B, T, HD), BF16),
        compiler_params=_params("parallel", "parallel"),
        name="sb_attention",
    )(qkv, qkv, qkv)


def _bdot(a, b):
    return jnp.dot(a.astype(BF16), b.astype(BF16), preferred_element_type=F32)


def _bdot_nt(a, b):
    return lax.dot_general(a.astype(BF16), b.astype(BF16), (((1,), (1,)), ((), ())),
                           preferred_element_type=F32)


def _bdot_tn(a, b):
    return lax.dot_general(a.astype(BF16), b.astype(BF16), (((0,), (0,)), ((), ())),
                           preferred_element_type=F32)


def _gdn_kernel(xq_ref, xk_ref, xv_ref, z_ref, cwq_ref, cwk_ref, cwv_ref, g_ref, beta_ref, gain_ref,
                o_ref, q_ref, k_ref, v_ref, hq_ref, hk_ref, hv_ref, buf_ref,
                u_ref, wq_ref, kd_ref, attn_ref, gl_ref, out_ref, state_ref,
                *, chunk, kheads, rep, group, taps):
    C = chunk

    @pl.when(pl.program_id(2) == 0)
    def _():
        state_ref[...] = jnp.zeros_like(state_ref)
        hq_ref[...] = jnp.zeros_like(hq_ref)
        hk_ref[...] = jnp.zeros_like(hk_ref)
        hv_ref[...] = jnp.zeros_like(hv_ref)

    def conv_silu(x_ref, halo_ref, w_ref, dst_ref, l2_scale):
        n, width = x_ref.shape
        buf_ref[0:SUBLANES, 0:width] = halo_ref[...]
        buf_ref[SUBLANES:SUBLANES + n, 0:width] = x_ref[...]
        halo_ref[...] = x_ref[n - SUBLANES:n, :]
        w = w_ref[...]
        y = w[taps - 1:taps, :] * x_ref[...]
        for kk in range(taps - 1):
            shift = taps - 1 - kk
            y = y + w[kk:kk + 1, :] * buf_ref[SUBLANES - shift:SUBLANES - shift + n, 0:width]
        y = _silu(y)
        for h in range(width // HEAD_DIM):
            sl = slice(h * HEAD_DIM, (h + 1) * HEAD_DIM)
            yh = y[:, sl]
            if l2_scale is not None:
                ss = jnp.sum(yh * yh, axis=-1, keepdims=True)
                yh = yh * (lax.rsqrt(ss + NORM_EPS) * l2_scale)
            dst_ref[:, sl] = yh

    conv_silu(xq_ref, hq_ref, cwq_ref, q_ref, HEAD_DIM ** -0.5)
    conv_silu(xk_ref, hk_ref, cwk_ref, k_ref, 1.0)
    conv_silu(xv_ref, hv_ref, cwv_ref, v_ref, None)

    ri = lax.broadcasted_iota(jnp.int32, (C, C), 0)
    ci = lax.broadcasted_iota(jnp.int32, (C, C), 1)
    eye = ri == ci
    lower = ci <= ri
    strict = ci < ri
    gain = gain_ref[...]
    n_double = max(1, int(math.ceil(math.log2(C))) - 1)
    nv = kheads * rep
    heads = [slice(r * HEAD_DIM, (r + 1) * HEAD_DIM) for r in range(nv)]
    G = group
    rows = G * C
    bmm = functools.partial(jnp.einsum, "gij,gjk->gik", preferred_element_type=F32)
    bmm_nt = functools.partial(jnp.einsum, "gid,gjd->gij", preferred_element_type=F32)

    def prepare(khs):
        rs = [r for kh in khs for r in range(kh * rep, (kh + 1) * rep)]
        every = lambda f: [f(i) for i in range(len(rs))]
        qs = {kh: q_ref[:, heads[kh]].reshape(G, C, HEAD_DIM) for kh in khs}
        ks = {kh: k_ref[:, heads[kh]].reshape(G, C, HEAD_DIM) for kh in khs}
        k16s = {kh: ks[kh].astype(BF16) for kh in khs}
        qks = {kh: bmm_nt(qs[kh].astype(BF16), k16s[kh]) for kh in khs}
        q = every(lambda i: qs[rs[i] // rep])
        k = every(lambda i: ks[rs[i] // rep])
        k16 = every(lambda i: k16s[rs[i] // rep])
        g_row = every(lambda i: g_ref[rs[i]])
        beta_row = every(lambda i: beta_ref[rs[i]])
        g_col = every(lambda i: jnp.sum(jnp.where(eye, g_row[i], 0.0), axis=2, keepdims=True))
        beta_col = every(lambda i: jnp.sum(jnp.where(eye, beta_row[i], 0.0), axis=2, keepdims=True))
        gc_col = every(lambda i: jnp.sum(jnp.where(lower, g_row[i], 0.0), axis=2, keepdims=True))
        gc_row = every(lambda i: jnp.sum(jnp.where(ri <= ci, g_col[i], 0.0), axis=1, keepdims=True))
        g_last = every(lambda i: jnp.sum(g_row[i], axis=2, keepdims=True))
        decay = every(lambda i: jnp.where(
            lower, jnp.exp(jnp.where(lower, gc_col[i] - gc_row[i], 0.0)), 0.0))
        eg = every(lambda i: jnp.exp(gc_col[i]))
        kb = every(lambda i: k[i] * beta_col[i])
        a_mat = every(lambda i: jnp.where(strict, bmm_nt(kb[i].astype(BF16), k16[i]) * decay[i], 0.0))
        t_mat = every(lambda i: jnp.where(eye, 1.0, 0.0) - a_mat[i])
        p = a_mat
        for _ in range(n_double):
            p16 = every(lambda i: p[i].astype(BF16))
            p = every(lambda i: bmm(p16[i], p16[i]))
            t_mat = every(lambda i: t_mat[i] + bmm(t_mat[i].astype(BF16), p[i].astype(BF16)))
        rhs = every(lambda i: jnp.concatenate(
            [v_ref[:, heads[rs[i]]].reshape(G, C, HEAD_DIM) * beta_col[i], kb[i] * eg[i]],
            axis=2).astype(BF16))
        uw = every(lambda i: bmm(t_mat[i].astype(BF16), rhs[i]))
        for i, r in enumerate(rs):
            u_ref[r] = uw[i][:, :, :HEAD_DIM].reshape(rows, HEAD_DIM)
            wq_ref[r] = jnp.concatenate([uw[i][:, :, HEAD_DIM:], q[i] * eg[i]], axis=1).astype(
                BF16).reshape(2 * rows, HEAD_DIM)
            kd_ref[r] = (k[i] * jnp.exp(g_last[i] - gc_col[i])).astype(BF16).reshape(rows, HEAD_DIM)
            attn_ref[r] = (qks[r // rep] * decay[i]).astype(BF16).reshape(rows, C)
            gl_ref[r] = jnp.broadcast_to(g_last[i], (G, 1, HEAD_DIM))

    def recur(nl, states):
        ls = pl.multiple_of(nl * C, C)
        ls2 = pl.multiple_of(nl * 2 * C, 2 * C)
        hs = range(nv)
        s16 = [states[r].astype(BF16) for r in hs]
        ws = [jnp.dot(wq_ref[r, pl.ds(ls2, 2 * C), :], s16[r], preferred_element_type=F32) for r in hs]
        v_new = [(u_ref[r, pl.ds(ls, C), :] - ws[r][:C]).astype(BF16) for r in hs]
        upd = [lax.dot_general(kd_ref[r, pl.ds(ls, C), :], v_new[r], (((0,), (0,)), ((), ())),
                               preferred_element_type=F32) for r in hs]
        new = tuple(states[r] * jnp.exp(gl_ref[r, nl]) + upd[r] for r in hs)
        for r in hs:
            out_ref[r, pl.ds(ls, C), :] = ws[r][C:] + jnp.dot(
                attn_ref[r, pl.ds(ls, C), :], v_new[r], preferred_element_type=F32)
        return new

    for k0 in range(0, kheads, 2):
        prepare(list(range(k0, min(k0 + 2, kheads))))
    states = lax.fori_loop(0, group, recur, tuple(state_ref[r] for r in range(nv)))
    for r in range(nv):
        state_ref[r] = states[r]
        o_ref[:, heads[r]] = (_rms(out_ref[r], gain) * _silu(z_ref[:, heads[r]])).astype(o_ref.dtype)


def gdn_delta_rule(proj, conv_w, KD, VD, g, beta, gain, *, chunk=GDN_CHUNK, group=8, kheads=4):
    B, T, _ = proj.shape
    taps = conv_w.shape[0]
    Hk, Hv = KD // HEAD_DIM, VD // HEAD_DIM
    rep = Hv // Hk
    N = T // chunk
    group = _pick(N, group, 1)
    kheads = _pick(Hk, kheads, 1)
    nv = kheads * rep
    kw, vw = kheads * HEAD_DIM, nv * HEAD_DIM
    assert KD % kw == 0 and (2 * KD) % vw == 0 and N % group == 0
    koff, voff, zoff = KD // kw, 2 * KD // vw, (2 * KD + VD) // vw
    rows = group * chunk
    assert rows % SUBLANES == 0
    tile = lambda width, f: pl.BlockSpec((None, rows, width), f)
    tapw = lambda width, f: pl.BlockSpec((taps, width), f)
    gate = pl.BlockSpec((None, nv, group, 1, chunk), lambda b, h, t: (b, h, t, 0, 0))
    return pl.pallas_call(
        functools.partial(_gdn_kernel, chunk=chunk, kheads=kheads, rep=rep, group=group, taps=taps),
        grid=(B, Hk // kheads, N // group),
        in_specs=[
            tile(kw, lambda b, h, t: (b, t, h)),
            tile(kw, lambda b, h, t: (b, t, h + koff)),
            tile(vw, lambda b, h, t: (b, t, h + voff)),
            tile(vw, lambda b, h, t: (b, t, h + zoff)),
            tapw(kw, lambda b, h, t: (0, h)),
            tapw(kw, lambda b, h, t: (0, h + koff)),
            tapw(vw, lambda b, h, t: (0, h + voff)),
            gate,
            gate,
            pl.BlockSpec((1, HEAD_DIM), lambda b, h, t: (0, 0)),
        ],
        out_specs=tile(vw, lambda b, h, t: (b, t, h)),
        out_shape=jax.ShapeDtypeStruct((B, T, VD), BF16),
        scratch_shapes=[
            pltpu.VMEM((rows, kw), F32),
            pltpu.VMEM((rows, kw), F32),
            pltpu.VMEM((rows, vw), F32),
            pltpu.VMEM((SUBLANES, kw), F32),
            pltpu.VMEM((SUBLANES, kw), F32),
            pltpu.VMEM((SUBLANES, vw), F32),
            pltpu.VMEM((rows + SUBLANES, vw), F32),
            pltpu.VMEM((nv, rows, HEAD_DIM), F32),
            pltpu.VMEM((nv, 2 * rows, HEAD_DIM), BF16),
            pltpu.VMEM((nv, rows, HEAD_DIM), BF16),
            pltpu.VMEM((nv, rows, chunk), BF16),
            pltpu.VMEM((nv, group, 1, HEAD_DIM), F32),
            pltpu.VMEM((nv, rows, HEAD_DIM), F32),
            pltpu.VMEM((nv, HEAD_DIM, HEAD_DIM), F32),
        ],
        compiler_params=_params("parallel", "parallel", "arbitrary"),
        name="gdn_delta_rule",
    )(proj, proj, proj, proj, conv_w, conv_w, conv_w, g, beta, gain)


def _rope_table_kernel(pos_ref, freq_ref, sign_ref, cos_ref, sin_ref):
    ang = pos_ref[...].astype(F32) * freq_ref[...]
    cos_ref[...] = jnp.cos(ang)
    sin_ref[...] = jnp.sin(ang) * sign_ref[...]


def rope_tables(pos, *, tm=1024):
    M = pos.shape[0]
    tm = _pick(M, tm, SUBLANES)
    half = ROPE_DIM // 2
    lane = jnp.arange(HEAD_DIM)
    inv_freq = ROPE_THETA ** (-(lane % half).astype(F32) / half)
    freq = jnp.where(lane < ROPE_DIM, inv_freq, 0.0).reshape(1, HEAD_DIM).astype(F32)
    sign = jnp.where(lane < half, -1.0, 1.0).reshape(1, HEAD_DIM).astype(F32)
    spec = pl.BlockSpec((tm, HEAD_DIM), lambda i: (i, 0))
    const = pl.BlockSpec((1, HEAD_DIM), lambda i: (0, 0))
    return pl.pallas_call(
        _rope_table_kernel,
        grid=(M // tm,),
        in_specs=[pl.BlockSpec((tm, 1), lambda i: (i, 0)), const, const],
        out_specs=[spec, spec],
        out_shape=[jax.ShapeDtypeStruct((M, HEAD_DIM), F32)] * 2,
        compiler_params=_params("parallel"),
        name="rope_tables",
    )(pos, freq, sign)


def _band_kernel(q_ref, kc_ref, kp_ref, vc_ref, vp_ref, o_ref, lse_ref, *stage, blk, span, heads, d, batch):
    n = pl.program_id(2)
    qi = lax.broadcasted_iota(jnp.int32, (blk, 2 * blk), 0)
    kj = lax.broadcasted_iota(jnp.int32, (blk, 2 * blk), 1)
    steps = blk + qi - kj
    first_key = jnp.where(n > 0, 0, blk)
    valid = (steps >= 0) & (steps <= span) & (kj >= first_key)
    ins = (q_ref, kc_ref, kp_ref, vc_ref, vp_ref)

    def attend(part, load, store):
        s_list = []
        for key in part:
            q = load(0, key).astype(BF16)
            kw = jnp.concatenate([load(2, key), load(1, key)], axis=0).astype(BF16)
            s = lax.dot_general(q, kw, (((1,), (1,)), ((), ())), preferred_element_type=F32)
            s_list.append(jnp.where(valid, s, NEG_BIG))
        m_list = [jnp.max(s, axis=-1, keepdims=True) for s in s_list]
        p_list = [jnp.where(valid, jnp.exp(s - m), 0.0) for s, m in zip(s_list, m_list)]
        den_list = [jnp.sum(p, axis=-1, keepdims=True) for p in p_list]
        for key, p, m, den in zip(part, p_list, m_list, den_list):
            vw = jnp.concatenate([load(4, key), load(3, key)], axis=0).astype(BF16)
            o = jnp.dot((p / den).astype(BF16), vw, preferred_element_type=F32)
            store(key, o, jnp.broadcast_to(m + jnp.log(den), (blk, HEAD_DIM)))

    if d == 1:
        def load(a, h):
            return ins[a][:, h * HEAD_DIM:(h + 1) * HEAD_DIM]

        def store(h, o, lse):
            o_ref[:, h * HEAD_DIM:(h + 1) * HEAD_DIM] = o
            lse_ref[:, h * HEAD_DIM:(h + 1) * HEAD_DIM] = lse

        for b0 in range(0, heads, batch):
            attend(list(range(b0, min(b0 + batch, heads))), load, store)
        return

    if heads == 1:
        def load(a, r):
            return ins[a][pl.ds(r, blk, stride=d), :]

        def store(r, o, lse):
            o_ref[pl.ds(r, blk, stride=d), :] = o
            lse_ref[pl.ds(r, blk, stride=d), :] = lse

        for b0 in range(0, d, batch):
            attend(list(range(b0, min(b0 + batch, d))), load, store)
        return

    stage_in, stage_o, stage_l = stage[:5], stage[5], stage[6]
    for h in range(heads):
        sl = slice(h * HEAD_DIM, (h + 1) * HEAD_DIM)
        for a in range(5):
            stage_in[a][...] = ins[a][:, sl]

        def load(a, r):
            return stage_in[a][pl.ds(r, blk, stride=d), :]

        def store(r, o, lse):
            stage_o[pl.ds(r, blk, stride=d), :] = o
            stage_l[pl.ds(r, blk, stride=d), :] = lse

        for b0 in range(0, d, batch):
            attend(list(range(b0, min(b0 + batch, d))), load, store)
        o_ref[:, sl] = stage_o[...]
        lse_ref[:, sl] = stage_l[...]


def band_attention(q, k, v, group, window, dilation, *, blk=DSW_BLOCK, group_heads=DSW_HEADS_PER_GROUP):
    B, T, HD = q.shape
    d = dilation
    span = window // d
    unit = d * blk
    assert span <= blk and T % unit == 0
    heads = group_heads if unit <= 512 else 1
    assert group_heads % heads == 0
    hw = heads * HEAD_DIM
    col0 = group * group_heads // heads
    cur = pl.BlockSpec((None, unit, hw), lambda b, c, n: (b, n, col0 + c))
    prev = pl.BlockSpec((None, unit, hw), lambda b, c, n: (b, jnp.maximum(n - 1, 0), col0 + c))
    out = pl.BlockSpec((None, unit, hw), lambda b, c, n: (b, n, c))
    return pl.pallas_call(
        functools.partial(_band_kernel, blk=blk, span=span, heads=heads, d=d, batch=6 if d == 1 else 4),
        grid=(B, group_heads // heads, T // unit),
        in_specs=[cur, cur, prev, cur, prev],
        out_specs=[out, out],
        out_shape=[jax.ShapeDtypeStruct((B, T, group_heads * HEAD_DIM), F32)] * 2,
        scratch_shapes=[pltpu.VMEM((unit, HEAD_DIM), F32)] * 7 if d > 1 and heads > 1 else [],
        compiler_params=_params("parallel", "parallel", "arbitrary"),
        name=f"band_attention_d{d}",
    )(q, k, k, v, v)


def _combine_kernel(*refs, n):
    o_refs, l_refs, out_ref = refs[:n], refs[n:2 * n], refs[2 * n]
    ls = [r[...] for r in l_refs]
    m = functools.reduce(jnp.maximum, ls)
    es = [jnp.exp(l - m) for l in ls]
    inv = 1.0 / functools.reduce(lambda a, b: a + b, es)
    w = o_refs[0].shape[1]
    for gi in range(n):
        out_ref[:, gi * w:(gi + 1) * w] = (o_refs[gi][...] * (es[gi] * inv)).astype(out_ref.dtype)


def combine_groups(os_, lses, *, tm=512):
    n = len(os_)
    M, w = os_[0].shape
    tm = _pick(M, tm, SUBLANES)
    spec = pl.BlockSpec((tm, w), lambda i: (i, 0))
    return pl.pallas_call(
        functools.partial(_combine_kernel, n=n),
        grid=(M // tm,),
        in_specs=[spec] * (2 * n),
        out_specs=pl.BlockSpec((tm, n * w), lambda i: (i, 0)),
        out_shape=jax.ShapeDtypeStruct((M, n * w), BF16),
        compiler_params=_params("parallel"),
        name="combine_groups",
    )(*os_, *lses)


def _lru_kernel(x_ref, gate_ref, cw_ref, cb_ref, wa_ref, wx_ref, ba_ref, bx_ref, lam_ref, o_ref,
                a_ref, u_ref, h_ref, xc_ref, halo_ref, buf_ref, *, tt, nblk, taps):
    @pl.when(pl.program_id(1) == 0)
    def _():
        h_ref[...] = jnp.zeros_like(h_ref)
        halo_ref[...] = jnp.zeros_like(halo_ref)

    buf_ref[0:SUBLANES, :] = halo_ref[...]
    buf_ref[SUBLANES:SUBLANES + tt, :] = x_ref[...]
    halo_ref[...] = x_ref[tt - SUBLANES:tt, :]
    cw = cw_ref[...]
    xc = cb_ref[...] + cw[taps - 1:taps, :] * x_ref[...]
    for kk in range(taps - 1):
        shift = taps - 1 - kk
        xc = xc + cw[kk:kk + 1, :] * buf_ref[SUBLANES - shift:SUBLANES - shift + tt, :]
    xc_ref[...] = xc

    for nb in range(nblk):
        sl = slice(nb * LRU_BLOCK_DIM, (nb + 1) * LRU_BLOCK_DIM)
        xb = xc_ref[:, sl]
        xb16 = xb.astype(BF16)
        r = _sigmoid(jnp.dot(xb16, wa_ref[nb], preferred_element_type=F32) + ba_ref[:, sl])
        ig = _sigmoid(jnp.dot(xb16, wx_ref[nb], preferred_element_type=F32) + bx_ref[:, sl])
        log_a = (-LRU_C) * r * _softplus(-lam_ref[:, sl])
        a = jnp.exp(log_a)
        a_ref[:, sl] = a
        u_ref[:, sl] = jnp.sqrt(1.0 - a * a) * (ig * xb)

    def group(gi, h):
        base = pl.multiple_of(gi * SUBLANES, SUBLANES)
        for r8 in range(SUBLANES):
            h = a_ref[pl.ds(base + r8, 1), :] * h + u_ref[pl.ds(base + r8, 1), :]
            u_ref[pl.ds(base + r8, 1), :] = h
        return h

    h_ref[...] = lax.fori_loop(0, tt // SUBLANES, group, h_ref[...])
    o_ref[...] = (u_ref[...] * gate_ref[...].astype(F32)).astype(o_ref.dtype)


def lru_scan(proj, conv_w, conv_b, w_a, w_x, b_a, b_x, lam, *, tt=256):
    B, T, W2 = proj.shape
    W = W2 // 2
    nblk = W // LRU_BLOCK_DIM
    taps = conv_w.shape[0]
    tt = _pick(T, tt, SUBLANES)
    tile = pl.BlockSpec((None, tt, W), lambda b, i: (b, i, 0))
    xtile = pl.BlockSpec((None, tt, W), lambda b, i: (b, i, 1))
    wspec = pl.BlockSpec((nblk, LRU_BLOCK_DIM, LRU_BLOCK_DIM), lambda b, i: (0, 0, 0))
    vec = pl.BlockSpec((1, W), lambda b, i: (0, 0))
    return pl.pallas_call(
        functools.partial(_lru_kernel, tt=tt, nblk=nblk, taps=taps),
        grid=(B, T // tt),
        in_specs=[xtile, tile, pl.BlockSpec((taps, W), lambda b, i: (0, 0)), vec,
                  wspec, wspec, vec, vec, vec],
        out_specs=tile,
        out_shape=jax.ShapeDtypeStruct((B, T, W), BF16),
        scratch_shapes=[
            pltpu.VMEM((tt, W), F32),
            pltpu.VMEM((tt, W), F32),
            pltpu.VMEM((1, W), F32),
            pltpu.VMEM((tt, W), F32),
            pltpu.VMEM((SUBLANES, W), F32),
            pltpu.VMEM((tt + SUBLANES, W), F32),
        ],
        compiler_params=_params("parallel", "arbitrary"),
        name="lru_scan",
    )(proj, proj, conv_w, conv_b, w_a, w_x, b_a, b_x, lam)


def _row(v):
    return v.reshape(1, -1).astype(F32)


def sb_mixer(x2, B, T, norm_g, w_in, q_norm, k_norm, w_out):
    hd = w_in.shape[1] // 3
    w_in = w_in.astype(BF16)
    scale = HEAD_DIM ** -0.5
    tn = _pick(hd, 1024, LANES)
    nt = hd // tn
    q_ep, k_ep = _ep_headnorm(scale, False), _ep_headnorm(1.0, False)
    qkv = norm_matmul(x2, norm_g, w_in, tn=tn, out_dtype=BF16, name="sb_proj",
                      const_extras=(_row(q_norm), _row(k_norm)),
                      epilogue=[(0, nt, lambda y, gq, gk: q_ep(y, gq)),
                                (nt, 2 * nt, lambda y, gq, gk: k_ep(y, gk)),
                                (2 * nt, 3 * nt, None)])
    o = sb_attention(qkv.reshape(B, T, 3 * hd))
    return matmul_residual(o.reshape(B * T, hd), w_out.astype(BF16), x2, name="sb_out")


def gdn_mixer(x2, B, T, norm_g, w_in, conv_w, a_log, dt_bias, o_norm, w_out):
    M = B * T
    Hv = a_log.shape[0]
    vd = Hv * HEAD_DIM
    kd = (conv_w.shape[1] - vd) // 2
    main = 2 * kd + 2 * vd
    proj = norm_matmul(x2, norm_g, w_in.astype(BF16), col0=0, ncols=main, name="gdn_proj")
    w_ba = jnp.pad(w_in[:, main:], ((0, 0), (0, LANES - 2 * Hv))).astype(BF16)
    pad = lambda p: jnp.pad(p.astype(F32), (Hv, LANES - 2 * Hv)).reshape(1, LANES)
    gates = norm_matmul(x2, norm_g, w_ba, epilogue=_ep_gdn_gates(Hv),
                        const_extras=(pad(a_log), pad(dt_bias)), name="gdn_gates")
    N = T // GDN_CHUNK
    per_head = lambda a: a.reshape(B, T, Hv).transpose(0, 2, 1).reshape(B, Hv, N, 1, GDN_CHUNK)
    beta = per_head(gates[:, :Hv])
    g = per_head(gates[:, Hv:2 * Hv])
    o = gdn_delta_rule(proj.reshape(B, T, main), conv_w, kd, vd, g, beta, _row(o_norm))
    return matmul_residual(o.reshape(M, vd), w_out.astype(BF16), x2, name="gdn_out")


def dsw_mixer(x2, B, T, norm_g, positions, w_in, q_norm, k_norm, w_out):
    M = B * T
    hd = w_in.shape[1] // 3
    w_in = w_in.astype(BF16)
    cos_t, sin_t = rope_tables(positions.reshape(M, 1).astype(jnp.int32))
    scale = HEAD_DIM ** -0.5
    tn = 3 * HEAD_DIM * 2
    q = norm_matmul(x2, norm_g, w_in, col0=0, ncols=hd, epilogue=_ep_headnorm(scale, True),
                    row_extras=(cos_t, sin_t), const_extras=(_row(q_norm),), out_dtype=F32, tn=tn,
                    pipelined=True, name="dsw_proj_q")
    k = norm_matmul(x2, norm_g, w_in, col0=hd, ncols=hd, epilogue=_ep_headnorm(1.0, True),
                    row_extras=(cos_t, sin_t), const_extras=(_row(k_norm),), out_dtype=F32, tn=tn,
                    pipelined=True, name="dsw_proj_k")
    v = norm_matmul(x2, norm_g, w_in, col0=2 * hd, ncols=hd, out_dtype=F32, tn=tn, name="dsw_proj_v")
    shp = (B, T, hd)
    q, k, v = q.reshape(shp), k.reshape(shp), v.reshape(shp)
    os_, lses = [], []
    for gi, (window, dilation) in enumerate(DSW_GROUPS):
        o_g, lse_g = band_attention(q, k, v, gi, window, dilation)
        os_.append(o_g.reshape(M, -1))
        lses.append(lse_g.reshape(M, -1))
    o = combine_groups(os_, lses)
    return matmul_residual(o, w_out.astype(BF16), x2, name="dsw_out")


def lru_mixer(x2, B, T, norm_g, w_in, conv_w, conv_b, w_a, b_a, w_x, b_x, lam, w_out):
    W = w_in.shape[1] // 2
    w_in = w_in.astype(BF16)
    tn = _pick(W, 1024, LANES)
    nt = W // tn
    proj = norm_matmul(x2, norm_g, w_in, tn=tn, name="lru_proj",
                       epilogue=[(0, nt, _ep_gelu), (nt, 2 * nt, None)])
    y = lru_scan(proj.reshape(B, T, 2 * W), conv_w, _row(conv_b),
                 w_a.astype(BF16), w_x.astype(BF16), _row(b_a), _row(b_x), _row(lam))
    return matmul_residual(y.reshape(B * T, W), w_out.astype(BF16), x2, name="lru_out")


def kernel(x, positions, mix_norm, ffn_norm, ffn_w_gu, ffn_w_down, sb_w_in, sb_q_norm, sb_k_norm, sb_w_out, gdn_w_in, gdn_conv_w, gdn_a_log, gdn_dt_bias, gdn_o_norm, gdn_w_out, dsw_w_in, dsw_q_norm, dsw_k_norm, dsw_w_out, lru_w_in, lru_conv_w, lru_conv_b, lru_w_a, lru_b_a, lru_w_x, lru_b_x, lru_lambda, lru_w_out):
    B, T, D = x.shape
    depth = mix_norm.shape[0]
    x2 = x.reshape(B * T, D)
    for i in range(depth):
        kind, j = i % 4, i // 4
        if kind == 0:
            x2 = sb_mixer(x2, B, T, mix_norm[i], sb_w_in[j], sb_q_norm[j], sb_k_norm[j], sb_w_out[j])
        elif kind == 1:
            x2 = gdn_mixer(x2, B, T, mix_norm[i], gdn_w_in[j], gdn_conv_w[j], gdn_a_log[j],
                           gdn_dt_bias[j], gdn_o_norm[j], gdn_w_out[j])
        elif kind == 2:
            x2 = dsw_mixer(x2, B, T, mix_norm[i], positions, dsw_w_in[j], dsw_q_norm[j],
                           dsw_k_norm[j], dsw_w_out[j])
        else:
            x2 = lru_mixer(x2, B, T, mix_norm[i], lru_w_in[j], lru_conv_w[j], lru_conv_b[j],
                           lru_w_a[j], lru_b_a[j], lru_w_x[j], lru_b_x[j], lru_lambda[j], lru_w_out[j])
        x2 = ffn(x2, ffn_norm[i], ffn_w_gu[i].astype(BF16), ffn_w_down[i].astype(BF16))
    return x2.reshape(B, T, D)
```

```python
import functools
import math

import jax
import jax.numpy as jnp
from jax import lax
from jax.experimental import pallas as pl
from jax.experimental.pallas import tpu as pltpu

F32 = jnp.float32
BF16 = jnp.bfloat16

NORM_EPS = 1e-6
HEAD_DIM = 128
LANES = 128
SUBLANES = 8
VMEM_LIMIT_BYTES = 56 * 1024 * 1024

GDN_CONV = 4
GDN_CHUNK = 64
DSW_GROUPS = ((128, 1), (512, 4), (2048, 16))
DSW_HEADS_PER_GROUP = 6
DSW_BLOCK = 128
ROPE_DIM = HEAD_DIM // 4
ROPE_THETA = 500000.0
LRU_BLOCK_DIM = 256
LRU_C = 8.0
NEG_BIG = -1e30


def _pick(dim, pref, align):
    if dim <= pref:
        return dim
    t = (pref // align) * align
    while t >= align:
        if dim % t == 0:
            return t
        t -= align
    return dim


def _params(*sem):
    return pltpu.CompilerParams(dimension_semantics=sem, vmem_limit_bytes=VMEM_LIMIT_BYTES)


def _softplus(x):
    return jnp.maximum(x, 0.0) + jnp.log(1.0 + jnp.exp(-jnp.abs(x)))


def _sigmoid(x):
    return 1.0 / (1.0 + jnp.exp(-x))


def _silu(x):
    return x * _sigmoid(x)


def _rms(x, g):
    ms = jnp.mean(x * x, axis=-1, keepdims=True)
    return x * lax.rsqrt(ms + NORM_EPS) * g


def _norm_matmul_kernel(*refs, epilogue, n_extra):
    x_ref, g_ref, w_ref = refs[:3]
    extra = refs[3:3 + n_extra]
    o_ref, xn_ref = refs[3 + n_extra], refs[4 + n_extra]
    j = pl.program_id(1)

    @pl.when(j == 0)
    def _():
        xn_ref[...] = _rms(x_ref[...], g_ref[...]).astype(BF16)

    y = jnp.dot(xn_ref[...], w_ref[...], preferred_element_type=F32)
    ex = [e[...] for e in extra]
    if isinstance(epilogue, list):
        for lo, hi, fn in epilogue:
            @pl.when((j >= lo) & (j < hi))
            def _(fn=fn):
                o_ref[...] = (y if fn is None else fn(y, *ex)).astype(o_ref.dtype)
        return
    if epilogue is not None:
        y = epilogue(y, *ex)
    o_ref[...] = y.astype(o_ref.dtype)


def norm_matmul(x, g, w, *, col0=0, ncols=None, epilogue=None, row_extras=(), const_extras=(),
                out_dtype=F32, tm=1024, tn=1024, pipelined=False, name="norm_matmul"):
    M, K = x.shape
    ncols = w.shape[1] - col0 if ncols is None else ncols
    tm = _pick(M, tm, SUBLANES)
    tn = _pick(ncols, tn, LANES)
    assert col0 % tn == 0 and ncols % tn == 0 and M % tm == 0
    off = col0 // tn
    in_specs = [
        pl.BlockSpec((tm, K), lambda i, j: (i, 0)),
        pl.BlockSpec((1, K), lambda i, j: (0, 0)),
        pl.BlockSpec((K, tn), lambda i, j: (0, j + off)),
    ]
    args = [x, g.reshape(1, K), w]
    for e in row_extras:
        in_specs.append(pl.BlockSpec((tm, e.shape[1]), lambda i, j: (i, 0)))
        args.append(e)
    for e in const_extras:
        in_specs.append(pl.BlockSpec(e.shape, lambda i, j: (0, 0)))
        args.append(e)
    n_extra = len(row_extras) + len(const_extras)
    if pipelined:
        return _norm_matmul_pipelined(x, g, w, off, ncols, epilogue, row_extras, const_extras,
                                      out_dtype, tm, tn, name)
    return pl.pallas_call(
        functools.partial(_norm_matmul_kernel, epilogue=epilogue, n_extra=n_extra),
        grid=(M // tm, ncols // tn),
        in_specs=in_specs,
        out_specs=pl.BlockSpec((tm, tn), lambda i, j: (i, j)),
        out_shape=jax.ShapeDtypeStruct((M, ncols), out_dtype),
        scratch_shapes=[pltpu.VMEM((tm, K), BF16)],
        compiler_params=_params("parallel", "arbitrary"),
        name=name,
    )(*args)


def _norm_matmul_pipe_kernel(*refs, epilogue, n_extra, ncol, ntiles):
    x_ref, g_ref, w_ref = refs[:3]
    extra = refs[3:3 + n_extra]
    o_ref, xn_ref, y_ref = refs[3 + n_extra:6 + n_extra]
    t = pl.program_id(0)

    @pl.when(t == 0)
    def _():
        y_ref[1] = jnp.zeros(y_ref.shape[1:], F32)

    @pl.when((t % ncol == 0) & (t < ntiles))
    def _():
        xn_ref[...] = _rms(x_ref[...], g_ref[...]).astype(BF16)

    slot = t % 2
    o_ref[...] = epilogue(y_ref[1 - slot], *[e[...] for e in extra]).astype(o_ref.dtype)
    y_ref[slot] = jnp.dot(xn_ref[...], w_ref[...], preferred_element_type=F32)


def _norm_matmul_pipelined(x, g, w, off, ncols, epilogue, row_extras, const_extras, out_dtype,
                           tm, tn, name):
    M, K = x.shape
    ncol = ncols // tn
    ntiles = (M // tm) * ncol
    cur = lambda t: jnp.minimum(t, ntiles - 1)
    prv = lambda t: jnp.maximum(t - 1, 0)
    in_specs = [
        pl.BlockSpec((tm, K), lambda t: (cur(t) // ncol, 0)),
        pl.BlockSpec((1, K), lambda t: (0, 0)),
        pl.BlockSpec((K, tn), lambda t: (0, cur(t) % ncol + off)),
    ]
    args = [x, g.reshape(1, K), w]
    for e in row_extras:
        in_specs.append(pl.BlockSpec((tm, e.shape[1]), lambda t: (prv(t) // ncol, 0)))
        args.append(e)
    for e in const_extras:
        in_specs.append(pl.BlockSpec(e.shape, lambda t: (0, 0)))
        args.append(e)
    n_extra = len(row_extras) + len(const_extras)
    return pl.pallas_call(
        functools.partial(_norm_matmul_pipe_kernel, epilogue=epilogue, n_extra=n_extra,
                          ncol=ncol, ntiles=ntiles),
        grid=(ntiles + 1,),
        in_specs=in_specs,
        out_specs=pl.BlockSpec((tm, tn), lambda t: (prv(t) // ncol, prv(t) % ncol)),
        out_shape=jax.ShapeDtypeStruct((M, ncols), out_dtype),
        scratch_shapes=[pltpu.VMEM((tm, K), BF16), pltpu.VMEM((2, tm, tn), F32)],
        compiler_params=_params("arbitrary"),
        name=name,
    )(*args)


def _ep_headnorm(scale, rope):
    half = ROPE_DIM // 2

    def ep(y, *extra):
        yhs = [y[:, h * HEAD_DIM:(h + 1) * HEAD_DIM] for h in range(y.shape[1] // HEAD_DIM)]
        invs = [lax.rsqrt(jnp.mean(yh * yh, axis=-1, keepdims=True) + NORM_EPS) * scale for yh in yhs]
        if rope:
            cos_t, sin_t, gain = extra
            lane = lax.broadcasted_iota(jnp.int32, cos_t.shape, 1)
            gain_b = jnp.broadcast_to(gain, cos_t.shape)
            cg = cos_t * gain
            s_lo = jnp.where(lane < half, pltpu.roll(gain_b, HEAD_DIM - half, 1) * sin_t, 0.0)
            s_hi = jnp.where(lane >= half, pltpu.roll(gain_b, half, 1) * sin_t, 0.0)
            lo = [pltpu.roll(yh, HEAD_DIM - half, 1) for yh in yhs]
            hi = [pltpu.roll(yh, half, 1) for yh in yhs]
            outs = [(yh * cg + a * s_lo + b * s_hi) * inv for yh, a, b, inv in zip(yhs, lo, hi, invs)]
        else:
            (gain,) = extra
            outs = [yh * inv * gain for yh, inv in zip(yhs, invs)]
        return outs[0] if len(outs) == 1 else jnp.concatenate(outs, axis=1)

    return ep


def _ep_gelu(y):
    c = math.sqrt(2.0 / math.pi)
    return 0.5 * y * (1.0 + jnp.tanh(c * (y + 0.044715 * (y * y * y))))


def _ep_gdn_gates(n_heads):
    def ep(y, a_log, dt_bias):
        lane = lax.broadcasted_iota(jnp.int32, y.shape, 1)
        beta = _sigmoid(y)
        g = -jnp.exp(a_log) * _softplus(y + dt_bias)
        return jnp.where(lane < n_heads, beta, g)

    return ep


def _matmul_res_kernel(a_ref, w_ref, r_ref, o_ref):
    y = jnp.dot(a_ref[...].astype(BF16), w_ref[...], preferred_element_type=F32)
    o_ref[...] = r_ref[...] + y


def matmul_residual(a, w, res, *, tm=1024, name="out_proj"):
    M, K = a.shape
    N = w.shape[1]
    tm = _pick(M, tm, SUBLANES)
    tn = _pick(N, 1024 if K <= 2560 else 512, LANES)
    return pl.pallas_call(
        _matmul_res_kernel,
        grid=(M // tm, N // tn),
        in_specs=[
            pl.BlockSpec((tm, K), lambda i, j: (i, 0)),
            pl.BlockSpec((K, tn), lambda i, j: (0, j)),
            pl.BlockSpec((tm, tn), lambda i, j: (i, j)),
        ],
        out_specs=pl.BlockSpec((tm, tn), lambda i, j: (i, j)),
        out_shape=jax.ShapeDtypeStruct((M, N), F32),
        compiler_params=_params("parallel", "arbitrary"),
        name=name,
    )(a, w, res)


def _ffn_kernel(x_ref, g_ref, wg_ref, wu_ref, wd_ref, o_ref, xn_ref, *, sub):
    @pl.when(pl.program_id(1) == 0)
    def _():
        x = x_ref[...]
        xn_ref[...] = _rms(x, g_ref[...]).astype(BF16)
        o_ref[...] = x

    xn = xn_ref[...]
    tf = wg_ref.shape[1]
    hs = []
    for c in range(tf // sub):
        cs = slice(c * sub, (c + 1) * sub)
        gate = jnp.dot(xn, wg_ref[:, cs], preferred_element_type=F32)
        up = jnp.dot(xn, wu_ref[:, cs], preferred_element_type=F32)
        hs.append((_silu(gate) * up).astype(BF16))
    h = hs[0] if len(hs) == 1 else jnp.concatenate(hs, axis=1)
    o_ref[...] += jnp.dot(h, wd_ref[...], preferred_element_type=F32)


def ffn(x, g, w_gu, w_down, *, tm=512, tf=512, sub=256):
    M, D = x.shape
    Fh = w_down.shape[0]
    tm = _pick(M, tm, SUBLANES)
    tf = _pick(Fh, tf, LANES)
    nf = Fh // tf
    return pl.pallas_call(
        functools.partial(_ffn_kernel, sub=min(sub, tf)),
        grid=(M // tm, nf),
        in_specs=[
            pl.BlockSpec((tm, D), lambda i, j: (i, 0)),
            pl.BlockSpec((1, D), lambda i, j: (0, 0)),
            pl.BlockSpec((D, tf), lambda i, j: (0, j)),
            pl.BlockSpec((D, tf), lambda i, j: (0, j + nf)),
            pl.BlockSpec((tf, D), lambda i, j: (j, 0)),
        ],
        out_specs=pl.BlockSpec((tm, D), lambda i, j: (i, 0)),
        out_shape=jax.ShapeDtypeStruct((M, D), F32),
        scratch_shapes=[pltpu.VMEM((tm, D), BF16)],
        compiler_params=_params("parallel", "arbitrary"),
        name="ffn",
    )(x, g.reshape(1, D), w_gu, w_gu, w_down)


SB_STOP = 88.0


def _sb_kernel(q_ref, k_ref, v_ref, o_ref, *, blk, heads, n_q):
    W = 2 * blk
    row = lax.broadcasted_iota(jnp.int32, (blk, W), 0)
    col = lax.broadcasted_iota(jnp.int32, (blk, W), 1)
    r2 = lax.broadcasted_iota(jnp.int32, (W, W), 0)
    c2 = lax.broadcasted_iota(jnp.int32, (W, W), 1)
    suffix_w = jnp.where(r2 >= c2, 1.0, 0.0).astype(BF16)
    rb = lax.broadcasted_iota(jnp.int32, (blk, blk), 0)
    cb = lax.broadcasted_iota(jnp.int32, (blk, blk), 1)
    suffix_b = jnp.where(rb >= cb, 1.0, 0.0).astype(BF16)

    def dot_nt(a, b):
        return lax.dot_general(a, b, (((1,), (1,)), ((), ())), preferred_element_type=F32)

    def suffix_sum(sp, suffix):
        hi = sp.astype(BF16)
        lo = (sp - hi.astype(F32)).astype(BF16)
        return (jnp.dot(hi, suffix, preferred_element_type=F32)
                + jnp.dot(lo, suffix, preferred_element_type=F32))

    def qblock(i, carry):
        q0 = pl.multiple_of(i * blk, blk)
        w0 = pl.multiple_of(jnp.maximum(i - 1, 0) * blk, blk)
        valid = col < row + (q0 - w0)
        hsl = [slice(h * HEAD_DIM, (h + 1) * HEAD_DIM) for h in range(heads)]
        qs = [q_ref[pl.ds(q0, blk), sl] for sl in hsl]
        zs = [dot_nt(q, k_ref[pl.ds(w0, W), sl]) for q, sl in zip(qs, hsl)]
        sps = [jnp.where(valid, _softplus(z), 0.0) for z in zs]
        cums = [suffix_sum(sp, suffix_w) for sp in sps]
        ws = [jnp.exp(jnp.where(valid, z - cum, NEG_BIG)).astype(BF16) for z, cum in zip(zs, cums)]
        accs = [jnp.dot(w, v_ref[pl.ds(w0, W), sl], preferred_element_type=F32)
                for w, sl in zip(ws, hsl)]
        runs = [cum[:, 0:1] for cum in cums]
        def cond(c):
            least = functools.reduce(jnp.minimum, c[2])
            return (c[0] >= 0) & (jnp.min(least) < SB_STOP)

        def body(c):
            j, acc, run = c
            s0 = pl.multiple_of(j * blk, blk)
            z = [dot_nt(q, k_ref[pl.ds(s0, blk), sl]) for q, sl in zip(qs, hsl)]
            cum = [suffix_sum(_softplus(zh), suffix_b) for zh in z]
            w = [jnp.exp(zh - ch - rh).astype(BF16) for zh, ch, rh in zip(z, cum, run)]
            acc = tuple(a + jnp.dot(wh, v_ref[pl.ds(s0, blk), sl], preferred_element_type=F32)
                        for a, wh, sl in zip(acc, w, hsl))
            return j - 1, acc, tuple(rh + ch[:, 0:1] for rh, ch in zip(run, cum))

        _, accs, _ = lax.while_loop(cond, body, (i - 2, tuple(accs), tuple(runs)))
        for acc, sl in zip(accs, hsl):
            o_ref[pl.ds(q0, blk), sl] = acc.astype(o_ref.dtype)
        return carry

    lax.fori_loop(0, n_q, qblock, 0)


def sb_attention(qkv, *, blk=128, heads=4):
    B, T, HD3 = qkv.shape
    HD = HD3 // 3
    H = HD // HEAD_DIM
    assert T % blk == 0 and T >= 2 * blk and H % heads == 0
    nh = H // heads
    seq = lambda off: pl.BlockSpec((None, T, heads * HEAD_DIM), lambda b, h: (b, 0, h + off))
    return pl.pallas_call(
        functools.partial(_sb_kernel, blk=blk, heads=heads, n_q=T // blk),
        grid=(B, nh),
        in_specs=[seq(0), seq(nh), seq(2 * nh)],
        out_specs=seq(0),
        out_shape=jax.ShapeDtypeStruct((B, T, HD), BF16),
        compiler_params=_params("parallel", "parallel"),
        name="sb_attention",
    )(qkv, qkv, qkv)


def _bdot(a, b):
    return jnp.dot(a.astype(BF16), b.astype(BF16), preferred_element_type=F32)


def _bdot_nt(a, b):
    return lax.dot_general(a.astype(BF16), b.astype(BF16), (((1,), (1,)), ((), ())),
                           preferred_element_type=F32)


def _bdot_tn(a, b):
    return lax.dot_general(a.astype(BF16), b.astype(BF16), (((0,), (0,)), ((), ())),
                           preferred_element_type=F32)


def _gdn_kernel(xq_ref, xk_ref, xv_ref, z_ref, cwq_ref, cwk_ref, cwv_ref, g_ref, beta_ref, gain_ref,
                o_ref, q_ref, k_ref, v_ref, hq_ref, hk_ref, hv_ref, buf_ref,
                u_ref, wq_ref, kd_ref, attn_ref, gl_ref, out_ref, state_ref,
                *, chunk, kheads, rep, group, taps):
    C = chunk

    @pl.when(pl.program_id(2) == 0)
    def _():
        state_ref[...] = jnp.zeros_like(state_ref)
        hq_ref[...] = jnp.zeros_like(hq_ref)
        hk_ref[...] = jnp.zeros_like(hk_ref)
        hv_ref[...] = jnp.zeros_like(hv_ref)

    def conv_silu(x_ref, halo_ref, w_ref, dst_ref, l2_scale):
        n, width = x_ref.shape
        buf_ref[0:SUBLANES, 0:width] = halo_ref[...]
        buf_ref[SUBLANES:SUBLANES + n, 0:width] = x_ref[...]
        halo_ref[...] = x_ref[n - SUBLANES:n, :]
        w = w_ref[...]
        y = w[taps - 1:taps, :] * x_ref[...]
        for kk in range(taps - 1):
            shift = taps - 1 - kk
            y = y + w[kk:kk + 1, :] * buf_ref[SUBLANES - shift:SUBLANES - shift + n, 0:width]
        y = _silu(y)
        for h in range(width // HEAD_DIM):
            sl = slice(h * HEAD_DIM, (h + 1) * HEAD_DIM)
            yh = y[:, sl]
            if l2_scale is not None:
                ss = jnp.sum(yh * yh, axis=-1, keepdims=True)
                yh = yh * (lax.rsqrt(ss + NORM_EPS) * l2_scale)
            dst_ref[:, sl] = yh

    conv_silu(xq_ref, hq_ref, cwq_ref, q_ref, HEAD_DIM ** -0.5)
    conv_silu(xk_ref, hk_ref, cwk_ref, k_ref, 1.0)
    conv_silu(xv_ref, hv_ref, cwv_ref, v_ref, None)

    ri = lax.broadcasted_iota(jnp.int32, (C, C), 0)
    ci = lax.broadcasted_iota(jnp.int32, (C, C), 1)
    eye = ri == ci
    lower = ci <= ri
    strict = ci < ri
    gain = gain_ref[...]
    n_double = max(1, int(math.ceil(math.log2(C))) - 1)
    nv = kheads * rep
    heads = [slice(r * HEAD_DIM, (r + 1) * HEAD_DIM) for r in range(nv)]
    G = group
    rows = G * C
    bmm = functools.partial(jnp.einsum, "gij,gjk->gik", preferred_element_type=F32)
    bmm_nt = functools.partial(jnp.einsum, "gid,gjd->gij", preferred_element_type=F32)

    def prepare(khs):
        rs = [r for kh in khs for r in range(kh * rep, (kh + 1) * rep)]
        every = lambda f: [f(i) for i in range(len(rs))]
        qs = {kh: q_ref[:, heads[kh]].reshape(G, C, HEAD_DIM) for kh in khs}
        ks = {kh: k_ref[:, heads[kh]].reshape(G, C, HEAD_DIM) for kh in khs}
        k16s = {kh: ks[kh].astype(BF16) for kh in khs}
        qks = {kh: bmm_nt(qs[kh].astype(BF16), k16s[kh]) for kh in khs}
        q = every(lambda i: qs[rs[i] // rep])
        k = every(lambda i: ks[rs[i] // rep])
        k16 = every(lambda i: k16s[rs[i] // rep])
        g_row = every(lambda i: g_ref[rs[i]])
        beta_row = every(lambda i: beta_ref[rs[i]])
        g_col = every(lambda i: jnp.sum(jnp.where(eye, g_row[i], 0.0), axis=2, keepdims=True))
        beta_col = every(lambda i: jnp.sum(jnp.where(eye, beta_row[i], 0.0), axis=2, keepdims=True))
        gc_col = every(lambda i: jnp.sum(jnp.where(lower, g_row[i], 0.0), axis=2, keepdims=True))
        gc_row = every(lambda i: jnp.sum(jnp.where(ri <= ci, g_col[i], 0.0), axis=1, keepdims=True))
        g_last = every(lambda i: jnp.sum(g_row[i], axis=2, keepdims=True))
        decay = every(lambda i: jnp.where(
            lower, jnp.exp(jnp.where(lower, gc_col[i] - gc_row[i], 0.0)), 0.0))
        eg = every(lambda i: jnp.exp(gc_col[i]))
        kb = every(lambda i: k[i] * beta_col[i])
        a_mat = every(lambda i: jnp.where(strict, bmm_nt(kb[i].astype(BF16), k16[i]) * decay[i], 0.0))
        t_mat = every(lambda i: jnp.where(eye, 1.0, 0.0) - a_mat[i])
        p = a_mat
        for _ in range(n_double):
            p16 = every(lambda i: p[i].astype(BF16))
            p = every(lambda i: bmm(p16[i], p16[i]))
            t_mat = every(lambda i: t_mat[i] + bmm(t_mat[i].astype(BF16), p[i].astype(BF16)))
        rhs = every(lambda i: jnp.concatenate(
            [v_ref[:, heads[rs[i]]].reshape(G, C, HEAD_DIM) * beta_col[i], kb[i] * eg[i]],
            axis=2).astype(BF16))
        uw = every(lambda i: bmm(t_mat[i].astype(BF16), rhs[i]))
        for i, r in enumerate(rs):
            u_ref[r] = uw[i][:, :, :HEAD_DIM].reshape(rows, HEAD_DIM)
            wq_ref[r] = jnp.concatenate([uw[i][:, :, HEAD_DIM:], q[i] * eg[i]], axis=1).astype(
                BF16).reshape(2 * rows, HEAD_DIM)
            kd_ref[r] = (k[i] * jnp.exp(g_last[i] - gc_col[i])).astype(BF16).reshape(rows, HEAD_DIM)
            attn_ref[r] = (qks[r // rep] * decay[i]).astype(BF16).reshape(rows, C)
            gl_ref[r] = jnp.broadcast_to(g_last[i], (G, 1, HEAD_DIM))

    def recur(nl, states):
        ls = pl.multiple_of(nl * C, C)
        ls2 = pl.multiple_of(nl * 2 * C, 2 * C)
        hs = range(nv)
        s16 = [states[r].astype(BF16) for r in hs]
        ws = [jnp.dot(wq_ref[r, pl.ds(ls2, 2 * C), :], s16[r], preferred_element_type=F32) for r in hs]
        v_new = [(u_ref[r, pl.ds(ls, C), :] - ws[r][:C]).astype(BF16) for r in hs]
        upd = [lax.dot_general(kd_ref[r, pl.ds(ls, C), :], v_new[r], (((0,), (0,)), ((), ())),
                               preferred_element_type=F32) for r in hs]
        new = tuple(states[r] * jnp.exp(gl_ref[r, nl]) + upd[r] for r in hs)
        for r in hs:
            out_ref[r, pl.ds(ls, C), :] = ws[r][C:] + jnp.dot(
                attn_ref[r, pl.ds(ls, C), :], v_new[r], preferred_element_type=F32)
        return new

    for k0 in range(0, kheads, 2):
        prepare(list(range(k0, min(k0 + 2, kheads))))
    states = lax.fori_loop(0, group, recur, tuple(state_ref[r] for r in range(nv)))
    for r in range(nv):
        state_ref[r] = states[r]
        o_ref[:, heads[r]] = (_rms(out_ref[r], gain) * _silu(z_ref[:, heads[r]])).astype(o_ref.dtype)


def gdn_delta_rule(proj, conv_w, KD, VD, g, beta, gain, *, chunk=GDN_CHUNK, group=8, kheads=4):
    B, T, _ = proj.shape
    taps = conv_w.shape[0]
    Hk, Hv = KD // HEAD_DIM, VD // HEAD_DIM
    rep = Hv // Hk
    N = T // chunk
    group = _pick(N, group, 1)
    kheads = _pick(Hk, kheads, 1)
    nv = kheads * rep
    kw, vw = kheads * HEAD_DIM, nv * HEAD_DIM
    assert KD % kw == 0 and (2 * KD) % vw == 0 and N % group == 0
    koff, voff, zoff = KD // kw, 2 * KD // vw, (2 * KD + VD) // vw
    rows = group * chunk
    assert rows % SUBLANES == 0
    tile = lambda width, f: pl.BlockSpec((None, rows, width), f)
    tapw = lambda width, f: pl.BlockSpec((taps, width), f)
    gate = pl.BlockSpec((None, nv, group, 1, chunk), lambda b, h, t: (b, h, t, 0, 0))
    return pl.pallas_call(
        functools.partial(_gdn_kernel, chunk=chunk, kheads=kheads, rep=rep, group=group, taps=taps),
        grid=(B, Hk // kheads, N // group),
        in_specs=[
            tile(kw, lambda b, h, t: (b, t, h)),
            tile(kw, lambda b, h, t: (b, t, h + koff)),
            tile(vw, lambda b, h, t: (b, t, h + voff)),
            tile(vw, lambda b, h, t: (b, t, h + zoff)),
            tapw(kw, lambda b, h, t: (0, h)),
            tapw(kw, lambda b, h, t: (0, h + koff)),
            tapw(vw, lambda b, h, t: (0, h + voff)),
            gate,
            gate,
            pl.BlockSpec((1, HEAD_DIM), lambda b, h, t: (0, 0)),
        ],
        out_specs=tile(vw, lambda b, h, t: (b, t, h)),
        out_shape=jax.ShapeDtypeStruct((B, T, VD), BF16),
        scratch_shapes=[
            pltpu.VMEM((rows, kw), F32),
            pltpu.VMEM((rows, kw), F32),
            pltpu.VMEM((rows, vw), F32),
            pltpu.VMEM((SUBLANES, kw), F32),
            pltpu.VMEM((SUBLANES, kw), F32),
            pltpu.VMEM((SUBLANES, vw), F32),
            pltpu.VMEM((rows + SUBLANES, vw), F32),
            pltpu.VMEM((nv, rows, HEAD_DIM), F32),
            pltpu.VMEM((nv, 2 * rows, HEAD_DIM), BF16),
            pltpu.VMEM((nv, rows, HEAD_DIM), BF16),
            pltpu.VMEM((nv, rows, chunk), BF16),
            pltpu.VMEM((nv, group, 1, HEAD_DIM), F32),
            pltpu.VMEM((nv, rows, HEAD_DIM), F32),
            pltpu.VMEM((nv, HEAD_DIM, HEAD_DIM), F32),
        ],
        compiler_params=_params("parallel", "parallel", "arbitrary"),
        name="gdn_delta_rule",
    )(proj, proj, proj, proj, conv_w, conv_w, conv_w, g, beta, gain)


def _rope_table_kernel(pos_ref, freq_ref, sign_ref, cos_ref, sin_ref):
    ang = pos_ref[...].astype(F32) * freq_ref[...]
    cos_ref[...] = jnp.cos(ang)
    sin_ref[...] = jnp.sin(ang) * sign_ref[...]


def rope_tables(pos, *, tm=1024):
    M = pos.shape[0]
    tm = _pick(M, tm, SUBLANES)
    half = ROPE_DIM // 2
    lane = jnp.arange(HEAD_DIM)
    inv_freq = ROPE_THETA ** (-(lane % half).astype(F32) / half)
    freq = jnp.where(lane < ROPE_DIM, inv_freq, 0.0).reshape(1, HEAD_DIM).astype(F32)
    sign = jnp.where(lane < half, -1.0, 1.0).reshape(1, HEAD_DIM).astype(F32)
    spec = pl.BlockSpec((tm, HEAD_DIM), lambda i: (i, 0))
    const = pl.BlockSpec((1, HEAD_DIM), lambda i: (0, 0))
    return pl.pallas_call(
        _rope_table_kernel,
        grid=(M // tm,),
        in_specs=[pl.BlockSpec((tm, 1), lambda i: (i, 0)), const, const],
        out_specs=[spec, spec],
        out_shape=[jax.ShapeDtypeStruct((M, HEAD_DIM), F32)] * 2,
        compiler_params=_params("parallel"),
        name="rope_tables",
    )(pos, freq, sign)


def _band_kernel(q_ref, kc_ref, vc_ref, o_ref, lse_ref, kp_ref, vp_ref, *stage, **kw):
    @pl.when(pl.program_id(2) == 0)
    def _():
        kp_ref[...] = jnp.zeros_like(kp_ref)
        vp_ref[...] = jnp.zeros_like(vp_ref)

    _band_body(q_ref, kc_ref, kp_ref, vc_ref, vp_ref, o_ref, lse_ref, *stage, **kw)
    kp_ref[...] = kc_ref[...]
    vp_ref[...] = vc_ref[...]


def _band_body(q_ref, kc_ref, kp_ref, vc_ref, vp_ref, o_ref, lse_ref, *stage, blk, span, heads, d, batch):
    n = pl.program_id(2)
    qi = lax.broadcasted_iota(jnp.int32, (blk, 2 * blk), 0)
    kj = lax.broadcasted_iota(jnp.int32, (blk, 2 * blk), 1)
    steps = blk + qi - kj
    first_key = jnp.where(n > 0, 0, blk)
    valid = (steps >= 0) & (steps <= span) & (kj >= first_key)
    ins = (q_ref, kc_ref, kp_ref, vc_ref, vp_ref)

    def attend(part, load, store):
        s_list = []
        for key in part:
            q = load(0, key).astype(BF16)
            kw = jnp.concatenate([load(2, key), load(1, key)], axis=0).astype(BF16)
            s = lax.dot_general(q, kw, (((1,), (1,)), ((), ())), preferred_element_type=F32)
            s_list.append(jnp.where(valid, s, NEG_BIG))
        m_list = [jnp.max(s, axis=-1, keepdims=True) for s in s_list]
        p_list = [jnp.where(valid, jnp.exp(s - m), 0.0) for s, m in zip(s_list, m_list)]
        den_list = [jnp.sum(p, axis=-1, keepdims=True) for p in p_list]
        for key, p, m, den in zip(part, p_list, m_list, den_list):
            vw = jnp.concatenate([load(4, key), load(3, key)], axis=0).astype(BF16)
            o = jnp.dot((p / den).astype(BF16), vw, preferred_element_type=F32)
            store(key, o, jnp.broadcast_to(m + jnp.log(den), (blk, HEAD_DIM)))

    if d == 1:
        def load(a, h):
            return ins[a][:, h * HEAD_DIM:(h + 1) * HEAD_DIM]

        def store(h, o, lse):
            o_ref[:, h * HEAD_DIM:(h + 1) * HEAD_DIM] = o
            lse_ref[:, h * HEAD_DIM:(h + 1) * HEAD_DIM] = lse

        for b0 in range(0, heads, batch):
            attend(list(range(b0, min(b0 + batch, heads))), load, store)
        return

    if heads == 1:
        def load(a, r):
            return ins[a][pl.ds(r, blk, stride=d), :]

        def store(r, o, lse):
            o_ref[pl.ds(r, blk, stride=d), :] = o
            lse_ref[pl.ds(r, blk, stride=d), :] = lse

        for b0 in range(0, d, batch):
            attend(list(range(b0, min(b0 + batch, d))), load, store)
        return

    stage_in, stage_o, stage_l = stage[:5], stage[5], stage[6]
    for h in range(heads):
        sl = slice(h * HEAD_DIM, (h + 1) * HEAD_DIM)
        for a in range(5):
            stage_in[a][...] = ins[a][:, sl]

        def load(a, r):
            return stage_in[a][pl.ds(r, blk, stride=d), :]

        def store(r, o, lse):
            stage_o[pl.ds(r, blk, stride=d), :] = o
            stage_l[pl.ds(r, blk, stride=d), :] = lse

        for b0 in range(0, d, batch):
            attend(list(range(b0, min(b0 + batch, d))), load, store)
        o_ref[:, sl] = stage_o[...]
        lse_ref[:, sl] = stage_l[...]


def band_attention(q, k, v, group, window, dilation, *, blk=DSW_BLOCK, group_heads=DSW_HEADS_PER_GROUP):
    B, T, HD = q.shape
    d = dilation
    span = window // d
    unit = d * blk
    assert span <= blk and T % unit == 0
    heads = group_heads if unit <= 512 else 1
    assert group_heads % heads == 0
    hw = heads * HEAD_DIM
    col0 = group * group_heads // heads
    cur = pl.BlockSpec((None, unit, hw), lambda b, c, n: (b, n, col0 + c))
    prev = pl.BlockSpec((None, unit, hw), lambda b, c, n: (b, jnp.maximum(n - 1, 0), col0 + c))
    out = pl.BlockSpec((None, unit, hw), lambda b, c, n: (b, n, c))
    return pl.pallas_call(
        functools.partial(_band_kernel, blk=blk, span=span, heads=heads, d=d, batch=6 if d == 1 else 4),
        grid=(B, group_heads // heads, T // unit),
        in_specs=[cur, cur, cur],
        out_specs=[out, out],
        out_shape=[jax.ShapeDtypeStruct((B, T, group_heads * HEAD_DIM), F32)] * 2,
        scratch_shapes=[pltpu.VMEM((unit, hw), F32)] * 2 + (
            [pltpu.VMEM((unit, HEAD_DIM), F32)] * 7 if d > 1 and heads > 1 else []),
        compiler_params=_params("parallel", "parallel", "arbitrary"),
        name=f"band_attention_d{d}",
    )(q, k, v)


def _combine_kernel(*refs, n):
    o_refs, l_refs, out_ref = refs[:n], refs[n:2 * n], refs[2 * n]
    ls = [r[...] for r in l_refs]
    m = functools.reduce(jnp.maximum, ls)
    es = [jnp.exp(l - m) for l in ls]
    inv = 1.0 / functools.reduce(lambda a, b: a + b, es)
    w = o_refs[0].shape[1]
    for gi in range(n):
        out_ref[:, gi * w:(gi + 1) * w] = (o_refs[gi][...] * (es[gi] * inv)).astype(out_ref.dtype)


def combine_groups(os_, lses, *, tm=512):
    n = len(os_)
    M, w = os_[0].shape
    tm = _pick(M, tm, SUBLANES)
    spec = pl.BlockSpec((tm, w), lambda i: (i, 0))
    return pl.pallas_call(
        functools.partial(_combine_kernel, n=n),
        grid=(M // tm,),
        in_specs=[spec] * (2 * n),
        out_specs=pl.BlockSpec((tm, n * w), lambda i: (i, 0)),
        out_shape=jax.ShapeDtypeStruct((M, n * w), BF16),
        compiler_params=_params("parallel"),
        name="combine_groups",
    )(*os_, *lses)


def _lru_kernel(x_ref, gate_ref, cw_ref, cb_ref, wa_ref, wx_ref, ba_ref, bx_ref, lam_ref, o_ref,
                a_ref, u_ref, h_ref, xc_ref, halo_ref, buf_ref, *, tt, nblk, taps):
    @pl.when(pl.program_id(1) == 0)
    def _():
        h_ref[...] = jnp.zeros_like(h_ref)
        halo_ref[...] = jnp.zeros_like(halo_ref)

    buf_ref[0:SUBLANES, :] = halo_ref[...]
    buf_ref[SUBLANES:SUBLANES + tt, :] = x_ref[...]
    halo_ref[...] = x_ref[tt - SUBLANES:tt, :]
    cw = cw_ref[...]
    xc = cb_ref[...] + cw[taps - 1:taps, :] * x_ref[...]
    for kk in range(taps - 1):
        shift = taps - 1 - kk
        xc = xc + cw[kk:kk + 1, :] * buf_ref[SUBLANES - shift:SUBLANES - shift + tt, :]
    xc_ref[...] = xc

    for nb in range(nblk):
        sl = slice(nb * LRU_BLOCK_DIM, (nb + 1) * LRU_BLOCK_DIM)
        xb = xc_ref[:, sl]
        xb16 = xb.astype(BF16)
        r = _sigmoid(jnp.dot(xb16, wa_ref[nb], preferred_element_type=F32) + ba_ref[:, sl])
        ig = _sigmoid(jnp.dot(xb16, wx_ref[nb], preferred_element_type=F32) + bx_ref[:, sl])
        log_a = (-LRU_C) * r * _softplus(-lam_ref[:, sl])
        a = jnp.exp(log_a)
        a_ref[:, sl] = a
        u_ref[:, sl] = jnp.sqrt(1.0 - a * a) * (ig * xb)

    def group(gi, h):
        base = pl.multiple_of(gi * SUBLANES, SUBLANES)
        for r8 in range(SUBLANES):
            h = a_ref[pl.ds(base + r8, 1), :] * h + u_ref[pl.ds(base + r8, 1), :]
            u_ref[pl.ds(base + r8, 1), :] = h
        return h

    h_ref[...] = lax.fori_loop(0, tt // SUBLANES, group, h_ref[...])
    o_ref[...] = (u_ref[...] * gate_ref[...].astype(F32)).astype(o_ref.dtype)


def lru_scan(proj, conv_w, conv_b, w_a, w_x, b_a, b_x, lam, *, tt=256):
    B, T, W2 = proj.shape
    W = W2 // 2
    nblk = W // LRU_BLOCK_DIM
    taps = conv_w.shape[0]
    tt = _pick(T, tt, SUBLANES)
    tile = pl.BlockSpec((None, tt, W), lambda b, i: (b, i, 0))
    xtile = pl.BlockSpec((None, tt, W), lambda b, i: (b, i, 1))
    wspec = pl.BlockSpec((nblk, LRU_BLOCK_DIM, LRU_BLOCK_DIM), lambda b, i: (0, 0, 0))
    vec = pl.BlockSpec((1, W), lambda b, i: (0, 0))
    return pl.pallas_call(
        functools.partial(_lru_kernel, tt=tt, nblk=nblk, taps=taps),
        grid=(B, T // tt),
        in_specs=[xtile, tile, pl.BlockSpec((taps, W), lambda b, i: (0, 0)), vec,
                  wspec, wspec, vec, vec, vec],
        out_specs=tile,
        out_shape=jax.ShapeDtypeStruct((B, T, W), BF16),
        scratch_shapes=[
            pltpu.VMEM((tt, W), F32),
            pltpu.VMEM((tt, W), F32),
            pltpu.VMEM((1, W), F32),
            pltpu.VMEM((tt, W), F32),
            pltpu.VMEM((SUBLANES, W), F32),
            pltpu.VMEM((tt + SUBLANES, W), F32),
        ],
        compiler_params=_params("parallel", "arbitrary"),
        name="lru_scan",
    )(proj, proj, conv_w, conv_b, w_a, w_x, b_a, b_x, lam)


def _row(v):
    return v.reshape(1, -1).astype(F32)


def sb_mixer(x2, B, T, norm_g, w_in, q_norm, k_norm, w_out):
    hd = w_in.shape[1] // 3
    w_in = w_in.astype(BF16)
    scale = HEAD_DIM ** -0.5
    tn = _pick(hd, 1024, LANES)
    nt = hd // tn
    q_ep, k_ep = _ep_headnorm(scale, False), _ep_headnorm(1.0, False)
    qkv = norm_matmul(x2, norm_g, w_in, tn=tn, out_dtype=BF16, name="sb_proj",
                      const_extras=(_row(q_norm), _row(k_norm)),
                      epilogue=[(0, nt, lambda y, gq, gk: q_ep(y, gq)),
                                (nt, 2 * nt, lambda y, gq, gk: k_ep(y, gk)),
                                (2 * nt, 3 * nt, None)])
    o = sb_attention(qkv.reshape(B, T, 3 * hd))
    return matmul_residual(o.reshape(B * T, hd), w_out.astype(BF16), x2, name="sb_out")


def gdn_mixer(x2, B, T, norm_g, w_in, conv_w, a_log, dt_bias, o_norm, w_out):
    M = B * T
    Hv = a_log.shape[0]
    vd = Hv * HEAD_DIM
    kd = (conv_w.shape[1] - vd) // 2
    main = 2 * kd + 2 * vd
    proj = norm_matmul(x2, norm_g, w_in.astype(BF16), col0=0, ncols=main, name="gdn_proj")
    w_ba = jnp.pad(w_in[:, main:], ((0, 0), (0, LANES - 2 * Hv))).astype(BF16)
    pad = lambda p: jnp.pad(p.astype(F32), (Hv, LANES - 2 * Hv)).reshape(1, LANES)
    gates = norm_matmul(x2, norm_g, w_ba, epilogue=_ep_gdn_gates(Hv),
                        const_extras=(pad(a_log), pad(dt_bias)), name="gdn_gates")
    N = T // GDN_CHUNK
    per_head = lambda a: a.reshape(B, T, Hv).transpose(0, 2, 1).reshape(B, Hv, N, 1, GDN_CHUNK)
    beta = per_head(gates[:, :Hv])
    g = per_head(gates[:, Hv:2 * Hv])
    o = gdn_delta_rule(proj.reshape(B, T, main), conv_w, kd, vd, g, beta, _row(o_norm))
    return matmul_residual(o.reshape(M, vd), w_out.astype(BF16), x2, name="gdn_out")


def dsw_mixer(x2, B, T, norm_g, positions, w_in, q_norm, k_norm, w_out):
    M = B * T
    hd = w_in.shape[1] // 3
    w_in = w_in.astype(BF16)
    cos_t, sin_t = rope_tables(positions.reshape(M, 1).astype(jnp.int32))
    scale = HEAD_DIM ** -0.5
    tn = 3 * HEAD_DIM * 2
    q = norm_matmul(x2, norm_g, w_in, col0=0, ncols=hd, epilogue=_ep_headnorm(scale, True),
                    row_extras=(cos_t, sin_t), const_extras=(_row(q_norm),), out_dtype=F32, tn=tn,
                    pipelined=True, name="dsw_proj_q")
    k = norm_matmul(x2, norm_g, w_in, col0=hd, ncols=hd, epilogue=_ep_headnorm(1.0, True),
                    row_extras=(cos_t, sin_t), const_extras=(_row(k_norm),), out_dtype=F32, tn=tn,
                    pipelined=True, name="dsw_proj_k")
    v = norm_matmul(x2, norm_g, w_in, col0=2 * hd, ncols=hd, out_dtype=F32, tn=tn, name="dsw_proj_v")
    shp = (B, T, hd)
    q, k, v = q.reshape(shp), k.reshape(shp), v.reshape(shp)
    os_, lses = [], []
    for gi, (window, dilation) in enumerate(DSW_GROUPS):
        o_g, lse_g = band_attention(q, k, v, gi, window, dilation)
        os_.append(o_g.reshape(M, -1))
        lses.append(lse_g.reshape(M, -1))
    o = combine_groups(os_, lses)
    return matmul_residual(o, w_out.astype(BF16), x2, name="dsw_out")


def lru_mixer(x2, B, T, norm_g, w_in, conv_w, conv_b, w_a, b_a, w_x, b_x, lam, w_out):
    W = w_in.shape[1] // 2
    w_in = w_in.astype(BF16)
    tn = _pick(W, 1024, LANES)
    nt = W // tn
    proj = norm_matmul(x2, norm_g, w_in, tn=tn, name="lru_proj",
                       epilogue=[(0, nt, _ep_gelu), (nt, 2 * nt, None)])
    y = lru_scan(proj.reshape(B, T, 2 * W), conv_w, _row(conv_b),
                 w_a.astype(BF16), w_x.astype(BF16), _row(b_a), _row(b_x), _row(lam))
    return matmul_residual(y.reshape(B * T, W), w_out.astype(BF16), x2, name="lru_out")


def kernel(x, positions, mix_norm, ffn_norm, ffn_w_gu, ffn_w_down, sb_w_in, sb_q_norm, sb_k_norm, sb_w_out, gdn_w_in, gdn_conv_w, gdn_a_log, gdn_dt_bias, gdn_o_norm, gdn_w_out, dsw_w_in, dsw_q_norm, dsw_k_norm, dsw_w_out, lru_w_in, lru_conv_w, lru_conv_b, lru_w_a, lru_b_a, lru_w_x, lru_b_x, lru_lambda, lru_w_out):
    B, T, D = x.shape
    depth = mix_norm.shape[0]
    x2 = x.reshape(B * T, D)
    for i in range(depth):
        kind, j = i % 4, i // 4
        if kind == 0:
            x2 = sb_mixer(x2, B, T, mix_norm[i], sb_w_in[j], sb_q_norm[j], sb_k_norm[j], sb_w_out[j])
        elif kind == 1:
            x2 = gdn_mixer(x2, B, T, mix_norm[i], gdn_w_in[j], gdn_conv_w[j], gdn_a_log[j],
                           gdn_dt_bias[j], gdn_o_norm[j], gdn_w_out[j])
        elif kind == 2:
            x2 = dsw_mixer(x2, B, T, mix_norm[i], positions, dsw_w_in[j], dsw_q_norm[j],
                           dsw_k_norm[j], dsw_w_out[j])
        else:
            x2 = lru_mixer(x2, B, T, mix_norm[i], lru_w_in[j], lru_conv_w[j], lru_conv_b[j],
                           lru_w_a[j], lru_b_a[j], lru_w_x[j], lru_b_x[j], lru_lambda[j], lru_w_out[j])
        x2 = ffn(x2, ffn_norm[i], ffn_w_gu[i].astype(BF16), ffn_w_down[i].astype(BF16))
    return x2.reshape(B, T, D)
```
